```python
import jax, jax.numpy as jnp
from jax import lax
import numpy as np

D_MODEL = 2048
BATCH = 2
SEQ = 8192
DEPTH = 4

A_HEADS = 8
A_DK = 128
A_DV = 128
A_WIDTH = A_HEADS * A_DK
B_HEADS = 8
B_HEAD_DIM = 128
B_WIDTH = B_HEADS * B_HEAD_DIM
AB_IN = 4 * A_WIDTH + 4 * B_WIDTH + B_HEADS
AB_CAT = A_HEADS * A_DV + B_WIDTH
C_HEADS = 4
C_DK = 256
C_DV = 512
C_KW = C_HEADS * C_DK
C_VW = C_HEADS * C_DV
C_GATE_RANK = 16
C_GATE_NORMALIZER = 16.0
C_IN = 2 * C_KW + 2 * C_VW + C_GATE_RANK
D_FF = 256 * (-(-8 * D_MODEL // (3 * 256)))
N_AB = (DEPTH + 1) // 2
N_C = DEPTH // 2
CHUNK = 64
Q_BLOCK = 128
EPS = 1e-6
MASK_VALUE = -1e30
MIN_FORGET = 1e-30

kernel_name = "hybrid_hgrn2_fox_gla_adaln_trunk"


def rms_norm(x, gain):
    xf = x.astype(jnp.float32)
    y = xf * lax.rsqrt(jnp.mean(xf * xf, axis=-1, keepdims=True) + EPS)
    return (y * gain.astype(jnp.float32)).astype(x.dtype)


def modulate(x, shift, scale):
    return x * (1 + scale[:, None, :]) + shift[:, None, :]


def split_heads(x, n_heads):
    b, t, _ = x.shape
    return x.reshape(b, t, n_heads, -1).transpose(0, 2, 1, 3)


def merge_heads(x):
    b, h, t, d = x.shape
    return x.transpose(0, 2, 1, 3).reshape(b, t, h * d)


def chunk_gated_linear_recurrence(q, k, v, log_g):
    b, h, t, dk = q.shape
    dv = v.shape[-1]
    n = t // CHUNK

    def to_chunks(a):
        return jnp.moveaxis(a.astype(jnp.float32).reshape(b, h, n, CHUNK, a.shape[-1]), 2, 0)

    causal = jnp.tril(jnp.ones((CHUNK, CHUNK), dtype=bool))[:, :, None]

    def step(state, inp):
        qc, kc, vc, gc = inp
        cum = jnp.cumsum(gc, axis=-2)
        o_inter = jnp.einsum('bhtd,bhde->bhte', qc * jnp.exp(cum), state)
        diff = cum[..., :, None, :] - cum[..., None, :, :]
        decay = jnp.where(causal, jnp.exp(jnp.where(causal, diff, 0.0)), 0.0)
        scores = jnp.einsum('bhtd,bhsd,bhtsd->bhts', qc, kc, decay)
        o_intra = jnp.einsum('bhts,bhse->bhte', scores, vc)
        last = cum[..., -1:, :]
        state = jnp.exp(last[..., 0, :])[..., None] * state + jnp.einsum(
            'bhsd,bhse->bhde', kc * jnp.exp(last - cum), vc)
        return state, o_inter + o_intra

    state0 = jnp.zeros((b, h, dk, dv), jnp.float32)
    _, o = lax.scan(step, state0, (to_chunks(q), to_chunks(k), to_chunks(v), to_chunks(log_g)))
    return jnp.moveaxis(o, 0, 2).reshape(b, h, t, dv)


def forgetting_attention(q, k, v, log_f):
    b, h, t, d = q.shape
    nb = t // Q_BLOCK
    cum_f = jnp.cumsum(log_f.astype(jnp.float32), axis=-1)
    qf = q.astype(jnp.float32) * (d ** -0.5)
    kf = k.astype(jnp.float32)
    vf = v.astype(jnp.float32)
    q_blocks = jnp.moveaxis(qf.reshape(b, h, nb, Q_BLOCK, d), 2, 0)
    f_blocks = jnp.moveaxis(cum_f.reshape(b, h, nb, Q_BLOCK), 2, 0)
    key_pos = jnp.arange(t)

    def block(args):
        qb, fb, start = args
        s = jnp.einsum('bhqd,bhkd->bhqk', qb, kf) + fb[..., :, None] - cum_f[..., None, :]
        q_pos = start + jnp.arange(Q_BLOCK)
        s = jnp.where(key_pos[None, :] <= q_pos[:, None], s, MASK_VALUE)
        p = jax.nn.softmax(s, axis=-1)
        return jnp.einsum('bhqk,bhkd->bhqd', p, vf)

    o = lax.map(block, (q_blocks, f_blocks, jnp.arange(nb) * Q_BLOCK))
    return jnp.moveaxis(o, 0, 2).reshape(b, h, t, d)


def hgrn2_fox_mixer(h, w_in, w_out, lower_bound, a_out_gain, q_gain, k_gain, f_bias):
    proj = h @ w_in
    cuts = [A_WIDTH, 2 * A_WIDTH, 3 * A_WIDTH, 4 * A_WIDTH,
            4 * A_WIDTH + B_WIDTH, 4 * A_WIDTH + 2 * B_WIDTH,
            4 * A_WIDTH + 3 * B_WIDTH, 4 * A_WIDTH + 4 * B_WIDTH]
    qa, fa, ia, ga, qb, kb, vb, gb, fb = jnp.split(proj, cuts, axis=-1)
    z = split_heads(fa, A_HEADS).astype(jnp.float32)
    lb = lower_bound.astype(jnp.float32).reshape(A_HEADS, 1, A_DK)
    sig = jax.nn.sigmoid(z)
    forget = lb + (1 - lb) * sig
    log_forget = jnp.log(jnp.maximum(forget, MIN_FORGET))
    key = (1 - lb) * (1 - sig)
    o_a = chunk_gated_linear_recurrence(jax.nn.silu(split_heads(qa, A_HEADS)), key,
                                        split_heads(ia, A_HEADS), log_forget)
    o_a = rms_norm(o_a, a_out_gain[:, None, :])
    o_a = merge_heads(o_a).astype(h.dtype) * jax.nn.silu(ga)
    q = rms_norm(split_heads(qb, B_HEADS), q_gain)
    k = rms_norm(split_heads(kb, B_HEADS), k_gain)
    log_f = jax.nn.log_sigmoid(fb.astype(jnp.float32) + f_bias).transpose(0, 2, 1)
    o_b = forgetting_attention(q, k, split_heads(vb, B_HEADS), log_f)
    o_b = merge_heads(o_b).astype(h.dtype) * jax.nn.sigmoid(gb)
    return jnp.concatenate([o_a, o_b], axis=-1) @ w_out


def gla_mixer(h, w_in, w_gate_up, b_gate, out_gain, w_out):
    proj = h @ w_in
    cuts = [C_KW, 2 * C_KW, 2 * C_KW + C_VW, 2 * C_KW + 2 * C_VW]
    q, k, v, g, g_low = jnp.split(proj, cuts, axis=-1)
    log_alpha = jax.nn.log_sigmoid((g_low @ w_gate_up).astype(jnp.float32) + b_gate) / C_GATE_NORMALIZER
    o = chunk_gated_linear_recurrence(split_heads(q, C_HEADS) * (C_DK ** -0.5),
                                      split_heads(k, C_HEADS), split_heads(v, C_HEADS),
                                      split_heads(log_alpha, C_HEADS))
    o = rms_norm(o, out_gain)
    o = merge_heads(o).astype(h.dtype) * jax.nn.silu(g)
    return o @ w_out


def swiglu(h, w_in, w_out):
    a, u = jnp.split(h @ w_in, 2, axis=-1)
    return (jax.nn.silu(a) * u) @ w_out


def setup_inputs(seed: int = 0) -> dict:
    key = jax.random.key(seed)
    ks = jax.random.split(key, 24)

    def normal(k, shape, std):
        return jax.random.normal(k, shape, jnp.float32) * std

    def gain(k, shape):
        return 1.0 + normal(k, shape, 0.02)

    return {
        "x": normal(ks[0], (BATCH, SEQ, D_MODEL), 1.0),
        "c": normal(ks[1], (BATCH, D_MODEL), 1.0),
        "mod_w": normal(ks[2], (DEPTH, D_MODEL, 6 * D_MODEL), 0.5 * D_MODEL ** -0.5),
        "mod_b": normal(ks[3], (DEPTH, 6 * D_MODEL), 0.02),
        "norm_mix_gain": gain(ks[4], (DEPTH, D_MODEL)),
        "norm_ffn_gain": gain(ks[5], (DEPTH, D_MODEL)),
        "ab_w_in": normal(ks[6], (N_AB, D_MODEL, AB_IN), D_MODEL ** -0.5),
        "ab_w_out": normal(ks[7], (N_AB, AB_CAT, D_MODEL), AB_CAT ** -0.5),
        "hgrn_lb_logits": normal(ks[8], (N_AB, A_WIDTH), 1.0),
        "hgrn_out_gain": gain(ks[9], (N_AB, A_HEADS, A_DV)),
        "fox_q_gain": gain(ks[10], (N_AB, B_HEAD_DIM)),
        "fox_k_gain": gain(ks[11], (N_AB, B_HEAD_DIM)),
        "fox_f_bias": jax.random.uniform(ks[12], (N_AB, B_HEADS), jnp.float32, 1.0, 4.0),
        "gla_w_in": normal(ks[13], (N_C, D_MODEL, C_IN), D_MODEL ** -0.5),
        "gla_w_gate_up": normal(ks[14], (N_C, C_GATE_RANK, C_KW), C_GATE_RANK ** -0.5),
        "gla_b_gate": normal(ks[15], (N_C, C_KW), 0.02),
        "gla_out_gain": gain(ks[16], (N_C, C_DV)),
        "gla_w_out": normal(ks[17], (N_C, C_VW, D_MODEL), C_VW ** -0.5),
        "ffn_w_in": normal(ks[18], (DEPTH, D_MODEL, 2 * D_FF), D_MODEL ** -0.5),
        "ffn_w_out": normal(ks[19], (DEPTH, D_FF, D_MODEL), D_FF ** -0.5),
    }


def reference(x, c, mod_w, mod_b, norm_mix_gain, norm_ffn_gain, ab_w_in, ab_w_out,
              hgrn_lb_logits, hgrn_out_gain, fox_q_gain, fox_k_gain, fox_f_bias,
              gla_w_in, gla_w_gate_up, gla_b_gate, gla_out_gain, gla_w_out,
              ffn_w_in, ffn_w_out):
    probs = jax.nn.softmax(hgrn_lb_logits.astype(jnp.float32), axis=0)
    lower_bounds = jnp.concatenate(
        [jnp.zeros_like(probs[:1]), jnp.cumsum(probs[1:], axis=0)], axis=0)[:N_AB]
    lower_bounds = jnp.clip(lower_bounds, 0.0, 1.0 - 1e-6)
    silu_c = jax.nn.silu(c)
    for layer in range(DEPTH):
        mod = silu_c @ mod_w[layer] + mod_b[layer]
        sh1, sc1, g1, sh2, sc2, g2 = jnp.split(mod, 6, axis=-1)
        h = modulate(rms_norm(x, norm_mix_gain[layer]), sh1, sc1)
        if layer % 2 == 0:
            i = layer // 2
            y = hgrn2_fox_mixer(h, ab_w_in[i], ab_w_out[i], lower_bounds[i], hgrn_out_gain[i],
                                fox_q_gain[i], fox_k_gain[i], fox_f_bias[i])
        else:
            j = layer // 2
            y = gla_mixer(h, gla_w_in[j], gla_w_gate_up[j], gla_b_gate[j], gla_out_gain[j], gla_w_out[j])
        x = x + g1[:, None, :] * y
        h = modulate(rms_norm(x, norm_ffn_gain[layer]), sh2, sc2)
        x = x + g2[:, None, :] * swiglu(h, ffn_w_in[layer], ffn_w_out[layer])
    return x
```

```python
import functools

import jax
import jax.numpy as jnp
from jax import lax
from jax.experimental import pallas as pl
from jax.experimental.pallas import tpu as pltpu

F32 = jnp.float32
BF16 = jnp.bfloat16

EPS = 1e-6
MIN_FORGET = 1e-30
MASK_VALUE = -1e30
GLA_GATE_NORMALIZER = 16.0

LANES = 128
VMEM_LIMIT = 56 * 1024 * 1024

A_HEADS = 8
A_DK = 128
B_HEADS = 8
B_HEAD_DIM = 128
C_HEADS = 4
C_DK = 256
C_DV = 512
C_GATE_RANK = 16

PROJ_TM = 512
PROJ_TN = 1024
FFN_TM = 512
FFN_TF = 512
REC_TB = 512
REC_CHUNK = 64
HGRN_SUB = 16
GLA_SUB = 64
ATT_TQ = 512
ATT_TK = 512
PREP_TB = 512


def _params(*sem):
    return pltpu.CompilerParams(dimension_semantics=sem, vmem_limit_bytes=VMEM_LIMIT)


def _sigmoid(x):
    return 1.0 / (1.0 + jnp.exp(-x))


def _silu(x):
    return x * _sigmoid(x)


def _log_sigmoid(x):
    return jnp.minimum(x, 0.0) - jnp.log1p(jnp.exp(-jnp.abs(x)))


def _split3(x):
    hi = x.astype(BF16)
    r = x - hi.astype(F32)
    mid = r.astype(BF16)
    lo = (r - mid.astype(F32)).astype(BF16)
    return hi, mid, lo


def _dot(a, b):
    return jnp.dot(a, b, preferred_element_type=F32)


def _dot_nt(a, b):
    return lax.dot_general(a, b, (((1,), (1,)), ((), ())), preferred_element_type=F32)


def _dot_tn(a, b):
    return lax.dot_general(a, b, (((0,), (0,)), ((), ())), preferred_element_type=F32)


def _lower_bound_kernel(logit_ref, o_ref):
    z = logit_ref[...]
    n = z.shape[0]
    e = jnp.exp(z - jnp.max(z, axis=0, keepdims=True))
    p = e / jnp.sum(e, axis=0, keepdims=True)
    run = jnp.zeros_like(p[0:1])
    rows = [run]
    for k in range(1, n):
        run = run + p[k:k + 1]
        rows.append(run)
    o_ref[...] = jnp.clip(jnp.concatenate(rows, axis=0), 0.0, 1.0 - 1e-6)


def lower_bounds(logits):
    return pl.pallas_call(
        _lower_bound_kernel,
        out_shape=jax.ShapeDtypeStruct(logits.shape, F32),
        name="hgrn_lower_bounds",
    )(logits.astype(F32))


def _mod_kernel(c_ref, w_ref, b_ref, o_ref):
    s = _silu(c_ref[...])
    o_ref[0] = jnp.dot(s, w_ref[0], preferred_element_type=F32,
                       precision=lax.Precision.HIGHEST) + b_ref[0]


def modulation(c, mod_w, mod_b, tn=1024):
    n_layers, d, n = mod_w.shape
    b = c.shape[0]
    rows = 8
    c_pad = jnp.zeros((rows, d), F32).at[:b].set(c)
    out = pl.pallas_call(
        _mod_kernel,
        grid=(n_layers, n // tn),
        in_specs=[
            pl.BlockSpec((rows, d), lambda l, j: (0, 0)),
            pl.BlockSpec((1, d, tn), lambda l, j: (l, 0, j)),
            pl.BlockSpec((1, 1, tn), lambda l, j: (l, 0, j)),
        ],
        out_specs=pl.BlockSpec((1, rows, tn), lambda l, j: (l, 0, j)),
        out_shape=jax.ShapeDtypeStruct((n_layers, rows, n), F32),
        compiler_params=_params("parallel", "parallel"),
        name="adaln_modulation",
    )(c_pad, mod_w, mod_b.reshape(n_layers, 1, n))
    return out[:, :b]


def _norm_modulate(x, gain, shift, scale):
    var = jnp.mean(x * x, axis=-1, keepdims=True)
    y = x * lax.rsqrt(var + EPS) * gain
    return y * (1.0 + scale) + shift


def _norm_proj_kernel(x_ref, gain_ref, sh_ref, sc_ref, w_ref, waux_ref, o_ref, oaux_ref, h_ref):
    @pl.when(pl.program_id(1) == 0)
    def _():
        h = _norm_modulate(x_ref[...], gain_ref[...], sh_ref[0], sc_ref[0]).astype(BF16)
        h_ref[...] = h
        oaux_ref[...] = _dot(h, waux_ref[...])

    o_ref[...] = _dot(h_ref[...], w_ref[...])


def norm_proj(x, gain, shift, scale, w, w_aux, seq, tm=PROJ_TM, tn=PROJ_TN):
    m, d = x.shape
    n = w.shape[1]
    tm = min(tm, seq)
    per_batch = seq // tm
    return pl.pallas_call(
        _norm_proj_kernel,
        grid=(m // tm, n // tn),
        in_specs=[
            pl.BlockSpec((tm, d), lambda i, j: (i, 0)),
            pl.BlockSpec((1, d), lambda i, j: (0, 0)),
            pl.BlockSpec((1, 1, d), lambda i, j: (i // per_batch, 0, 0)),
            pl.BlockSpec((1, 1, d), lambda i, j: (i // per_batch, 0, 0)),
            pl.BlockSpec((d, tn), lambda i, j: (0, j)),
            pl.BlockSpec((d, LANES), lambda i, j: (0, 0)),
        ],
        out_specs=[
            pl.BlockSpec((tm, tn), lambda i, j: (i, j)),
            pl.BlockSpec((tm, LANES), lambda i, j: (i, 0)),
        ],
        out_shape=[
            jax.ShapeDtypeStruct((m, n), F32),
            jax.ShapeDtypeStruct((m, LANES), F32),
        ],
        scratch_shapes=[pltpu.VMEM((tm, d), BF16)],
        compiler_params=_params("parallel", "arbitrary"),
        name="norm_in_proj",
    )(x, gain.reshape(1, d), shift, scale, w, w_aux)


def _out_proj_kernel(*refs, n_lhs):
    lhs = refs[:n_lhs]
    ws = refs[n_lhs:2 * n_lhs]
    x_ref, gate_ref, o_ref = refs[2 * n_lhs:]
    y = _dot(lhs[0][...], ws[0][...])
    for a_ref, w_ref in zip(lhs[1:], ws[1:]):
        y = y + _dot(a_ref[...], w_ref[...])
    o_ref[...] = x_ref[...] + gate_ref[0] * y


def out_proj_residual(lhs_list, w_list, x, gate, seq, tm=PROJ_TM, tn=PROJ_TN):
    m, d = x.shape
    tm = min(tm, seq)
    per_batch = seq // tm
    n_lhs = len(lhs_list)
    in_specs = (
        [pl.BlockSpec((tm, a.shape[1]), lambda i, j: (i, 0)) for a in lhs_list]
        + [pl.BlockSpec((w.shape[0], tn), lambda i, j: (0, j)) for w in w_list]
        + [pl.BlockSpec((tm, tn), lambda i, j: (i, j)),
           pl.BlockSpec((1, 1, tn), lambda i, j: (i // per_batch, 0, j))]
    )
    return pl.pallas_call(
        functools.partial(_out_proj_kernel, n_lhs=n_lhs),
        grid=(m // tm, d // tn),
        in_specs=in_specs,
        out_specs=pl.BlockSpec((tm, tn), lambda i, j: (i, j)),
        out_shape=jax.ShapeDtypeStruct((m, d), F32),
        compiler_params=_params("parallel", "parallel"),
        name="out_proj_residual",
    )(*lhs_list, *w_list, x, gate)


def _ffn_kernel(x_ref, gain_ref, sh_ref, sc_ref, gate_ref, wa_ref, wu_ref, wo_ref, o_ref,
                h_ref, acc_ref):
    f = pl.program_id(1)

    @pl.when(f == 0)
    def _():
        h_ref[...] = _norm_modulate(x_ref[...], gain_ref[...], sh_ref[0], sc_ref[0]).astype(BF16)

    h = h_ref[...]
    a = _dot(h, wa_ref[...])
    u = _dot(h, wu_ref[...])
    part = _dot((_silu(a) * u).astype(BF16), wo_ref[...])

    @pl.when(f == 0)
    def _():
        acc_ref[...] = part

    @pl.when(f > 0)
    def _():
        acc_ref[...] += part

    @pl.when(f == pl.num_programs(1) - 1)
    def _():
        o_ref[...] = x_ref[...] + gate_ref[0] * acc_ref[...]


def ffn_residual(x, gain, shift, scale, gate, w_in, w_out, seq, tm=FFN_TM, tf=FFN_TF):
    m, d = x.shape
    d_ff = w_out.shape[0]
    tm = min(tm, seq)
    per_batch = seq // tm
    nf = d_ff // tf
    vec = pl.BlockSpec((1, 1, d), lambda i, f: (i // per_batch, 0, 0))
    return pl.pallas_call(
        _ffn_kernel,
        grid=(m // tm, nf),
        in_specs=[
            pl.BlockSpec((tm, d), lambda i, f: (i, 0)),
            pl.BlockSpec((1, d), lambda i, f: (0, 0)),
            vec, vec, vec,
            pl.BlockSpec((d, tf), lambda i, f: (0, f)),
            pl.BlockSpec((d, tf), lambda i, f: (0, nf + f)),
            pl.BlockSpec((tf, d), lambda i, f: (f, 0)),
        ],
        out_specs=pl.BlockSpec((tm, d), lambda i, f: (i, 0)),
        out_shape=jax.ShapeDtypeStruct((m, d), F32),
        scratch_shapes=[pltpu.VMEM((tm, d), BF16), pltpu.VMEM((tm, d), F32)],
        compiler_params=_params("parallel", "arbitrary"),
        name="swiglu_ffn_residual",
    )(x, gain.reshape(1, d), shift, scale, gate, w_in, w_in, w_out)


def _glr_chunk(q, k, v, g, s_ref, tril, sub):
    c, dk = q.shape
    dv = v.shape[1]
    cum = sum(_dot(tril, p) for p in _split3(g))
    state = s_ref[...]
    vb = v.astype(BF16)
    o = _dot((q * jnp.exp(cum)).astype(BF16), state.astype(BF16))
    last = cum[c - 1:c, :]
    k_state = (k * jnp.exp(last - cum)).astype(BF16)
    update = _dot_tn(k_state, vb)
    decay_cols = jnp.transpose(jnp.broadcast_to(jnp.exp(last), (LANES, dk)))
    s_ref[...] = jnp.concatenate([decay_cols] * (dv // LANES), axis=1) * state + update

    blocks = []
    for r0 in range(0, c, sub):
        r1 = r0 + sub
        hi = cum[r1 - 1:r1, :]
        mid = 0.5 * hi if r0 == 0 else 0.5 * (cum[r0 - 1:r0, :] + hi)
        qi = (q[r0:r1] * jnp.exp(cum[r0:r1] - mid)).astype(BF16)
        ki = (k[:r1] * jnp.exp(mid - cum[:r1])).astype(BF16)
        a = _dot_nt(qi, ki)
        row = r0 + lax.broadcasted_iota(jnp.int32, (sub, r1), 0)
        col = lax.broadcasted_iota(jnp.int32, (sub, r1), 1)
        a = jnp.where(col <= row, a, 0.0)
        blocks.append(_dot(a.astype(BF16), vb[:r1]))
    intra = blocks[0] if len(blocks) == 1 else jnp.concatenate(blocks, axis=0)
    return o + intra


def _tril_bf16(c):
    row = lax.broadcasted_iota(jnp.int32, (c, c), 0)
    col = lax.broadcasted_iota(jnp.int32, (c, c), 1)
    return jnp.where(col <= row, 1.0, 0.0).astype(BF16)


def _hgrn_kernel(q_ref, f_ref, i_ref, g_ref, lb_ref, gain_ref, o_ref, s_ref, *, chunk, sub):
    @pl.when(pl.program_id(2) == 0)
    def _():
        s_ref[...] = jnp.zeros_like(s_ref)

    lb = lb_ref[0]
    gain = gain_ref[0]
    tril = _tril_bf16(chunk)

    def body(ci, carry):
        rows = pl.ds(pl.multiple_of(ci * chunk, chunk), chunk)
        sig = _sigmoid(f_ref[rows, :])
        forget = lb + (1.0 - lb) * sig
        g = jnp.log(jnp.maximum(forget, MIN_FORGET))
        key = (1.0 - lb) * (1.0 - sig)
        o = _glr_chunk(_silu(q_ref[rows, :]), key, i_ref[rows, :], g, s_ref, tril, sub)
        o = o * lax.rsqrt(jnp.mean(o * o, axis=-1, keepdims=True) + EPS) * gain
        o_ref[rows, :] = (o * _silu(g_ref[rows, :])).astype(o_ref.dtype)
        return carry

    lax.fori_loop(0, q_ref.shape[0] // chunk, body, 0)


def hgrn2_heads(proj, lb, out_gain, batch, seq, tb=REC_TB, chunk=REC_CHUNK, sub=HGRN_SUB):
    m = proj.shape[0]
    tb = min(tb, seq)
    nt = seq // tb
    h, dk = A_HEADS, A_DK

    def col(group):
        return pl.BlockSpec((tb, dk), lambda b, hh, t: (b * nt + t, group * h + hh))

    vec = pl.BlockSpec((1, 1, dk), lambda b, hh, t: (hh, 0, 0))
    return pl.pallas_call(
        functools.partial(_hgrn_kernel, chunk=chunk, sub=sub),
        grid=(batch, h, nt),
        in_specs=[col(0), col(1), col(2), col(3), vec, vec],
        out_specs=pl.BlockSpec((tb, dk), lambda b, hh, t: (b * nt + t, hh)),
        out_shape=jax.ShapeDtypeStruct((m, h * dk), BF16),
        scratch_shapes=[pltpu.VMEM((dk, dk), F32)],
        compiler_params=_params("parallel", "parallel", "arbitrary"),
        name="hgrn2_recurrence",
    )(proj, proj, proj, proj, lb.reshape(h, 1, dk), out_gain.reshape(h, 1, dk))


def _gla_kernel(q_ref, k_ref, v_ref, g_ref, low_ref, wup_ref, bias_ref, gain_ref, o_ref, s_ref,
                *, chunk, sub):
    @pl.when(pl.program_id(2) == 0)
    def _():
        s_ref[...] = jnp.zeros_like(s_ref)

    dk = q_ref.shape[1]
    tril = _tril_bf16(chunk)
    wup = wup_ref[...]
    bias = bias_ref[0]
    gain = gain_ref[...]
    q_scale = dk ** -0.5

    def body(ci, carry):
        rows = pl.ds(pl.multiple_of(ci * chunk, chunk), chunk)
        z = _dot(low_ref[rows, :].astype(BF16), wup) + bias
        log_alpha = _log_sigmoid(z) * (1.0 / GLA_GATE_NORMALIZER)
        o = _glr_chunk(q_ref[rows, :] * q_scale, k_ref[rows, :], v_ref[rows, :], log_alpha,
                       s_ref, tril, sub)
        o = o * lax.rsqrt(jnp.mean(o * o, axis=-1, keepdims=True) + EPS) * gain
        o_ref[rows, :] = (o * _silu(g_ref[rows, :])).astype(o_ref.dtype)
        return carry

    lax.fori_loop(0, q_ref.shape[0] // chunk, body, 0)


def gla_heads(proj, low, w_up, b_gate, out_gain, batch, seq, tb=REC_TB, chunk=REC_CHUNK,
              sub=GLA_SUB):
    m = proj.shape[0]
    tb = min(tb, seq)
    nt = seq // tb
    h, dk, dv = C_HEADS, C_DK, C_DV
    kw = h * dk
    v0 = 2 * kw // dv
    g0 = v0 + h
    return pl.pallas_call(
        functools.partial(_gla_kernel, chunk=chunk, sub=sub),
        grid=(batch, h, nt),
        in_specs=[
            pl.BlockSpec((tb, dk), lambda b, hh, t: (b * nt + t, hh)),
            pl.BlockSpec((tb, dk), lambda b, hh, t: (b * nt + t, h + hh)),
            pl.BlockSpec((tb, dv), lambda b, hh, t: (b * nt + t, v0 + hh)),
            pl.BlockSpec((tb, dv), lambda b, hh, t: (b * nt + t, g0 + hh)),
            pl.BlockSpec((tb, LANES), lambda b, hh, t: (b * nt + t, 0)),
            pl.BlockSpec((LANES, dk), lambda b, hh, t: (0, hh)),
            pl.BlockSpec((1, 1, dk), lambda b, hh, t: (hh, 0, 0)),
            pl.BlockSpec((1, dv), lambda b, hh, t: (0, 0)),
        ],
        out_specs=pl.BlockSpec((tb, dv), lambda b, hh, t: (b * nt + t, hh)),
        out_shape=jax.ShapeDtypeStruct((m, h * dv), BF16),
        scratch_shapes=[pltpu.VMEM((dk, dv), F32)],
        compiler_params=_params("parallel", "parallel", "arbitrary"),
        name="gla_recurrence",
    )(proj, proj, proj, proj, low, w_up, b_gate.reshape(h, 1, dk), out_gain.reshape(1, dv))


def _fox_prep_kernel(q_ref, k_ref, v_ref, logit_ref, bias_ref, qg_ref, kg_ref,
                     qo_ref, ko_ref, vo_ref, f_ref, carry_ref, *, heads, head_dim):
    @pl.when(pl.program_id(1) == 0)
    def _():
        carry_ref[...] = jnp.zeros_like(carry_ref)

    tb = q_ref.shape[0]
    qg = qg_ref[...] * (head_dim ** -0.5)
    kg = kg_ref[...]
    for hh in range(heads):
        cols = slice(hh * head_dim, (hh + 1) * head_dim)
        q = q_ref[:, cols]
        qo_ref[:, cols] = (q * lax.rsqrt(jnp.mean(q * q, axis=-1, keepdims=True) + EPS) * qg
                           ).astype(qo_ref.dtype)
        k = k_ref[:, cols]
        ko_ref[:, cols] = (k * lax.rsqrt(jnp.mean(k * k, axis=-1, keepdims=True) + EPS) * kg
                           ).astype(ko_ref.dtype)
    vo_ref[...] = v_ref[...].astype(vo_ref.dtype)

    log_f = _log_sigmoid(logit_ref[...] + bias_ref[...])
    log_f_t = jnp.transpose(log_f)
    row = lax.broadcasted_iota(jnp.int32, (tb, tb), 0)
    col = lax.broadcasted_iota(jnp.int32, (tb, tb), 1)
    triu = jnp.where(row <= col, 1.0, 0.0).astype(BF16)
    cum = sum(_dot(p, triu) for p in _split3(log_f_t)) + carry_ref[:, 0:1]
    carry_ref[...] = jnp.broadcast_to(cum[:, tb - 1:tb], carry_ref.shape)
    f_ref[0] = cum[0:heads, :]


def fox_prep(proj, logits, f_bias, q_gain, k_gain, batch, seq, tb=PREP_TB):
    m = proj.shape[0]
    h, hd = B_HEADS, B_HEAD_DIM
    w = h * hd
    tb = min(tb, seq)
    nt = seq // tb
    base = (4 * A_HEADS * A_DK) // w
    bias = jnp.zeros((1, LANES), F32).at[0, :h].set(f_bias)

    def col(group):
        return pl.BlockSpec((tb, w), lambda b, t: (b * nt + t, base + group))

    out_tok = pl.BlockSpec((tb, w), lambda b, t: (b * nt + t, 0))
    return pl.pallas_call(
        functools.partial(_fox_prep_kernel, heads=h, head_dim=hd),
        grid=(batch, nt),
        in_specs=[
            col(0), col(1), col(2),
            pl.BlockSpec((tb, LANES), lambda b, t: (b * nt + t, 0)),
            pl.BlockSpec((1, LANES), lambda b, t: (0, 0)),
            pl.BlockSpec((1, hd), lambda b, t: (0, 0)),
            pl.BlockSpec((1, hd), lambda b, t: (0, 0)),
        ],
        out_specs=[out_tok, out_tok, out_tok,
                   pl.BlockSpec((1, h, tb), lambda b, t: (b, 0, t))],
        out_shape=[jax.ShapeDtypeStruct((m, w), BF16)] * 3
        + [jax.ShapeDtypeStruct((batch, h, seq), F32)],
        scratch_shapes=[pltpu.VMEM((LANES, LANES), F32)],
        compiler_params=_params("parallel", "arbitrary"),
        name="fox_prep",
    )(proj, proj, proj, logits, bias, q_gain.reshape(1, hd), k_gain.reshape(1, hd))


def _fox_attn_kernel(q_ref, k_ref, v_ref, f_ref, g_ref, o_ref, acc_ref, *, tq, tk):
    qi = pl.program_id(2)
    q = q_ref[...]
    q_start = pl.multiple_of(qi * tq, tq)
    f_base = f_ref[0, :, pl.ds(q_start, LANES)][:, 0:1]

    def scores(start):
        s = _dot_nt(q, k_ref[pl.ds(start, tk), :])
        return s - (f_ref[0, :, pl.ds(start, tk)] - f_base)

    def accumulate(s, start, m, l):
        m_new = jnp.maximum(m, jnp.max(s, axis=-1, keepdims=True))
        p = jnp.exp(s - m_new)
        alpha = jnp.exp(m - m_new)
        l_new = alpha * l + jnp.sum(p, axis=-1, keepdims=True)
        acc_ref[...] = alpha * acc_ref[...] + _dot(p.astype(BF16), v_ref[pl.ds(start, tk), :])
        return m_new, l_new

    def body(ki, carry):
        start = pl.multiple_of(ki * tk, tk)
        return accumulate(scores(start), start, *carry)

    acc_ref[...] = jnp.zeros_like(acc_ref)
    m0 = jnp.full((tq, 1), MASK_VALUE, F32)
    l0 = jnp.zeros((tq, 1), F32)
    m, l = lax.fori_loop(0, qi * (tq // tk), body, (m0, l0))
    for d in range(tq // tk):
        start = pl.multiple_of(q_start + d * tk, tk)
        row = lax.broadcasted_iota(jnp.int32, (tq, tk), 0)
        col = lax.broadcasted_iota(jnp.int32, (tq, tk), 1) + d * tk
        s = jnp.where(col <= row, scores(start), MASK_VALUE)
        m, l = accumulate(s, start, m, l)
    o_ref[...] = (acc_ref[...] / l * _sigmoid(g_ref[...])).astype(o_ref.dtype)


def fox_attention(qn, kn, vn, f_cum, proj, batch, seq, tq=ATT_TQ, tk=ATT_TK):
    m = qn.shape[0]
    h, hd = B_HEADS, B_HEAD_DIM
    tq = min(tq, seq)
    tk = min(tk, tq)
    nq = seq // tq
    gate_block = (4 * A_HEADS * A_DK + 3 * h * hd) // hd
    kv = pl.BlockSpec((seq, hd), lambda b, hh, i: (b, hh))
    return pl.pallas_call(
        functools.partial(_fox_attn_kernel, tq=tq, tk=tk),
        grid=(batch, h, nq),
        in_specs=[
            pl.BlockSpec((tq, hd), lambda b, hh, i: (b * nq + i, hh)),
            kv, kv,
            pl.BlockSpec((1, 1, seq), lambda b, hh, i: (b * h + hh, 0, 0)),
            pl.BlockSpec((tq, hd), lambda b, hh, i: (b * nq + i, gate_block + hh)),
        ],
        out_specs=pl.BlockSpec((tq, hd), lambda b, hh, i: (b * nq + i, hh)),
        out_shape=jax.ShapeDtypeStruct((m, h * hd), BF16),
        scratch_shapes=[pltpu.VMEM((tq, hd), F32)],
        compiler_params=_params("parallel", "parallel", "arbitrary"),
        name="fox_attention",
    )(qn, kn, vn, f_cum.reshape(batch * h, 1, seq), proj)


def _pad_cols(w, n):
    return jnp.zeros((w.shape[0], n), w.dtype).at[:, :w.shape[1]].set(w)


def kernel(x, c, mod_w, mod_b, norm_mix_gain, norm_ffn_gain, ab_w_in, ab_w_out, hgrn_lb_logits,
           hgrn_out_gain, fox_q_gain, fox_k_gain, fox_f_bias, gla_w_in, gla_w_gate_up, gla_b_gate,
           gla_out_gain, gla_w_out, ffn_w_in, ffn_w_out):
    batch, seq, d = x.shape
    depth = mod_w.shape[0]
    a_width = A_HEADS * A_DK
    ab_main = 4 * a_width + 4 * B_HEADS * B_HEAD_DIM
    c_main = 2 * C_HEADS * C_DK + 2 * C_HEADS * C_DV

    lbs = lower_bounds(hgrn_lb_logits)
    mod = modulation(c, mod_w, mod_b)
    xs = x.reshape(batch * seq, d)

    for layer in range(depth):
        sh1, sc1, g1, sh2, sc2, g2 = [
            mod[layer, :, i * d:(i + 1) * d].reshape(batch, 1, d) for i in range(6)]
        if layer % 2 == 0:
            i = layer // 2
            w_in = ab_w_in[i]
            proj, logits = norm_proj(xs, norm_mix_gain[layer], sh1, sc1,
                                     w_in[:, :ab_main].astype(BF16),
                                     _pad_cols(w_in[:, ab_main:], LANES).astype(BF16), seq)
            o_a = hgrn2_heads(proj, lbs[i], hgrn_out_gain[i], batch, seq)
            qn, kn, vn, f_cum = fox_prep(proj, logits, fox_f_bias[i], fox_q_gain[i], fox_k_gain[i],
                                         batch, seq)
            o_b = fox_attention(qn, kn, vn, f_cum, proj, batch, seq)
            w_out = ab_w_out[i].astype(BF16)
            xs = out_proj_residual([o_a, o_b], [w_out[:a_width], w_out[a_width:]], xs, g1, seq)
        else:
            j = layer // 2
            w_in = gla_w_in[j]
            proj, low = norm_proj(xs, norm_mix_gain[layer], sh1, sc1,
                                  w_in[:, :c_main].astype(BF16),
                                  _pad_cols(w_in[:, c_main:], LANES).astype(BF16), seq)
            w_up = jnp.zeros((LANES, gla_w_gate_up.shape[2]), BF16).at[:C_GATE_RANK].set(
                gla_w_gate_up[j].astype(BF16))
            o_c = gla_heads(proj, low, w_up, gla_b_gate[j], gla_out_gain[j], batch, seq)
            xs = out_proj_residual([o_c], [gla_w_out[j].astype(BF16)], xs, g1, seq)
        xs = ffn_residual(xs, norm_ffn_gain[layer], sh2, sc2, g2,
                          ffn_w_in[layer].astype(BF16), ffn_w_out[layer].astype(BF16), seq)
    return xs.reshape(batch, seq, d)
```

```python
import functools

import jax
import jax.numpy as jnp
from jax import lax
from jax.experimental import pallas as pl
from jax.experimental.pallas import tpu as pltpu

F32 = jnp.float32
BF16 = jnp.bfloat16

EPS = 1e-6
MIN_FORGET = 1e-30
MASK_VALUE = -1e30
GLA_GATE_NORMALIZER = 16.0

LANES = 128
VMEM_LIMIT = 56 * 1024 * 1024

A_HEADS = 8
A_DK = 128
B_HEADS = 8
B_HEAD_DIM = 128
C_HEADS = 4
C_DK = 256
C_DV = 512
C_GATE_RANK = 16

PROJ_TM = 512
PROJ_TN = 1024
FFN_TM = 512
FFN_TF = 512
REC_TB = 512
REC_CHUNK = 64
HGRN_SUB = 16
GLA_SUB = 64
ATT_TQ = 512
ATT_TK = 512
ATT_HEADS = 2
PREP_TB = 512


def _params(*sem):
    return pltpu.CompilerParams(dimension_semantics=sem, vmem_limit_bytes=VMEM_LIMIT)


def _sigmoid(x):
    return 1.0 / (1.0 + jnp.exp(-x))


def _silu(x):
    return x * _sigmoid(x)


def _log_sigmoid(x):
    return jnp.minimum(x, 0.0) - jnp.log1p(jnp.exp(-jnp.abs(x)))


def _split3(x):
    hi = x.astype(BF16)
    r = x - hi.astype(F32)
    mid = r.astype(BF16)
    lo = (r - mid.astype(F32)).astype(BF16)
    return hi, mid, lo


def _dot(a, b):
    return jnp.dot(a, b, preferred_element_type=F32)


def _dot_nt(a, b):
    return lax.dot_general(a, b, (((1,), (1,)), ((), ())), preferred_element_type=F32)


def _dot_tn(a, b):
    return lax.dot_general(a, b, (((0,), (0,)), ((), ())), preferred_element_type=F32)


def _lower_bound_kernel(logit_ref, o_ref):
    z = logit_ref[...]
    n = z.shape[0]
    e = jnp.exp(z - jnp.max(z, axis=0, keepdims=True))
    p = e / jnp.sum(e, axis=0, keepdims=True)
    run = jnp.zeros_like(p[0:1])
    rows = [run]
    for k in range(1, n):
        run = run + p[k:k + 1]
        rows.append(run)
    o_ref[...] = jnp.clip(jnp.concatenate(rows, axis=0), 0.0, 1.0 - 1e-6)


def lower_bounds(logits):
    return pl.pallas_call(
        _lower_bound_kernel,
        out_shape=jax.ShapeDtypeStruct(logits.shape, F32),
        name="hgrn_lower_bounds",
    )(logits.astype(F32))


def _mod_kernel(c_ref, w_ref, b_ref, o_ref):
    s = _silu(c_ref[...])
    o_ref[0] = jnp.dot(s, w_ref[0], preferred_element_type=F32,
                       precision=lax.Precision.HIGHEST) + b_ref[0]


def modulation(c, mod_w, mod_b, tn=1024):
    n_layers, d, n = mod_w.shape
    b = c.shape[0]
    rows = 8
    c_pad = jnp.zeros((rows, d), F32).at[:b].set(c)
    out = pl.pallas_call(
        _mod_kernel,
        grid=(n_layers, n // tn),
        in_specs=[
            pl.BlockSpec((rows, d), lambda l, j: (0, 0)),
            pl.BlockSpec((1, d, tn), lambda l, j: (l, 0, j)),
            pl.BlockSpec((1, 1, tn), lambda l, j: (l, 0, j)),
        ],
        out_specs=pl.BlockSpec((1, rows, tn), lambda l, j: (l, 0, j)),
        out_shape=jax.ShapeDtypeStruct((n_layers, rows, n), F32),
        compiler_params=_params("parallel", "parallel"),
        name="adaln_modulation",
    )(c_pad, mod_w, mod_b.reshape(n_layers, 1, n))
    return out[:, :b]


def _norm_modulate(x, gain, shift, scale):
    var = jnp.mean(x * x, axis=-1, keepdims=True)
    y = x * lax.rsqrt(var + EPS) * gain
    return y * (1.0 + scale) + shift


def _norm_proj_kernel(x_ref, gain_ref, sh_ref, sc_ref, w_ref, waux_ref, o_ref, oaux_ref, h_ref):
    @pl.when(pl.program_id(1) == 0)
    def _():
        h = _norm_modulate(x_ref[...], gain_ref[...], sh_ref[0], sc_ref[0]).astype(BF16)
        h_ref[...] = h
        oaux_ref[...] = _dot(h, waux_ref[...])

    o_ref[...] = _dot(h_ref[...], w_ref[...])


def norm_proj(x, gain, shift, scale, w, w_aux, seq, tm=PROJ_TM, tn=PROJ_TN):
    m, d = x.shape
    n = w.shape[1]
    tm = min(tm, seq)
    per_batch = seq // tm
    return pl.pallas_call(
        _norm_proj_kernel,
        grid=(m // tm, n // tn),
        in_specs=[
            pl.BlockSpec((tm, d), lambda i, j: (i, 0)),
            pl.BlockSpec((1, d), lambda i, j: (0, 0)),
            pl.BlockSpec((1, 1, d), lambda i, j: (i // per_batch, 0, 0)),
            pl.BlockSpec((1, 1, d), lambda i, j: (i // per_batch, 0, 0)),
            pl.BlockSpec((d, tn), lambda i, j: (0, j)),
            pl.BlockSpec((d, LANES), lambda i, j: (0, 0)),
        ],
        out_specs=[
            pl.BlockSpec((tm, tn), lambda i, j: (i, j)),
            pl.BlockSpec((tm, LANES), lambda i, j: (i, 0)),
        ],
        out_shape=[
            jax.ShapeDtypeStruct((m, n), F32),
            jax.ShapeDtypeStruct((m, LANES), F32),
        ],
        scratch_shapes=[pltpu.VMEM((tm, d), BF16)],
        compiler_params=_params("parallel", "arbitrary"),
        name="norm_in_proj",
    )(x, gain.reshape(1, d), shift, scale, w, w_aux)


def _out_proj_kernel(*refs, n_lhs):
    lhs = refs[:n_lhs]
    ws = refs[n_lhs:2 * n_lhs]
    x_ref, gate_ref, o_ref = refs[2 * n_lhs:]
    y = _dot(lhs[0][...], ws[0][...])
    for a_ref, w_ref in zip(lhs[1:], ws[1:]):
        y = y + _dot(a_ref[...], w_ref[...])
    o_ref[...] = x_ref[...] + gate_ref[0] * y


def out_proj_residual(lhs_list, w_list, x, gate, seq, tm=PROJ_TM, tn=PROJ_TN):
    m, d = x.shape
    tm = min(tm, seq)
    per_batch = seq // tm
    n_lhs = len(lhs_list)
    in_specs = (
        [pl.BlockSpec((tm, a.shape[1]), lambda i, j: (i, 0)) for a in lhs_list]
        + [pl.BlockSpec((w.shape[0], tn), lambda i, j: (0, j)) for w in w_list]
        + [pl.BlockSpec((tm, tn), lambda i, j: (i, j)),
           pl.BlockSpec((1, 1, tn), lambda i, j: (i // per_batch, 0, j))]
    )
    return pl.pallas_call(
        functools.partial(_out_proj_kernel, n_lhs=n_lhs),
        grid=(m // tm, d // tn),
        in_specs=in_specs,
        out_specs=pl.BlockSpec((tm, tn), lambda i, j: (i, j)),
        out_shape=jax.ShapeDtypeStruct((m, d), F32),
        compiler_params=_params("parallel", "parallel"),
        name="out_proj_residual",
    )(*lhs_list, *w_list, x, gate)


def _ffn_kernel(x_ref, gain_ref, sh_ref, sc_ref, gate_ref, wa_ref, wu_ref, wo_ref, o_ref,
                h_ref, acc_ref):
    f = pl.program_id(1)

    @pl.when(f == 0)
    def _():
        h_ref[...] = _norm_modulate(x_ref[...], gain_ref[...], sh_ref[0], sc_ref[0]).astype(BF16)

    h = h_ref[...]
    a = _dot(h, wa_ref[...])
    u = _dot(h, wu_ref[...])
    part = _dot((_silu(a) * u).astype(BF16), wo_ref[...])

    @pl.when(f == 0)
    def _():
        acc_ref[...] = part

    @pl.when(f > 0)
    def _():
        acc_ref[...] += part

    @pl.when(f == pl.num_programs(1) - 1)
    def _():
        o_ref[...] = x_ref[...] + gate_ref[0] * acc_ref[...]


def ffn_residual(x, gain, shift, scale, gate, w_in, w_out, seq, tm=FFN_TM, tf=FFN_TF):
    m, d = x.shape
    d_ff = w_out.shape[0]
    tm = min(tm, seq)
    per_batch = seq // tm
    nf = d_ff // tf
    vec = pl.BlockSpec((1, 1, d), lambda i, f: (i // per_batch, 0, 0))
    return pl.pallas_call(
        _ffn_kernel,
        grid=(m // tm, nf),
        in_specs=[
            pl.BlockSpec((tm, d), lambda i, f: (i, 0)),
            pl.BlockSpec((1, d), lambda i, f: (0, 0)),
            vec, vec, vec,
            pl.BlockSpec((d, tf), lambda i, f: (0, f)),
            pl.BlockSpec((d, tf), lambda i, f: (0, nf + f)),
            pl.BlockSpec((tf, d), lambda i, f: (f, 0)),
        ],
        out_specs=pl.BlockSpec((tm, d), lambda i, f: (i, 0)),
        out_shape=jax.ShapeDtypeStruct((m, d), F32),
        scratch_shapes=[pltpu.VMEM((tm, d), BF16), pltpu.VMEM((tm, d), F32)],
        compiler_params=_params("parallel", "arbitrary"),
        name="swiglu_ffn_residual",
    )(x, gain.reshape(1, d), shift, scale, gate, w_in, w_in, w_out)


def _cumsum_rows(g, tril):
    n = g.shape[1]
    y = _dot(tril, jnp.concatenate(_split3(g), axis=1))
    return y[:, :n] + y[:, n:2 * n] + y[:, 2 * n:]


def _glr_heads(qs, ks, vs, cums, s_ref, sub):
    heads = range(len(qs))
    c, dk = qs[0].shape
    dv = vs[0].shape[1]
    vbs = [v.astype(BF16) for v in vs]
    lasts = [cum[c - 1:c, :] for cum in cums]

    inter = [_dot((qs[h] * jnp.exp(cums[h])).astype(BF16), s_ref[h].astype(BF16)) for h in heads]
    updates = [_dot_tn((ks[h] * jnp.exp(lasts[h] - cums[h])).astype(BF16), vbs[h]) for h in heads]
    for h in heads:
        decay_cols = jnp.transpose(jnp.broadcast_to(jnp.exp(lasts[h]), (LANES, dk)))
        s_ref[h] = jnp.concatenate([decay_cols] * (dv // LANES), axis=1) * s_ref[h] + updates[h]

    spans = [(r0, r0 + sub) for r0 in range(0, c, sub)]
    scores = []
    for h in heads:
        q, k, cum = qs[h], ks[h], cums[h]
        per_head = []
        for r0, r1 in spans:
            hi = cum[r1 - 1:r1, :]
            mid = 0.5 * hi if r0 == 0 else 0.5 * (cum[r0 - 1:r0, :] + hi)
            qi = (q[r0:r1] * jnp.exp(cum[r0:r1] - mid)).astype(BF16)
            ki = (k[:r1] * jnp.exp(mid - cum[:r1])).astype(BF16)
            a = _dot_nt(qi, ki)
            row = r0 + lax.broadcasted_iota(jnp.int32, (sub, r1), 0)
            col = lax.broadcasted_iota(jnp.int32, (sub, r1), 1)
            per_head.append(jnp.where(col <= row, a, 0.0).astype(BF16))
        scores.append(per_head)
    outs = []
    for h in heads:
        blocks = [_dot(a, vbs[h][:r1]) for a, (_, r1) in zip(scores[h], spans)]
        intra = blocks[0] if len(blocks) == 1 else jnp.concatenate(blocks, axis=0)
        outs.append(inter[h] + intra)
    return outs


def _tril_bf16(c):
    row = lax.broadcasted_iota(jnp.int32, (c, c), 0)
    col = lax.broadcasted_iota(jnp.int32, (c, c), 1)
    return jnp.where(col <= row, 1.0, 0.0).astype(BF16)


def _hgrn_kernel(q_ref, f_ref, i_ref, g_ref, lb_ref, gain_ref, o_ref, s_ref, *, chunk, sub):
    @pl.when(pl.program_id(1) == 0)
    def _():
        s_ref[...] = jnp.zeros_like(s_ref)

    heads, dk, _ = s_ref.shape
    tril = _tril_bf16(chunk)

    def body(ci, carry):
        rows = pl.ds(pl.multiple_of(ci * chunk, chunk), chunk)
        lb = lb_ref[...]
        sig = _sigmoid(f_ref[rows, :])
        forget = lb + (1.0 - lb) * sig
        cum = _cumsum_rows(jnp.log(jnp.maximum(forget, MIN_FORGET)), tril)
        key = (1.0 - lb) * (1.0 - sig)
        q = _silu(q_ref[rows, :])
        v = i_ref[rows, :]
        cols = [slice(hh * dk, (hh + 1) * dk) for hh in range(heads)]
        outs = _glr_heads([q[:, cs] for cs in cols], [key[:, cs] for cs in cols],
                          [v[:, cs] for cs in cols], [cum[:, cs] for cs in cols], s_ref, sub)
        for o, cs in zip(outs, cols):
            o = o * lax.rsqrt(jnp.mean(o * o, axis=-1, keepdims=True) + EPS) * gain_ref[:, cs]
            o_ref[rows, cs] = (o * _silu(g_ref[rows, cs])).astype(o_ref.dtype)
        return carry

    lax.fori_loop(0, q_ref.shape[0] // chunk, body, 0)


def hgrn2_heads(proj, lb, out_gain, batch, seq, tb=REC_TB, chunk=REC_CHUNK, sub=HGRN_SUB):
    m = proj.shape[0]
    tb = min(tb, seq)
    nt = seq // tb
    h, dk = A_HEADS, A_DK
    w = h * dk

    def col(group):
        return pl.BlockSpec((tb, w), lambda b, t: (b * nt + t, group))

    vec = pl.BlockSpec((1, w), lambda b, t: (0, 0))
    return pl.pallas_call(
        functools.partial(_hgrn_kernel, chunk=chunk, sub=sub),
        grid=(batch, nt),
        in_specs=[col(0), col(1), col(2), col(3), vec, vec],
        out_specs=pl.BlockSpec((tb, w), lambda b, t: (b * nt + t, 0)),
        out_shape=jax.ShapeDtypeStruct((m, w), BF16),
        scratch_shapes=[pltpu.VMEM((h, dk, dk), F32)],
        compiler_params=_params("parallel", "arbitrary"),
        name="hgrn2_recurrence",
    )(proj, proj, proj, proj, lb.reshape(1, w), out_gain.reshape(1, w))


def _gla_kernel(q_ref, k_ref, v_ref, g_ref, low_ref, wup_ref, bias_ref, gain_ref, o_ref, s_ref,
                *, chunk, sub):
    @pl.when(pl.program_id(1) == 0)
    def _():
        s_ref[...] = jnp.zeros_like(s_ref)

    heads, dk, dv = s_ref.shape
    tril = _tril_bf16(chunk)
    q_scale = dk ** -0.5

    def body(ci, carry):
        rows = pl.ds(pl.multiple_of(ci * chunk, chunk), chunk)
        z = _dot(low_ref[rows, :].astype(BF16), wup_ref[...]) + bias_ref[...]
        cum = _cumsum_rows(_log_sigmoid(z) * (1.0 / GLA_GATE_NORMALIZER), tril)
        kcs = [slice(hh * dk, (hh + 1) * dk) for hh in range(heads)]
        vcs = [slice(hh * dv, (hh + 1) * dv) for hh in range(heads)]
        outs = _glr_heads([q_ref[rows, kc] * q_scale for kc in kcs], [k_ref[rows, kc] for kc in kcs],
                          [v_ref[rows, vc] for vc in vcs], [cum[:, kc] for kc in kcs], s_ref, sub)
        for o, vc in zip(outs, vcs):
            o = o * lax.rsqrt(jnp.mean(o * o, axis=-1, keepdims=True) + EPS) * gain_ref[...]
            o_ref[rows, vc] = (o * _silu(g_ref[rows, vc])).astype(o_ref.dtype)
        return carry

    lax.fori_loop(0, q_ref.shape[0] // chunk, body, 0)


def gla_heads(proj, low, w_up, b_gate, out_gain, batch, seq, tb=REC_TB, chunk=REC_CHUNK,
              sub=GLA_SUB):
    m = proj.shape[0]
    tb = min(tb, seq)
    nt = seq // tb
    h, dk, dv = C_HEADS, C_DK, C_DV
    kw, vw = h * dk, h * dv
    v0 = 2 * kw // vw
    return pl.pallas_call(
        functools.partial(_gla_kernel, chunk=chunk, sub=sub),
        grid=(batch, nt),
        in_specs=[
            pl.BlockSpec((tb, kw), lambda b, t: (b * nt + t, 0)),
            pl.BlockSpec((tb, kw), lambda b, t: (b * nt + t, 1)),
            pl.BlockSpec((tb, vw), lambda b, t: (b * nt + t, v0)),
            pl.BlockSpec((tb, vw), lambda b, t: (b * nt + t, v0 + 1)),
            pl.BlockSpec((tb, LANES), lambda b, t: (b * nt + t, 0)),
            pl.BlockSpec((LANES, kw), lambda b, t: (0, 0)),
            pl.BlockSpec((1, kw), lambda b, t: (0, 0)),
            pl.BlockSpec((1, dv), lambda b, t: (0, 0)),
        ],
        out_specs=pl.BlockSpec((tb, vw), lambda b, t: (b * nt + t, 0)),
        out_shape=jax.ShapeDtypeStruct((m, vw), BF16),
        scratch_shapes=[pltpu.VMEM((h, dk, dv), F32)],
        compiler_params=_params("parallel", "arbitrary"),
        name="gla_recurrence",
    )(proj, proj, proj, proj, low, w_up, b_gate.reshape(1, kw), out_gain.reshape(1, dv))


LOG2E = 1.4426950408889634
F_PIECES = 3


def _fox_prep_kernel(q_ref, k_ref, v_ref, logit_ref, bias_ref, qg_ref, kg_ref,
                     qo_ref, ko_ref, vo_ref, carry_ref, *, heads, head_dim):
    @pl.when(pl.program_id(1) == 0)
    def _():
        carry_ref[...] = jnp.zeros_like(carry_ref)

    tb = q_ref.shape[0]
    log_f = _log_sigmoid(logit_ref[...] + bias_ref[...])
    cum = _cumsum_rows(log_f, _tril_bf16(tb)) + carry_ref[0:1, :]
    carry_ref[...] = jnp.broadcast_to(cum[tb - 1:tb, :], carry_ref.shape)
    pieces = [p.astype(F32) for p in _split3(-LOG2E * cum)]
    lane = lax.broadcasted_iota(jnp.int32, (tb, head_dim), 1)
    one_cols = jnp.where(lane < F_PIECES, 1.0, 0.0).astype(qo_ref.dtype)

    qg = qg_ref[...] * (head_dim ** -0.5 * LOG2E)
    kg = kg_ref[...]
    for hh in range(heads):
        cols = slice(hh * head_dim, (hh + 1) * head_dim)
        lo = 2 * hh * head_dim
        main = slice(lo, lo + head_dim)
        extra = slice(lo + head_dim, lo + 2 * head_dim)
        q = q_ref[:, cols]
        qo_ref[:, main] = (q * lax.rsqrt(jnp.mean(q * q, axis=-1, keepdims=True) + EPS) * qg
                           ).astype(qo_ref.dtype)
        qo_ref[:, extra] = one_cols
        k = k_ref[:, cols]
        ko_ref[:, main] = (k * lax.rsqrt(jnp.mean(k * k, axis=-1, keepdims=True) + EPS) * kg
                           ).astype(ko_ref.dtype)
        bias_cols = jnp.zeros((tb, head_dim), F32)
        for j, piece in enumerate(pieces):
            bias_cols = jnp.where(lane == j, piece[:, hh:hh + 1], bias_cols)
        ko_ref[:, extra] = bias_cols.astype(ko_ref.dtype)
    vo_ref[...] = v_ref[...].astype(vo_ref.dtype)


def fox_prep(proj, logits, f_bias, q_gain, k_gain, batch, seq, tb=PREP_TB):
    m = proj.shape[0]
    h, hd = B_HEADS, B_HEAD_DIM
    w = h * hd
    tb = min(tb, seq)
    nt = seq // tb
    base = (4 * A_HEADS * A_DK) // w
    bias = jnp.zeros((1, LANES), F32).at[0, :h].set(f_bias)

    def col(group):
        return pl.BlockSpec((tb, w), lambda b, t: (b * nt + t, base + group))

    wide_tok = pl.BlockSpec((tb, 2 * w), lambda b, t: (b * nt + t, 0))
    return pl.pallas_call(
        functools.partial(_fox_prep_kernel, heads=h, head_dim=hd),
        grid=(batch, nt),
        in_specs=[
            col(0), col(1), col(2),
            pl.BlockSpec((tb, LANES), lambda b, t: (b * nt + t, 0)),
            pl.BlockSpec((1, LANES), lambda b, t: (0, 0)),
            pl.BlockSpec((1, hd), lambda b, t: (0, 0)),
            pl.BlockSpec((1, hd), lambda b, t: (0, 0)),
        ],
        out_specs=[wide_tok, wide_tok, pl.BlockSpec((tb, w), lambda b, t: (b * nt + t, 0))],
        out_shape=[jax.ShapeDtypeStruct((m, 2 * w), BF16)] * 2 + [jax.ShapeDtypeStruct((m, w), BF16)],
        scratch_shapes=[pltpu.VMEM((8, LANES), F32)],
        compiler_params=_params("parallel", "arbitrary"),
        name="fox_prep",
    )(proj, proj, proj, logits, bias, q_gain.reshape(1, hd), k_gain.reshape(1, hd))


def _fox_attn_kernel(q_ref, k_ref, v_ref, g_ref, o_ref, m_ref, acc_ref, *, tq, tk, heads, head_dim):
    qi = pl.program_id(2)
    q_start = pl.multiple_of(qi * tq, tq)
    wide = 2 * head_dim
    ones = jnp.ones((tk, head_dim), BF16)

    def tile(start, mask):
        rows = pl.ds(start, tk)
        scores = [_dot_nt(q_ref[:, hh * wide:(hh + 1) * wide], k_ref[rows, hh * wide:(hh + 1) * wide])
                  for hh in range(heads)]
        for hh, s in enumerate(scores):
            if mask is not None:
                s = jnp.where(mask, s, MASK_VALUE)
            m_prev = m_ref[hh]
            m_new = jnp.maximum(m_prev, jnp.max(s, axis=-1, keepdims=True))
            p = jnp.exp2(s - pltpu.repeat(m_new, tk // LANES, axis=1))
            alpha = jnp.exp2(m_prev - m_new)
            m_ref[hh] = m_new
            v_aug = jnp.concatenate([v_ref[rows, hh * head_dim:(hh + 1) * head_dim], ones], axis=1)
            acc_ref[hh] = (pltpu.repeat(alpha, wide // LANES, axis=1) * acc_ref[hh]
                           + _dot(p.astype(BF16), v_aug))

    m_ref[...] = jnp.full(m_ref.shape, MASK_VALUE, F32)
    acc_ref[...] = jnp.zeros_like(acc_ref)

    def body(ki, carry):
        tile(pl.multiple_of(ki * tk, tk), None)
        return carry

    lax.fori_loop(0, qi * (tq // tk), body, 0)
    for d in range(tq // tk):
        row = lax.broadcasted_iota(jnp.int32, (tq, tk), 0)
        col = lax.broadcasted_iota(jnp.int32, (tq, tk), 1) + d * tk
        tile(pl.multiple_of(q_start + d * tk, tk), col <= row)
    for hh in range(heads):
        acc = acc_ref[hh]
        cols = slice(hh * head_dim, (hh + 1) * head_dim)
        o_ref[:, cols] = (acc[:, :head_dim] / acc[:, head_dim:] * _sigmoid(g_ref[:, cols])
                          ).astype(o_ref.dtype)


def fox_attention(q_aug, k_aug, vn, proj, batch, seq, tq=ATT_TQ, tk=ATT_TK, group=ATT_HEADS):
    m = q_aug.shape[0]
    h, hd = B_HEADS, B_HEAD_DIM
    tq = min(tq, seq)
    tk = min(tk, tq)
    nq = seq // tq
    gate_block = (4 * A_HEADS * A_DK + 3 * h * hd) // (group * hd)
    return pl.pallas_call(
        functools.partial(_fox_attn_kernel, tq=tq, tk=tk, heads=group, head_dim=hd),
        grid=(batch, h // group, nq),
        in_specs=[
            pl.BlockSpec((tq, 2 * group * hd), lambda b, hg, i: (b * nq + i, hg)),
            pl.BlockSpec((seq, 2 * group * hd), lambda b, hg, i: (b, hg)),
            pl.BlockSpec((seq, group * hd), lambda b, hg, i: (b, hg)),
            pl.BlockSpec((tq, group * hd), lambda b, hg, i: (b * nq + i, gate_block + hg)),
        ],
        out_specs=pl.BlockSpec((tq, group * hd), lambda b, hg, i: (b * nq + i, hg)),
        out_shape=jax.ShapeDtypeStruct((m, h * hd), BF16),
        scratch_shapes=[pltpu.VMEM((group, tq, LANES), F32), pltpu.VMEM((group, tq, 2 * hd), F32)],
        compiler_params=_params("parallel", "parallel", "arbitrary"),
        name="fox_attention",
    )(q_aug, k_aug, vn, proj)


def _pad_cols(w, n):
    return jnp.zeros((w.shape[0], n), w.dtype).at[:, :w.shape[1]].set(w)


def kernel(x, c, mod_w, mod_b, norm_mix_gain, norm_ffn_gain, ab_w_in, ab_w_out, hgrn_lb_logits,
           hgrn_out_gain, fox_q_gain, fox_k_gain, fox_f_bias, gla_w_in, gla_w_gate_up, gla_b_gate,
           gla_out_gain, gla_w_out, ffn_w_in, ffn_w_out):
    batch, seq, d = x.shape
    depth = mod_w.shape[0]
    a_width = A_HEADS * A_DK
    ab_main = 4 * a_width + 4 * B_HEADS * B_HEAD_DIM
    c_main = 2 * C_HEADS * C_DK + 2 * C_HEADS * C_DV

    lbs = lower_bounds(hgrn_lb_logits)
    mod = modulation(c, mod_w, mod_b)
    xs = x.reshape(batch * seq, d)

    for layer in range(depth):
        sh1, sc1, g1, sh2, sc2, g2 = [
            mod[layer, :, i * d:(i + 1) * d].reshape(batch, 1, d) for i in range(6)]
        if layer % 2 == 0:
            i = layer // 2
            w_in = ab_w_in[i]
            proj, logits = norm_proj(xs, norm_mix_gain[layer], sh1, sc1,
                                     w_in[:, :ab_main].astype(BF16),
                                     _pad_cols(w_in[:, ab_main:], LANES).astype(BF16), seq)
            o_a = hgrn2_heads(proj, lbs[i], hgrn_out_gain[i], batch, seq)
            q_aug, k_aug, vn = fox_prep(proj, logits, fox_f_bias[i], fox_q_gain[i], fox_k_gain[i],
                                        batch, seq)
            o_b = fox_attention(q_aug, k_aug, vn, proj, batch, seq)
            w_out = ab_w_out[i].astype(BF16)
            xs = out_proj_residual([o_a, o_b], [w_out[:a_width], w_out[a_width:]], xs, g1, seq)
        else:
            j = layer // 2
            w_in = gla_w_in[j]
            proj, low = norm_proj(xs, norm_mix_gain[layer], sh1, sc1,
                                  w_in[:, :c_main].astype(BF16),
                                  _pad_cols(w_in[:, c_main:], LANES).astype(BF16), seq)
            w_up = jnp.zeros((LANES, gla_w_gate_up.shape[2]), BF16).at[:C_GATE_RANK].set(
                gla_w_gate_up[j].astype(BF16))
            o_c = gla_heads(proj, low, w_up, gla_b_gate[j], gla_out_gain[j], batch, seq)
            xs = out_proj_residual([o_c], [gla_w_out[j].astype(BF16)], xs, g1, seq)
        xs = ffn_residual(xs, norm_ffn_gain[layer], sh2, sc2, g2,
                          ffn_w_in[layer].astype(BF16), ffn_w_out[layer].astype(BF16), seq)
    return xs.reshape(batch, seq, d)
```

```python
import functools

import jax
import jax.numpy as jnp
from jax import lax
from jax.experimental import pallas as pl
from jax.experimental.pallas import tpu as pltpu

F32 = jnp.float32
BF16 = jnp.bfloat16

EPS = 1e-6
MIN_FORGET = 1e-30
MASK_VALUE = -1e30
GLA_GATE_NORMALIZER = 16.0

LANES = 128
VMEM_LIMIT = 60 * 1024 * 1024

A_HEADS = 8
A_DK = 128
B_HEADS = 8
B_HEAD_DIM = 128
C_HEADS = 4
C_DK = 256
C_DV = 512
C_GATE_RANK = 16

PROJ_TM = 1024
PROJ_TN = 512
OUT_TM = 1024
OUT_TN = 1024
FFN_TM = 1024
FFN_TF = 512
REC_TB = 512
REC_CHUNK = 64
HGRN_SUB = 16
GLA_SUB = 64
ATT_TQ = 512
ATT_TK = 512
ATT_HEADS = 4
PREP_TB = 512


def _params(*sem):
    return pltpu.CompilerParams(dimension_semantics=sem, vmem_limit_bytes=VMEM_LIMIT)


def _sigmoid(x):
    return 1.0 / (1.0 + jnp.exp(-x))


def _silu(x):
    return x * _sigmoid(x)


def _log_sigmoid(x):
    return jnp.minimum(x, 0.0) - jnp.log1p(jnp.exp(-jnp.abs(x)))


def _split3(x):
    hi = x.astype(BF16)
    r = x - hi.astype(F32)
    mid = r.astype(BF16)
    lo = (r - mid.astype(F32)).astype(BF16)
    return hi, mid, lo


def _dot(a, b):
    return jnp.dot(a, b, preferred_element_type=F32)


def _dot_nt(a, b):
    return lax.dot_general(a, b, (((1,), (1,)), ((), ())), preferred_element_type=F32)


def _dot_tn(a, b):
    return lax.dot_general(a, b, (((0,), (0,)), ((), ())), preferred_element_type=F32)


def _lower_bound_kernel(logit_ref, o_ref):
    z = logit_ref[...]
    n = z.shape[0]
    e = jnp.exp(z - jnp.max(z, axis=0, keepdims=True))
    p = e / jnp.sum(e, axis=0, keepdims=True)
    run = jnp.zeros_like(p[0:1])
    rows = [run]
    for k in range(1, n):
        run = run + p[k:k + 1]
        rows.append(run)
    o_ref[...] = jnp.clip(jnp.concatenate(rows, axis=0), 0.0, 1.0 - 1e-6)


def lower_bounds(logits):
    return pl.pallas_call(
        _lower_bound_kernel,
        out_shape=jax.ShapeDtypeStruct(logits.shape, F32),
        name="hgrn_lower_bounds",
    )(logits.astype(F32))


def _mod_kernel(c_ref, w_ref, b_ref, o_ref):
    s = _silu(c_ref[...])
    o_ref[0] = jnp.dot(s, w_ref[0], preferred_element_type=F32,
                       precision=lax.Precision.HIGHEST) + b_ref[0]


def modulation(c, mod_w, mod_b, tn=1024):
    n_layers, d, n = mod_w.shape
    b = c.shape[0]
    rows = 8
    c_pad = jnp.zeros((rows, d), F32).at[:b].set(c)
    out = pl.pallas_call(
        _mod_kernel,
        grid=(n_layers, n // tn),
        in_specs=[
            pl.BlockSpec((rows, d), lambda l, j: (0, 0)),
            pl.BlockSpec((1, d, tn), lambda l, j: (l, 0, j)),
            pl.BlockSpec((1, 1, tn), lambda l, j: (l, 0, j)),
        ],
        out_specs=pl.BlockSpec((1, rows, tn), lambda l, j: (l, 0, j)),
        out_shape=jax.ShapeDtypeStruct((n_layers, rows, n), F32),
        compiler_params=_params("parallel", "parallel"),
        name="adaln_modulation",
    )(c_pad, mod_w, mod_b.reshape(n_layers, 1, n))
    return out[:, :b]


def _norm_modulate(x, gain, shift, scale):
    var = jnp.mean(x * x, axis=-1, keepdims=True)
    y = x * lax.rsqrt(var + EPS) * gain
    return y * (1.0 + scale) + shift


def _norm_proj_kernel(x_ref, gain_ref, sh_ref, sc_ref, w_ref, waux_ref, o_ref, oaux_ref, h_ref):
    @pl.when(pl.program_id(1) == 0)
    def _():
        h = _norm_modulate(x_ref[...], gain_ref[...], sh_ref[0], sc_ref[0]).astype(BF16)
        h_ref[...] = h
        oaux_ref[...] = _dot(h, waux_ref[...])

    o_ref[...] = _dot(h_ref[...], w_ref[...])


def norm_proj(x, gain, shift, scale, w, w_aux, seq, tm=PROJ_TM, tn=PROJ_TN):
    m, d = x.shape
    n = w.shape[1]
    tm = min(tm, seq)
    per_batch = seq // tm
    return pl.pallas_call(
        _norm_proj_kernel,
        grid=(m // tm, n // tn),
        in_specs=[
            pl.BlockSpec((tm, d), lambda i, j: (i, 0)),
            pl.BlockSpec((1, d), lambda i, j: (0, 0)),
            pl.BlockSpec((1, 1, d), lambda i, j: (i // per_batch, 0, 0)),
            pl.BlockSpec((1, 1, d), lambda i, j: (i // per_batch, 0, 0)),
            pl.BlockSpec((d, tn), lambda i, j: (0, j)),
            pl.BlockSpec((d, LANES), lambda i, j: (0, 0)),
        ],
        out_specs=[
            pl.BlockSpec((tm, tn), lambda i, j: (i, j)),
            pl.BlockSpec((tm, LANES), lambda i, j: (i, 0)),
        ],
        out_shape=[
            jax.ShapeDtypeStruct((m, n), F32),
            jax.ShapeDtypeStruct((m, LANES), F32),
        ],
        scratch_shapes=[pltpu.VMEM((tm, d), BF16)],
        compiler_params=_params("parallel", "arbitrary"),
        name="norm_in_proj",
    )(x, gain.reshape(1, d), shift, scale, w, w_aux)


def _out_proj_kernel(*refs, n_lhs):
    lhs = refs[:n_lhs]
    ws = refs[n_lhs:2 * n_lhs]
    x_ref, gate_ref, o_ref = refs[2 * n_lhs:]
    y = _dot(lhs[0][...], ws[0][...])
    for a_ref, w_ref in zip(lhs[1:], ws[1:]):
        y = y + _dot(a_ref[...], w_ref[...])
    o_ref[...] = x_ref[...] + gate_ref[0] * y


def out_proj_residual(lhs_list, w_list, x, gate, seq, tm=OUT_TM, tn=OUT_TN):
    m, d = x.shape
    tm = min(tm, seq)
    per_batch = seq // tm
    n_lhs = len(lhs_list)
    in_specs = (
        [pl.BlockSpec((tm, a.shape[1]), lambda i, j: (i, 0)) for a in lhs_list]
        + [pl.BlockSpec((w.shape[0], tn), lambda i, j: (0, j)) for w in w_list]
        + [pl.BlockSpec((tm, tn), lambda i, j: (i, j)),
           pl.BlockSpec((1, 1, tn), lambda i, j: (i // per_batch, 0, j))]
    )
    return pl.pallas_call(
        functools.partial(_out_proj_kernel, n_lhs=n_lhs),
        grid=(m // tm, d // tn),
        in_specs=in_specs,
        out_specs=pl.BlockSpec((tm, tn), lambda i, j: (i, j)),
        out_shape=jax.ShapeDtypeStruct((m, d), F32),
        compiler_params=_params("parallel", "parallel"),
        name="out_proj_residual",
    )(*lhs_list, *w_list, x, gate)


def _ffn_kernel(x_ref, gain_ref, sh_ref, sc_ref, gate_ref, wa_ref, wu_ref, wo_ref, o_ref, h_ref):
    @pl.when(pl.program_id(1) == 0)
    def _():
        x = x_ref[...]
        h_ref[...] = _norm_modulate(x, gain_ref[...], sh_ref[0], sc_ref[0]).astype(BF16)
        o_ref[...] = x

    h = h_ref[...]
    a = _dot(h, wa_ref[...])
    u = _dot(h, wu_ref[...])
    o_ref[...] += gate_ref[0] * _dot((_silu(a) * u).astype(BF16), wo_ref[...])


def ffn_residual(x, gain, shift, scale, gate, w_in, w_out, seq, tm=FFN_TM, tf=FFN_TF):
    m, d = x.shape
    d_ff = w_out.shape[0]
    tm = min(tm, seq)
    per_batch = seq // tm
    nf = d_ff // tf
    vec = pl.BlockSpec((1, 1, d), lambda i, f: (i // per_batch, 0, 0))
    return pl.pallas_call(
        _ffn_kernel,
        grid=(m // tm, nf),
        in_specs=[
            pl.BlockSpec((tm, d), lambda i, f: (i, 0), pipeline_mode=pl.Buffered(1)),
            pl.BlockSpec((1, d), lambda i, f: (0, 0)),
            vec, vec, vec,
            pl.BlockSpec((d, tf), lambda i, f: (0, f)),
            pl.BlockSpec((d, tf), lambda i, f: (0, nf + f)),
            pl.BlockSpec((tf, d), lambda i, f: (f, 0)),
        ],
        out_specs=pl.BlockSpec((tm, d), lambda i, f: (i, 0)),
        out_shape=jax.ShapeDtypeStruct((m, d), F32),
        scratch_shapes=[pltpu.VMEM((tm, d), BF16)],
        compiler_params=_params("parallel", "arbitrary"),
        name="swiglu_ffn_residual",
    )(x, gain.reshape(1, d), shift, scale, gate, w_in, w_in, w_out)


def _cumsum_rows(g, tril):
    n = g.shape[1]
    y = _dot(tril, jnp.concatenate(_split3(g), axis=1))
    return y[:, :n] + y[:, n:2 * n] + y[:, 2 * n:]


def _glr_heads(qs, ks, vs, cums, s_ref, sub):
    heads = range(len(qs))
    c, dk = qs[0].shape
    dv = vs[0].shape[1]
    vbs = [v.astype(BF16) for v in vs]
    lasts = [cum[c - 1:c, :] for cum in cums]

    inter = [_dot((qs[h] * jnp.exp(cums[h])).astype(BF16), s_ref[h].astype(BF16)) for h in heads]
    updates = [_dot_tn((ks[h] * jnp.exp(lasts[h] - cums[h])).astype(BF16), vbs[h]) for h in heads]
    for h in heads:
        decay_cols = jnp.transpose(jnp.broadcast_to(jnp.exp(lasts[h]), (LANES, dk)))
        s_ref[h] = jnp.concatenate([decay_cols] * (dv // LANES), axis=1) * s_ref[h] + updates[h]

    spans = [(r0, r0 + sub) for r0 in range(0, c, sub)]
    scores = []
    for h in heads:
        q, k, cum = qs[h], ks[h], cums[h]
        per_head = []
        for r0, r1 in spans:
            hi = cum[r1 - 1:r1, :]
            mid = 0.5 * hi if r0 == 0 else 0.5 * (cum[r0 - 1:r0, :] + hi)
            qi = (q[r0:r1] * jnp.exp(cum[r0:r1] - mid)).astype(BF16)
            ki = (k[:r1] * jnp.exp(mid - cum[:r1])).astype(BF16)
            a = _dot_nt(qi, ki)
            row = r0 + lax.broadcasted_iota(jnp.int32, (sub, r1), 0)
            col = lax.broadcasted_iota(jnp.int32, (sub, r1), 1)
            per_head.append(jnp.where(col <= row, a, 0.0).astype(BF16))
        scores.append(per_head)
    outs = []
    for h in heads:
        blocks = [_dot(a, vbs[h][:r1]) for a, (_, r1) in zip(scores[h], spans)]
        intra = blocks[0] if len(blocks) == 1 else jnp.concatenate(blocks, axis=0)
        outs.append(inter[h] + intra)
    return outs


def _tril_bf16(c):
    row = lax.broadcasted_iota(jnp.int32, (c, c), 0)
    col = lax.broadcasted_iota(jnp.int32, (c, c), 1)
    return jnp.where(col <= row, 1.0, 0.0).astype(BF16)


def _hgrn_kernel(q_ref, f_ref, i_ref, g_ref, lb_ref, gain_ref, o_ref, s_ref, *, chunk, sub):
    @pl.when(pl.program_id(1) == 0)
    def _():
        s_ref[...] = jnp.zeros_like(s_ref)

    heads, dk, _ = s_ref.shape
    tril = _tril_bf16(chunk)

    def body(ci, carry):
        rows = pl.ds(pl.multiple_of(ci * chunk, chunk), chunk)
        lb = lb_ref[...]
        sig = _sigmoid(f_ref[rows, :])
        forget = lb + (1.0 - lb) * sig
        cum = _cumsum_rows(jnp.log(jnp.maximum(forget, MIN_FORGET)), tril)
        key = (1.0 - lb) * (1.0 - sig)
        q = _silu(q_ref[rows, :])
        v = i_ref[rows, :]
        cols = [slice(hh * dk, (hh + 1) * dk) for hh in range(heads)]
        outs = _glr_heads([q[:, cs] for cs in cols], [key[:, cs] for cs in cols],
                          [v[:, cs] for cs in cols], [cum[:, cs] for cs in cols], s_ref, sub)
        for o, cs in zip(outs, cols):
            o = o * lax.rsqrt(jnp.mean(o * o, axis=-1, keepdims=True) + EPS) * gain_ref[:, cs]
            o_ref[rows, cs] = (o * _silu(g_ref[rows, cs])).astype(o_ref.dtype)
        return carry

    lax.fori_loop(0, q_ref.shape[0] // chunk, body, 0)


def hgrn2_heads(proj, lb, out_gain, batch, seq, tb=REC_TB, chunk=REC_CHUNK, sub=HGRN_SUB):
    m = proj.shape[0]
    tb = min(tb, seq)
    nt = seq // tb
    h, dk = A_HEADS, A_DK
    w = h * dk

    def col(group):
        return pl.BlockSpec((tb, w), lambda b, t: (b * nt + t, group))

    vec = pl.BlockSpec((1, w), lambda b, t: (0, 0))
    return pl.pallas_call(
        functools.partial(_hgrn_kernel, chunk=chunk, sub=sub),
        grid=(batch, nt),
        in_specs=[col(0), col(1), col(2), col(3), vec, vec],
        out_specs=pl.BlockSpec((tb, w), lambda b, t: (b * nt + t, 0)),
        out_shape=jax.ShapeDtypeStruct((m, w), BF16),
        scratch_shapes=[pltpu.VMEM((h, dk, dk), F32)],
        compiler_params=_params("parallel", "arbitrary"),
        name="hgrn2_recurrence",
    )(proj, proj, proj, proj, lb.reshape(1, w), out_gain.reshape(1, w))


def _gla_kernel(q_ref, k_ref, v_ref, g_ref, low_ref, wup_ref, bias_ref, gain_ref, o_ref, s_ref,
                *, chunk, sub):
    @pl.when(pl.program_id(1) == 0)
    def _():
        s_ref[...] = jnp.zeros_like(s_ref)

    heads, dk, dv = s_ref.shape
    tril = _tril_bf16(chunk)
    q_scale = dk ** -0.5

    def body(ci, carry):
        rows = pl.ds(pl.multiple_of(ci * chunk, chunk), chunk)
        z = _dot(low_ref[rows, :].astype(BF16), wup_ref[...]) + bias_ref[...]
        cum = _cumsum_rows(_log_sigmoid(z) * (1.0 / GLA_GATE_NORMALIZER), tril)
        kcs = [slice(hh * dk, (hh + 1) * dk) for hh in range(heads)]
        vcs = [slice(hh * dv, (hh + 1) * dv) for hh in range(heads)]
        outs = _glr_heads([q_ref[rows, kc] * q_scale for kc in kcs], [k_ref[rows, kc] for kc in kcs],
                          [v_ref[rows, vc] for vc in vcs], [cum[:, kc] for kc in kcs], s_ref, sub)
        for o, vc in zip(outs, vcs):
            o = o * lax.rsqrt(jnp.mean(o * o, axis=-1, keepdims=True) + EPS) * gain_ref[...]
            o_ref[rows, vc] = (o * _silu(g_ref[rows, vc])).astype(o_ref.dtype)
        return carry

    lax.fori_loop(0, q_ref.shape[0] // chunk, body, 0)


def gla_heads(proj, low, w_up, b_gate, out_gain, batch, seq, tb=REC_TB, chunk=REC_CHUNK,
              sub=GLA_SUB):
    m = proj.shape[0]
    tb = min(tb, seq)
    nt = seq // tb
    h, dk, dv = C_HEADS, C_DK, C_DV
    kw, vw = h * dk, h * dv
    v0 = 2 * kw // vw
    return pl.pallas_call(
        functools.partial(_gla_kernel, chunk=chunk, sub=sub),
        grid=(batch, nt),
        in_specs=[
            pl.BlockSpec((tb, kw), lambda b, t: (b * nt + t, 0)),
            pl.BlockSpec((tb, kw), lambda b, t: (b * nt + t, 1)),
            pl.BlockSpec((tb, vw), lambda b, t: (b * nt + t, v0)),
            pl.BlockSpec((tb, vw), lambda b, t: (b * nt + t, v0 + 1)),
            pl.BlockSpec((tb, LANES), lambda b, t: (b * nt + t, 0)),
            pl.BlockSpec((LANES, kw), lambda b, t: (0, 0)),
            pl.BlockSpec((1, kw), lambda b, t: (0, 0)),
            pl.BlockSpec((1, dv), lambda b, t: (0, 0)),
        ],
        out_specs=pl.BlockSpec((tb, vw), lambda b, t: (b * nt + t, 0)),
        out_shape=jax.ShapeDtypeStruct((m, vw), BF16),
        scratch_shapes=[pltpu.VMEM((h, dk, dv), F32)],
        compiler_params=_params("parallel", "arbitrary"),
        name="gla_recurrence",
    )(proj, proj, proj, proj, low, w_up, b_gate.reshape(1, kw), out_gain.reshape(1, dv))


LOG2E = 1.4426950408889634
F_PIECES = 3


def _fox_prep_kernel(q_ref, k_ref, v_ref, logit_ref, bias_ref, qg_ref, kg_ref,
                     qo_ref, ko_ref, vo_ref, carry_ref, *, heads, head_dim):
    @pl.when(pl.program_id(1) == 0)
    def _():
        carry_ref[...] = jnp.zeros_like(carry_ref)

    tb = q_ref.shape[0]
    log_f = _log_sigmoid(logit_ref[...] + bias_ref[...])
    cum = _cumsum_rows(log_f, _tril_bf16(tb)) + carry_ref[0:1, :]
    carry_ref[...] = jnp.broadcast_to(cum[tb - 1:tb, :], carry_ref.shape)
    pieces = [p.astype(F32) for p in _split3(-LOG2E * cum)]
    lane = lax.broadcasted_iota(jnp.int32, (tb, head_dim), 1)
    one_cols = jnp.where(lane < F_PIECES, 1.0, 0.0).astype(qo_ref.dtype)

    qg = qg_ref[...] * (head_dim ** -0.5 * LOG2E)
    kg = kg_ref[...]
    for hh in range(heads):
        cols = slice(hh * head_dim, (hh + 1) * head_dim)
        lo = 2 * hh * head_dim
        main = slice(lo, lo + head_dim)
        extra = slice(lo + head_dim, lo + 2 * head_dim)
        q = q_ref[:, cols]
        qo_ref[:, main] = (q * lax.rsqrt(jnp.mean(q * q, axis=-1, keepdims=True) + EPS) * qg
                           ).astype(qo_ref.dtype)
        qo_ref[:, extra] = one_cols
        k = k_ref[:, cols]
        ko_ref[:, main] = (k * lax.rsqrt(jnp.mean(k * k, axis=-1, keepdims=True) + EPS) * kg
                           ).astype(ko_ref.dtype)
        bias_cols = jnp.zeros((tb, head_dim), F32)
        for j, piece in enumerate(pieces):
            bias_cols = jnp.where(lane == j, piece[:, hh:hh + 1], bias_cols)
        ko_ref[:, extra] = bias_cols.astype(ko_ref.dtype)
    vo_ref[...] = v_ref[...].astype(vo_ref.dtype)


def fox_prep(proj, logits, f_bias, q_gain, k_gain, batch, seq, tb=PREP_TB):
    m = proj.shape[0]
    h, hd = B_HEADS, B_HEAD_DIM
    w = h * hd
    tb = min(tb, seq)
    nt = seq // tb
    base = (4 * A_HEADS * A_DK) // w
    bias = jnp.zeros((1, LANES), F32).at[0, :h].set(f_bias)

    def col(group):
        return pl.BlockSpec((tb, w), lambda b, t: (b * nt + t, base + group))

    wide_tok = pl.BlockSpec((tb, 2 * w), lambda b, t: (b * nt + t, 0))
    return pl.pallas_call(
        functools.partial(_fox_prep_kernel, heads=h, head_dim=hd),
        grid=(batch, nt),
        in_specs=[
            col(0), col(1), col(2),
            pl.BlockSpec((tb, LANES), lambda b, t: (b * nt + t, 0)),
            pl.BlockSpec((1, LANES), lambda b, t: (0, 0)),
            pl.BlockSpec((1, hd), lambda b, t: (0, 0)),
            pl.BlockSpec((1, hd), lambda b, t: (0, 0)),
        ],
        out_specs=[wide_tok, wide_tok, pl.BlockSpec((tb, w), lambda b, t: (b * nt + t, 0))],
        out_shape=[jax.ShapeDtypeStruct((m, 2 * w), BF16)] * 2 + [jax.ShapeDtypeStruct((m, w), BF16)],
        scratch_shapes=[pltpu.VMEM((8, LANES), F32)],
        compiler_params=_params("parallel", "arbitrary"),
        name="fox_prep",
    )(proj, proj, proj, logits, bias, q_gain.reshape(1, hd), k_gain.reshape(1, hd))


def _fox_attn_kernel(q_ref, k_ref, v_ref, g_ref, o_ref, m_ref, acc_ref, *, tq, tk, heads, head_dim):
    qi = pl.program_id(2)
    q_start = pl.multiple_of(qi * tq, tq)
    wide = 2 * head_dim
    ones = jnp.ones((tk, head_dim), BF16)

    def tile(start, mask):
        rows = pl.ds(start, tk)
        scores = [_dot_nt(q_ref[:, hh * wide:(hh + 1) * wide], k_ref[rows, hh * wide:(hh + 1) * wide])
                  for hh in range(heads)]
        for hh, s in enumerate(scores):
            if mask is not None:
                s = jnp.where(mask, s, MASK_VALUE)
            m_prev = m_ref[hh]
            m_new = jnp.maximum(m_prev, jnp.max(s, axis=-1, keepdims=True))
            p = jnp.exp2(s - jnp.concatenate([m_new] * (tk // LANES), axis=1))
            alpha = jnp.exp2(m_prev - m_new)
            m_ref[hh] = m_new
            v_aug = jnp.concatenate([v_ref[rows, hh * head_dim:(hh + 1) * head_dim], ones], axis=1)
            acc_ref[hh] = (jnp.concatenate([alpha] * (wide // LANES), axis=1) * acc_ref[hh]
                           + _dot(p.astype(BF16), v_aug))

    m_ref[...] = jnp.full(m_ref.shape, MASK_VALUE, F32)
    acc_ref[...] = jnp.zeros_like(acc_ref)

    def body(ki, carry):
        tile(pl.multiple_of(ki * tk, tk), None)
        return carry

    lax.fori_loop(0, qi * (tq // tk), body, 0)
    for d in range(tq // tk):
        row = lax.broadcasted_iota(jnp.int32, (tq, tk), 0)
        col = lax.broadcasted_iota(jnp.int32, (tq, tk), 1) + d * tk
        tile(pl.multiple_of(q_start + d * tk, tk), col <= row)
    for hh in range(heads):
        acc = acc_ref[hh]
        cols = slice(hh * head_dim, (hh + 1) * head_dim)
        o_ref[:, cols] = (acc[:, :head_dim] / acc[:, head_dim:] * _sigmoid(g_ref[:, cols])
                          ).astype(o_ref.dtype)


def fox_attention(q_aug, k_aug, vn, proj, batch, seq, tq=ATT_TQ, tk=ATT_TK, group=ATT_HEADS):
    m = q_aug.shape[0]
    h, hd = B_HEADS, B_HEAD_DIM
    tq = min(tq, seq)
    tk = min(tk, tq)
    nq = seq // tq
    gate_block = (4 * A_HEADS * A_DK + 3 * h * hd) // (group * hd)
    return pl.pallas_call(
        functools.partial(_fox_attn_kernel, tq=tq, tk=tk, heads=group, head_dim=hd),
        grid=(batch, h // group, nq),
        in_specs=[
            pl.BlockSpec((tq, 2 * group * hd), lambda b, hg, i: (b * nq + i, hg)),
            pl.BlockSpec((seq, 2 * group * hd), lambda b, hg, i: (b, hg),
                         pipeline_mode=pl.Buffered(1)),
            pl.BlockSpec((seq, group * hd), lambda b, hg, i: (b, hg),
                         pipeline_mode=pl.Buffered(1)),
            pl.BlockSpec((tq, group * hd), lambda b, hg, i: (b * nq + i, gate_block + hg)),
        ],
        out_specs=pl.BlockSpec((tq, group * hd), lambda b, hg, i: (b * nq + i, hg)),
        out_shape=jax.ShapeDtypeStruct((m, h * hd), BF16),
        scratch_shapes=[pltpu.VMEM((group, tq, LANES), F32), pltpu.VMEM((group, tq, 2 * hd), F32)],
        compiler_params=_params("parallel", "parallel", "arbitrary"),
        name="fox_attention",
    )(q_aug, k_aug, vn, proj)


def _pad_cols(w, n):
    return jnp.zeros((w.shape[0], n), w.dtype).at[:, :w.shape[1]].set(w)


def kernel(x, c, mod_w, mod_b, norm_mix_gain, norm_ffn_gain, ab_w_in, ab_w_out, hgrn_lb_logits,
           hgrn_out_gain, fox_q_gain, fox_k_gain, fox_f_bias, gla_w_in, gla_w_gate_up, gla_b_gate,
           gla_out_gain, gla_w_out, ffn_w_in, ffn_w_out):
    batch, seq, d = x.shape
    depth = mod_w.shape[0]
    a_width = A_HEADS * A_DK
    ab_main = 4 * a_width + 4 * B_HEADS * B_HEAD_DIM
    c_main = 2 * C_HEADS * C_DK + 2 * C_HEADS * C_DV

    lbs = lower_bounds(hgrn_lb_logits)
    mod = modulation(c, mod_w, mod_b)
    xs = x.reshape(batch * seq, d)

    for layer in range(depth):
        sh1, sc1, g1, sh2, sc2, g2 = [
            mod[layer, :, i * d:(i + 1) * d].reshape(batch, 1, d) for i in range(6)]
        if layer % 2 == 0:
            i = layer // 2
            w_in = ab_w_in[i]
            proj, logits = norm_proj(xs, norm_mix_gain[layer], sh1, sc1,
                                     w_in[:, :ab_main].astype(BF16),
                                     _pad_cols(w_in[:, ab_main:], LANES).astype(BF16), seq)
            o_a = hgrn2_heads(proj, lbs[i], hgrn_out_gain[i], batch, seq)
            q_aug, k_aug, vn = fox_prep(proj, logits, fox_f_bias[i], fox_q_gain[i], fox_k_gain[i],
                                        batch, seq)
            o_b = fox_attention(q_aug, k_aug, vn, proj, batch, seq)
            w_out = ab_w_out[i].astype(BF16)
            xs = out_proj_residual([o_a, o_b], [w_out[:a_width], w_out[a_width:]], xs, g1, seq)
        else:
            j = layer // 2
            w_in = gla_w_in[j]
            proj, low = norm_proj(xs, norm_mix_gain[layer], sh1, sc1,
                                  w_in[:, :c_main].astype(BF16),
                                  _pad_cols(w_in[:, c_main:], LANES).astype(BF16), seq)
            w_up = jnp.zeros((LANES, gla_w_gate_up.shape[2]), BF16).at[:C_GATE_RANK].set(
                gla_w_gate_up[j].astype(BF16))
            o_c = gla_heads(proj, low, w_up, gla_b_gate[j], gla_out_gain[j], batch, seq)
            xs = out_proj_residual([o_c], [gla_w_out[j].astype(BF16)], xs, g1, seq)
        xs = ffn_residual(xs, norm_ffn_gain[layer], sh2, sc2, g2,
                          ffn_w_in[layer].astype(BF16), ffn_w_out[layer].astype(BF16), seq)
    return xs.reshape(batch, seq, d)
```

```python
import functools

import jax
import jax.numpy as jnp
from jax import lax
from jax.experimental import pallas as pl
from jax.experimental.pallas import tpu as pltpu

F32 = jnp.float32
BF16 = jnp.bfloat16

EPS = 1e-6
MIN_FORGET = 1e-30
MASK_VALUE = -1e30
GLA_GATE_NORMALIZER = 16.0

LANES = 128
VMEM_LIMIT = 60 * 1024 * 1024

A_HEADS = 8
A_DK = 128
B_HEADS = 8
B_HEAD_DIM = 128
C_HEADS = 4
C_DK = 256
C_DV = 512
C_GATE_RANK = 16

PROJ_TM = 1024
PROJ_TN = 1024
OUT_TM = 1024
OUT_TN = 1024
FFN_TM = 1024
FFN_TF = 512
REC_TB = 512
REC_CHUNK = 64
HGRN_SUB = 16
GLA_CHUNK = 256
GLA_SUB = 64
ATT_TQ = 512
ATT_TK = 512
ATT_HEADS = 4
PREP_TB = 512


def _params(*sem):
    return pltpu.CompilerParams(dimension_semantics=sem, vmem_limit_bytes=VMEM_LIMIT)


def _sigmoid(x):
    return 1.0 / (1.0 + jnp.exp(-x))


def _silu(x):
    return x * _sigmoid(x)


def _log_sigmoid(x):
    return jnp.minimum(x, 0.0) - jnp.log1p(jnp.exp(-jnp.abs(x)))


def _split3(x):
    hi = x.astype(BF16)
    r = x - hi.astype(F32)
    mid = r.astype(BF16)
    lo = (r - mid.astype(F32)).astype(BF16)
    return hi, mid, lo


def _dot(a, b):
    return jnp.dot(a, b, preferred_element_type=F32)


def _dot_nt(a, b):
    return lax.dot_general(a, b, (((1,), (1,)), ((), ())), preferred_element_type=F32)


def _dot_tn(a, b):
    return lax.dot_general(a, b, (((0,), (0,)), ((), ())), preferred_element_type=F32)


def _lower_bound_kernel(logit_ref, o_ref):
    z = logit_ref[...]
    n = z.shape[0]
    e = jnp.exp(z - jnp.max(z, axis=0, keepdims=True))
    p = e / jnp.sum(e, axis=0, keepdims=True)
    run = jnp.zeros_like(p[0:1])
    rows = [run]
    for k in range(1, n):
        run = run + p[k:k + 1]
        rows.append(run)
    o_ref[...] = jnp.clip(jnp.concatenate(rows, axis=0), 0.0, 1.0 - 1e-6)


def lower_bounds(logits):
    return pl.pallas_call(
        _lower_bound_kernel,
        out_shape=jax.ShapeDtypeStruct(logits.shape, F32),
        name="hgrn_lower_bounds",
    )(logits.astype(F32))


def _mod_kernel(c_ref, w_ref, b_ref, o_ref):
    s = _silu(c_ref[...]).astype(BF16)
    o_ref[0] = _dot(s, w_ref[0].astype(BF16)) + b_ref[0]


def modulation(c, mod_w, mod_b, tn=2048):
    n_layers, d, n = mod_w.shape
    b = c.shape[0]
    rows = 8
    c_pad = jnp.zeros((rows, d), F32).at[:b].set(c)
    out = pl.pallas_call(
        _mod_kernel,
        grid=(n_layers, n // tn),
        in_specs=[
            pl.BlockSpec((rows, d), lambda l, j: (0, 0)),
            pl.BlockSpec((1, d, tn), lambda l, j: (l, 0, j)),
            pl.BlockSpec((1, 1, tn), lambda l, j: (l, 0, j)),
        ],
        out_specs=pl.BlockSpec((1, rows, tn), lambda l, j: (l, 0, j)),
        out_shape=jax.ShapeDtypeStruct((n_layers, rows, n), F32),
        compiler_params=_params("parallel", "parallel"),
        name="adaln_modulation",
    )(c_pad, mod_w, mod_b.reshape(n_layers, 1, n))
    return out[:, :b]


def _norm_modulate(x, gain, shift, scale):
    var = jnp.mean(x * x, axis=-1, keepdims=True)
    y = x * lax.rsqrt(var + EPS) * gain
    return y * (1.0 + scale) + shift


def _norm_proj_kernel(x_ref, gain_ref, sh_ref, sc_ref, w_ref, waux_ref, o_ref, oaux_ref, h_ref):
    @pl.when(pl.program_id(1) == 0)
    def _():
        h = _norm_modulate(x_ref[...], gain_ref[...], sh_ref[0], sc_ref[0]).astype(BF16)
        h_ref[...] = h
        oaux_ref[...] = _dot(h, waux_ref[...])

    o_ref[...] = _dot(h_ref[...], w_ref[0].astype(BF16)).astype(o_ref.dtype)


def norm_proj(x, gain, shift, scale, w_all, layer, n, w_aux, seq, tm=PROJ_TM, tn=PROJ_TN):
    m, d = x.shape
    tm = min(tm, seq)
    per_batch = seq // tm
    return pl.pallas_call(
        _norm_proj_kernel,
        grid=(m // tm, n // tn),
        in_specs=[
            pl.BlockSpec((tm, d), lambda i, j: (i, 0)),
            pl.BlockSpec((1, d), lambda i, j: (0, 0)),
            pl.BlockSpec((1, 1, d), lambda i, j: (i // per_batch, 0, 0)),
            pl.BlockSpec((1, 1, d), lambda i, j: (i // per_batch, 0, 0)),
            pl.BlockSpec((1, d, tn), lambda i, j: (layer, 0, j)),
            pl.BlockSpec((d, LANES), lambda i, j: (0, 0)),
        ],
        out_specs=[
            pl.BlockSpec((tm, tn), lambda i, j: (i, j)),
            pl.BlockSpec((tm, LANES), lambda i, j: (i, 0)),
        ],
        out_shape=[
            jax.ShapeDtypeStruct((m, n), BF16),
            jax.ShapeDtypeStruct((m, LANES), F32),
        ],
        scratch_shapes=[pltpu.VMEM((tm, d), BF16)],
        compiler_params=_params("parallel", "arbitrary"),
        name="norm_in_proj",
    )(x, gain.reshape(1, d), shift, scale, w_all, w_aux)


def _out_proj_kernel(*refs, n_lhs):
    lhs = refs[:n_lhs]
    ws = refs[n_lhs:2 * n_lhs]
    x_ref, gate_ref, o_ref = refs[2 * n_lhs:]
    y = _dot(lhs[0][...], ws[0][0].astype(BF16))
    for a_ref, w_ref in zip(lhs[1:], ws[1:]):
        y = y + _dot(a_ref[...], w_ref[0].astype(BF16))
    o_ref[...] = x_ref[...] + gate_ref[0] * y


def out_proj_residual(lhs_list, w_all, layer, x, gate, seq, tm=OUT_TM, tn=OUT_TN):
    m, d = x.shape
    tm = min(tm, seq)
    per_batch = seq // tm
    n_lhs = len(lhs_list)
    width = lhs_list[0].shape[1]
    assert all(a.shape[1] == width for a in lhs_list) and n_lhs * width == w_all.shape[1]

    def w_spec(part):
        return pl.BlockSpec((1, width, tn), lambda i, j: (layer, part, j))

    in_specs = (
        [pl.BlockSpec((tm, width), lambda i, j: (i, 0)) for _ in lhs_list]
        + [w_spec(part) for part in range(n_lhs)]
        + [pl.BlockSpec((tm, tn), lambda i, j: (i, j)),
           pl.BlockSpec((1, 1, tn), lambda i, j: (i // per_batch, 0, j))]
    )
    return pl.pallas_call(
        functools.partial(_out_proj_kernel, n_lhs=n_lhs),
        grid=(m // tm, d // tn),
        in_specs=in_specs,
        out_specs=pl.BlockSpec((tm, tn), lambda i, j: (i, j)),
        out_shape=jax.ShapeDtypeStruct((m, d), F32),
        compiler_params=_params("parallel", "parallel"),
        name="out_proj_residual",
    )(*lhs_list, *([w_all] * n_lhs), x, gate)


def _ffn_kernel(x_ref, gain_ref, sh_ref, sc_ref, gate_ref, wa_ref, wu_ref, wo_ref, o_ref, h_ref):
    @pl.when(pl.program_id(1) == 0)
    def _():
        x = x_ref[...]
        h_ref[...] = _norm_modulate(x, gain_ref[...], sh_ref[0], sc_ref[0]).astype(BF16)
        o_ref[...] = x

    h = h_ref[...]
    a = _dot(h, wa_ref[...])
    u = _dot(h, wu_ref[...])
    o_ref[...] += gate_ref[0] * _dot((_silu(a) * u).astype(BF16), wo_ref[...])


def ffn_residual(x, gain, shift, scale, gate, w_in, w_out, seq, tm=FFN_TM, tf=FFN_TF):
    m, d = x.shape
    d_ff = w_out.shape[0]
    tm = min(tm, seq)
    per_batch = seq // tm
    nf = d_ff // tf
    vec = pl.BlockSpec((1, 1, d), lambda i, f: (i // per_batch, 0, 0))
    return pl.pallas_call(
        _ffn_kernel,
        grid=(m // tm, nf),
        in_specs=[
            pl.BlockSpec((tm, d), lambda i, f: (i, 0), pipeline_mode=pl.Buffered(1)),
            pl.BlockSpec((1, d), lambda i, f: (0, 0)),
            vec, vec, vec,
            pl.BlockSpec((d, tf), lambda i, f: (0, f)),
            pl.BlockSpec((d, tf), lambda i, f: (0, nf + f)),
            pl.BlockSpec((tf, d), lambda i, f: (f, 0)),
        ],
        out_specs=pl.BlockSpec((tm, d), lambda i, f: (i, 0)),
        out_shape=jax.ShapeDtypeStruct((m, d), F32),
        scratch_shapes=[pltpu.VMEM((tm, d), BF16)],
        compiler_params=_params("parallel", "arbitrary"),
        name="swiglu_ffn_residual",
    )(x, gain.reshape(1, d), shift, scale, gate, w_in, w_in, w_out)


def _cumsum_rows(g, tril):
    n = g.shape[1]
    y = _dot(tril, jnp.concatenate(_split3(g), axis=1))
    return y[:, :n] + y[:, n:2 * n] + y[:, 2 * n:]


def _glr_heads(qs, ks, vs, cums, s_ref, sub):
    heads = range(len(qs))
    c, dk = qs[0].shape
    dv = vs[0].shape[1]
    vbs = [v.astype(BF16) for v in vs]
    lasts = [cum[c - 1:c, :] for cum in cums]

    inter = [_dot((qs[h] * jnp.exp(cums[h])).astype(BF16), s_ref[h].astype(BF16)) for h in heads]
    updates = [_dot_tn((ks[h] * jnp.exp(lasts[h] - cums[h])).astype(BF16), vbs[h]) for h in heads]
    for h in heads:
        decay_cols = jnp.transpose(jnp.broadcast_to(jnp.exp(lasts[h]), (LANES, dk)))
        s_ref[h] = jnp.concatenate([decay_cols] * (dv // LANES), axis=1) * s_ref[h] + updates[h]

    spans = [(r0, r0 + sub) for r0 in range(0, c, sub)]
    scores = []
    for h in heads:
        q, k, cum = qs[h], ks[h], cums[h]
        per_head = []
        for r0, r1 in spans:
            hi = cum[r1 - 1:r1, :]
            mid = 0.5 * hi if r0 == 0 else 0.5 * (cum[r0 - 1:r0, :] + hi)
            qi = (q[r0:r1] * jnp.exp(cum[r0:r1] - mid)).astype(BF16)
            ki = (k[:r1] * jnp.exp(mid - cum[:r1])).astype(BF16)
            a = _dot_nt(qi, ki)
            row = r0 + lax.broadcasted_iota(jnp.int32, (sub, r1), 0)
            col = lax.broadcasted_iota(jnp.int32, (sub, r1), 1)
            per_head.append(jnp.where(col <= row, a, 0.0).astype(BF16))
        scores.append(per_head)
    outs = []
    for h in heads:
        blocks = [_dot(a, vbs[h][:r1]) for a, (_, r1) in zip(scores[h], spans)]
        intra = blocks[0] if len(blocks) == 1 else jnp.concatenate(blocks, axis=0)
        outs.append(inter[h] + intra)
    return outs


def _tril_bf16(c):
    row = lax.broadcasted_iota(jnp.int32, (c, c), 0)
    col = lax.broadcasted_iota(jnp.int32, (c, c), 1)
    return jnp.where(col <= row, 1.0, 0.0).astype(BF16)


def _hgrn_kernel(q_ref, f_ref, i_ref, g_ref, lb_ref, gain_ref, o_ref, s_ref, *, chunk, sub):
    @pl.when(pl.program_id(1) == 0)
    def _():
        s_ref[...] = jnp.zeros_like(s_ref)

    heads, dk, _ = s_ref.shape
    tril = _tril_bf16(chunk)

    def body(ci, carry):
        rows = pl.ds(pl.multiple_of(ci * chunk, chunk), chunk)
        lb = lb_ref[...]
        sig = _sigmoid(f_ref[rows, :].astype(F32))
        forget = lb + (1.0 - lb) * sig
        cum = _cumsum_rows(jnp.log(jnp.maximum(forget, MIN_FORGET)), tril)
        key = (1.0 - lb) * (1.0 - sig)
        q = _silu(q_ref[rows, :].astype(F32))
        v = i_ref[rows, :]
        cols = [slice(hh * dk, (hh + 1) * dk) for hh in range(heads)]
        outs = _glr_heads([q[:, cs] for cs in cols], [key[:, cs] for cs in cols],
                          [v[:, cs] for cs in cols], [cum[:, cs] for cs in cols], s_ref, sub)
        for o, cs in zip(outs, cols):
            o = o * lax.rsqrt(jnp.mean(o * o, axis=-1, keepdims=True) + EPS) * gain_ref[:, cs]
            o_ref[rows, cs] = (o * _silu(g_ref[rows, cs].astype(F32))).astype(o_ref.dtype)
        return carry

    lax.fori_loop(0, q_ref.shape[0] // chunk, body, 0, unroll=2)


def hgrn2_heads(proj, lb, out_gain, batch, seq, tb=REC_TB, chunk=REC_CHUNK, sub=HGRN_SUB):
    m = proj.shape[0]
    tb = min(tb, seq)
    nt = seq // tb
    h, dk = A_HEADS, A_DK
    w = h * dk

    def col(group):
        return pl.BlockSpec((tb, w), lambda b, t: (b * nt + t, group))

    vec = pl.BlockSpec((1, w), lambda b, t: (0, 0))
    return pl.pallas_call(
        functools.partial(_hgrn_kernel, chunk=chunk, sub=sub),
        grid=(batch, nt),
        in_specs=[col(0), col(1), col(2), col(3), vec, vec],
        out_specs=pl.BlockSpec((tb, w), lambda b, t: (b * nt + t, 0)),
        out_shape=jax.ShapeDtypeStruct((m, w), BF16),
        scratch_shapes=[pltpu.VMEM((h, dk, dk), F32)],
        compiler_params=_params("parallel", "arbitrary"),
        name="hgrn2_recurrence",
    )(proj, proj, proj, proj, lb.reshape(1, w), out_gain.reshape(1, w))


def _gla_kernel(q_ref, k_ref, v_ref, g_ref, low_ref, wup_ref, bias_ref, gain_ref, o_ref, s_ref,
                *, chunk, sub):
    @pl.when(pl.program_id(1) == 0)
    def _():
        s_ref[...] = jnp.zeros_like(s_ref)

    heads, dk, dv = s_ref.shape
    tril = _tril_bf16(chunk)
    q_scale = dk ** -0.5

    def body(ci, carry):
        rows = pl.ds(pl.multiple_of(ci * chunk, chunk), chunk)
        z = _dot(low_ref[rows, :].astype(BF16), wup_ref[...]) + bias_ref[...]
        cum = _cumsum_rows(_log_sigmoid(z) * (1.0 / GLA_GATE_NORMALIZER), tril)
        kcs = [slice(hh * dk, (hh + 1) * dk) for hh in range(heads)]
        vcs = [slice(hh * dv, (hh + 1) * dv) for hh in range(heads)]
        outs = _glr_heads([q_ref[rows, kc].astype(F32) * q_scale for kc in kcs],
                          [k_ref[rows, kc].astype(F32) for kc in kcs],
                          [v_ref[rows, vc] for vc in vcs], [cum[:, kc] for kc in kcs], s_ref, sub)
        for o, vc in zip(outs, vcs):
            o = o * lax.rsqrt(jnp.mean(o * o, axis=-1, keepdims=True) + EPS) * gain_ref[...]
            o_ref[rows, vc] = (o * _silu(g_ref[rows, vc].astype(F32))).astype(o_ref.dtype)
        return carry

    lax.fori_loop(0, q_ref.shape[0] // chunk, body, 0)


def gla_heads(proj, low, w_up, b_gate, out_gain, batch, seq, tb=REC_TB, chunk=GLA_CHUNK,
              sub=GLA_SUB):
    m = proj.shape[0]
    tb = min(tb, seq)
    nt = seq // tb
    h, dk, dv = C_HEADS, C_DK, C_DV
    kw, vw = h * dk, h * dv
    v0 = 2 * kw // vw
    return pl.pallas_call(
        functools.partial(_gla_kernel, chunk=chunk, sub=sub),
        grid=(batch, nt),
        in_specs=[
            pl.BlockSpec((tb, kw), lambda b, t: (b * nt + t, 0)),
            pl.BlockSpec((tb, kw), lambda b, t: (b * nt + t, 1)),
            pl.BlockSpec((tb, vw), lambda b, t: (b * nt + t, v0)),
            pl.BlockSpec((tb, vw), lambda b, t: (b * nt + t, v0 + 1)),
            pl.BlockSpec((tb, LANES), lambda b, t: (b * nt + t, 0)),
            pl.BlockSpec((LANES, kw), lambda b, t: (0, 0)),
            pl.BlockSpec((1, kw), lambda b, t: (0, 0)),
            pl.BlockSpec((1, dv), lambda b, t: (0, 0)),
        ],
        out_specs=pl.BlockSpec((tb, vw), lambda b, t: (b * nt + t, 0)),
        out_shape=jax.ShapeDtypeStruct((m, vw), BF16),
        scratch_shapes=[pltpu.VMEM((h, dk, dv), F32)],
        compiler_params=_params("parallel", "arbitrary"),
        name="gla_recurrence",
    )(proj, proj, proj, proj, low, w_up, b_gate.reshape(1, kw), out_gain.reshape(1, dv))


LOG2E = 1.4426950408889634
F_PIECES = 3


def _fox_prep_kernel(q_ref, k_ref, logit_ref, bias_ref, qg_ref, kg_ref,
                     qo_ref, ko_ref, carry_ref, *, heads, head_dim):
    @pl.when(pl.program_id(1) == 0)
    def _():
        carry_ref[...] = jnp.zeros_like(carry_ref)

    tb = q_ref.shape[0]
    log_f = _log_sigmoid(logit_ref[...] + bias_ref[...])
    cum = _cumsum_rows(log_f, _tril_bf16(tb)) + carry_ref[0:1, :]
    carry_ref[...] = jnp.broadcast_to(cum[tb - 1:tb, :], carry_ref.shape)
    pieces = [p.astype(F32) for p in _split3(-LOG2E * cum)]
    lane = lax.broadcasted_iota(jnp.int32, (tb, head_dim), 1)
    one_cols = jnp.where(lane < F_PIECES, 1.0, 0.0).astype(qo_ref.dtype)

    qg = qg_ref[...] * (head_dim ** -0.5 * LOG2E)
    kg = kg_ref[...]
    for hh in range(heads):
        cols = slice(hh * head_dim, (hh + 1) * head_dim)
        lo = 2 * hh * head_dim
        main = slice(lo, lo + head_dim)
        extra = slice(lo + head_dim, lo + 2 * head_dim)
        q = q_ref[:, cols].astype(F32)
        qo_ref[:, main] = (q * lax.rsqrt(jnp.mean(q * q, axis=-1, keepdims=True) + EPS) * qg
                           ).astype(qo_ref.dtype)
        qo_ref[:, extra] = one_cols
        k = k_ref[:, cols].astype(F32)
        ko_ref[:, main] = (k * lax.rsqrt(jnp.mean(k * k, axis=-1, keepdims=True) + EPS) * kg
                           ).astype(ko_ref.dtype)
        bias_cols = jnp.zeros((tb, head_dim), F32)
        for j, piece in enumerate(pieces):
            bias_cols = jnp.where(lane == j, piece[:, hh:hh + 1], bias_cols)
        ko_ref[:, extra] = bias_cols.astype(ko_ref.dtype)


def fox_prep(proj, logits, f_bias, q_gain, k_gain, batch, seq, tb=PREP_TB):
    m = proj.shape[0]
    h, hd = B_HEADS, B_HEAD_DIM
    w = h * hd
    tb = min(tb, seq)
    nt = seq // tb
    base = (4 * A_HEADS * A_DK) // w
    bias = jnp.zeros((1, LANES), F32).at[0, :h].set(f_bias)

    def col(group):
        return pl.BlockSpec((tb, w), lambda b, t: (b * nt + t, base + group))

    wide_tok = pl.BlockSpec((tb, 2 * w), lambda b, t: (b * nt + t, 0))
    return pl.pallas_call(
        functools.partial(_fox_prep_kernel, heads=h, head_dim=hd),
        grid=(batch, nt),
        in_specs=[
            col(0), col(1),
            pl.BlockSpec((tb, LANES), lambda b, t: (b * nt + t, 0)),
            pl.BlockSpec((1, LANES), lambda b, t: (0, 0)),
            pl.BlockSpec((1, hd), lambda b, t: (0, 0)),
            pl.BlockSpec((1, hd), lambda b, t: (0, 0)),
        ],
        out_specs=[wide_tok, wide_tok],
        out_shape=[jax.ShapeDtypeStruct((m, 2 * w), BF16)] * 2,
        scratch_shapes=[pltpu.VMEM((8, LANES), F32)],
        compiler_params=_params("parallel", "arbitrary"),
        name="fox_prep",
    )(proj, proj, logits, bias, q_gain.reshape(1, hd), k_gain.reshape(1, hd))


def _fox_attn_kernel(q_ref, k_ref, v_ref, g_ref, o_ref, m_ref, acc_ref, *, tq, tk, heads, head_dim):
    qi = pl.program_id(2)
    q_start = pl.multiple_of(qi * tq, tq)
    wide = 2 * head_dim
    ones = jnp.ones((tk, head_dim), BF16)

    def tile(start, mask):
        rows = pl.ds(start, tk)
        scores = [_dot_nt(q_ref[:, hh * wide:(hh + 1) * wide], k_ref[rows, hh * wide:(hh + 1) * wide])
                  for hh in range(heads)]
        for hh, s in enumerate(scores):
            if mask is not None:
                s = jnp.where(mask, s, MASK_VALUE)
            m_prev = m_ref[hh]
            m_new = jnp.maximum(m_prev, jnp.max(s, axis=-1, keepdims=True))
            p = jnp.exp2(s - jnp.concatenate([m_new] * (tk // LANES), axis=1))
            alpha = jnp.exp2(m_prev - m_new)
            m_ref[hh] = m_new
            v_aug = jnp.concatenate([v_ref[rows, hh * head_dim:(hh + 1) * head_dim], ones], axis=1)
            acc_ref[hh] = (jnp.concatenate([alpha] * (wide // LANES), axis=1) * acc_ref[hh]
                           + _dot(p.astype(BF16), v_aug))

    m_ref[...] = jnp.full(m_ref.shape, MASK_VALUE, F32)
    acc_ref[...] = jnp.zeros_like(acc_ref)

    def body(ki, carry):
        tile(pl.multiple_of(ki * tk, tk), None)
        return carry

    lax.fori_loop(0, qi * (tq // tk), body, 0)
    for d in range(tq // tk):
        row = lax.broadcasted_iota(jnp.int32, (tq, tk), 0)
        col = lax.broadcasted_iota(jnp.int32, (tq, tk), 1) + d * tk
        tile(pl.multiple_of(q_start + d * tk, tk), col <= row)
    for hh in range(heads):
        acc = acc_ref[hh]
        cols = slice(hh * head_dim, (hh + 1) * head_dim)
        o_ref[:, cols] = (acc[:, :head_dim] / acc[:, head_dim:]
                          * _sigmoid(g_ref[:, cols].astype(F32))).astype(o_ref.dtype)


def fox_attention(q_aug, k_aug, proj, batch, seq, tq=ATT_TQ, tk=ATT_TK, group=ATT_HEADS):
    m = q_aug.shape[0]
    h, hd = B_HEADS, B_HEAD_DIM
    tq = min(tq, seq)
    tk = min(tk, tq)
    nq = seq // tq
    v_block = (4 * A_HEADS * A_DK + 2 * h * hd) // (group * hd)
    gate_block = v_block + h // group
    return pl.pallas_call(
        functools.partial(_fox_attn_kernel, tq=tq, tk=tk, heads=group, head_dim=hd),
        grid=(batch, h // group, nq),
        in_specs=[
            pl.BlockSpec((tq, 2 * group * hd), lambda b, hg, i: (b * nq + i, hg)),
            pl.BlockSpec((seq, 2 * group * hd), lambda b, hg, i: (b, hg),
                         pipeline_mode=pl.Buffered(1)),
            pl.BlockSpec((seq, group * hd), lambda b, hg, i: (b, v_block + hg),
                         pipeline_mode=pl.Buffered(1)),
            pl.BlockSpec((tq, group * hd), lambda b, hg, i: (b * nq + i, gate_block + hg)),
        ],
        out_specs=pl.BlockSpec((tq, group * hd), lambda b, hg, i: (b * nq + i, hg)),
        out_shape=jax.ShapeDtypeStruct((m, h * hd), BF16),
        scratch_shapes=[pltpu.VMEM((group, tq, LANES), F32), pltpu.VMEM((group, tq, 2 * hd), F32)],
        compiler_params=_params("parallel", "parallel", "arbitrary"),
        name="fox_attention",
    )(q_aug, k_aug, proj, proj)


def _pad_cols(w, n):
    return jnp.zeros((w.shape[0], n), w.dtype).at[:, :w.shape[1]].set(w)


def kernel(x, c, mod_w, mod_b, norm_mix_gain, norm_ffn_gain, ab_w_in, ab_w_out, hgrn_lb_logits,
           hgrn_out_gain, fox_q_gain, fox_k_gain, fox_f_bias, gla_w_in, gla_w_gate_up, gla_b_gate,
           gla_out_gain, gla_w_out, ffn_w_in, ffn_w_out):
    batch, seq, d = x.shape
    depth = mod_w.shape[0]
    ab_main = 4 * A_HEADS * A_DK + 4 * B_HEADS * B_HEAD_DIM
    c_main = 2 * C_HEADS * C_DK + 2 * C_HEADS * C_DV

    lbs = lower_bounds(hgrn_lb_logits)
    mod = modulation(c, mod_w, mod_b)
    xs = x.reshape(batch * seq, d)

    for layer in range(depth):
        sh1, sc1, g1, sh2, sc2, g2 = [
            mod[layer, :, i * d:(i + 1) * d].reshape(batch, 1, d) for i in range(6)]
        if layer % 2 == 0:
            i = layer // 2
            proj, logits = norm_proj(xs, norm_mix_gain[layer], sh1, sc1, ab_w_in, i, ab_main,
                                     _pad_cols(ab_w_in[i, :, ab_main:], LANES).astype(BF16), seq)
            o_a = hgrn2_heads(proj, lbs[i], hgrn_out_gain[i], batch, seq)
            q_aug, k_aug = fox_prep(proj, logits, fox_f_bias[i], fox_q_gain[i], fox_k_gain[i],
                                    batch, seq)
            o_b = fox_attention(q_aug, k_aug, proj, batch, seq)
            xs = out_proj_residual([o_a, o_b], ab_w_out, i, xs, g1, seq)
        else:
            j = layer // 2
            proj, low = norm_proj(xs, norm_mix_gain[layer], sh1, sc1, gla_w_in, j, c_main,
                                  _pad_cols(gla_w_in[j, :, c_main:], LANES).astype(BF16), seq)
            w_up = jnp.zeros((LANES, gla_w_gate_up.shape[2]), BF16).at[:C_GATE_RANK].set(
                gla_w_gate_up[j].astype(BF16))
            o_c = gla_heads(proj, low, w_up, gla_b_gate[j], gla_out_gain[j], batch, seq)
            xs = out_proj_residual([o_c], gla_w_out, j, xs, g1, seq)
        xs = ffn_residual(xs, norm_ffn_gain[layer], sh2, sc2, g2,
                          ffn_w_in[layer].astype(BF16), ffn_w_out[layer].astype(BF16), seq)
    return xs.reshape(batch, seq, d)
```

```python
import functools

import jax
import jax.numpy as jnp
from jax import lax
from jax.experimental import pallas as pl
from jax.experimental.pallas import tpu as pltpu

F32 = jnp.float32
BF16 = jnp.bfloat16

EPS = 1e-6
MIN_FORGET = 1e-30
MASK_VALUE = -1e30
GLA_GATE_NORMALIZER = 16.0

LANES = 128
VMEM_LIMIT = 60 * 1024 * 1024

A_HEADS = 8
A_DK = 128
B_HEADS = 8
B_HEAD_DIM = 128
C_HEADS = 4
C_DK = 256
C_DV = 512
C_GATE_RANK = 16

PROJ_TM = 1024
PROJ_TN = 1024
OUT_TM = 512
FFN_TM = 1024
FFN_TF = 512
FFN_TN = 512
NORM_ROWS = 128
REC_TB = 512
REC_CHUNK = 64
HGRN_SUB = 16
GLA_CHUNK = 256
GLA_SUB = 64
ATT_TQ = 512
ATT_TK = 512
ATT_HEADS = 4
PREP_TB = 512


def _params(*sem):
    return pltpu.CompilerParams(dimension_semantics=sem, vmem_limit_bytes=VMEM_LIMIT)


def _sigmoid(x):
    return 1.0 / (1.0 + jnp.exp(-x))


def _silu(x):
    return x * _sigmoid(x)


def _log_sigmoid(x):
    return jnp.minimum(x, 0.0) - jnp.log1p(jnp.exp(-jnp.abs(x)))


def _split3(x):
    hi = x.astype(BF16)
    r = x - hi.astype(F32)
    mid = r.astype(BF16)
    lo = (r - mid.astype(F32)).astype(BF16)
    return hi, mid, lo


def _dot(a, b):
    return jnp.dot(a, b, preferred_element_type=F32)


def _dot_nt(a, b):
    return lax.dot_general(a, b, (((1,), (1,)), ((), ())), preferred_element_type=F32)


def _dot_tn(a, b):
    return lax.dot_general(a, b, (((0,), (0,)), ((), ())), preferred_element_type=F32)


def _lower_bound_kernel(logit_ref, o_ref):
    z = logit_ref[...]
    n = z.shape[0]
    e = jnp.exp(z - jnp.max(z, axis=0, keepdims=True))
    p = e / jnp.sum(e, axis=0, keepdims=True)
    run = jnp.zeros_like(p[0:1])
    rows = [run]
    for k in range(1, n):
        run = run + p[k:k + 1]
        rows.append(run)
    o_ref[...] = jnp.clip(jnp.concatenate(rows, axis=0), 0.0, 1.0 - 1e-6)


def lower_bounds(logits):
    return pl.pallas_call(
        _lower_bound_kernel,
        out_shape=jax.ShapeDtypeStruct(logits.shape, F32),
        name="hgrn_lower_bounds",
    )(logits.astype(F32))


def _mod_kernel(c_ref, w_ref, b_ref, o_ref):
    s = _silu(c_ref[...]).astype(BF16)
    o_ref[0] = _dot(s, w_ref[0].astype(BF16)) + b_ref[0]


def modulation(c, mod_w, mod_b, tn=2048):
    n_layers, d, n = mod_w.shape
    b = c.shape[0]
    rows = 8
    c_pad = jnp.zeros((rows, d), F32).at[:b].set(c)
    out = pl.pallas_call(
        _mod_kernel,
        grid=(n_layers, n // tn),
        in_specs=[
            pl.BlockSpec((rows, d), lambda l, j: (0, 0)),
            pl.BlockSpec((1, d, tn), lambda l, j: (l, 0, j)),
            pl.BlockSpec((1, 1, tn), lambda l, j: (l, 0, j)),
        ],
        out_specs=pl.BlockSpec((1, rows, tn), lambda l, j: (l, 0, j)),
        out_shape=jax.ShapeDtypeStruct((n_layers, rows, n), F32),
        compiler_params=_params("parallel", "parallel"),
        name="adaln_modulation",
    )(c_pad, mod_w, mod_b.reshape(n_layers, 1, n))
    return out[:, :b]


def _norm_modulate(x, gain, shift, scale):
    var = jnp.mean(x * x, axis=-1, keepdims=True)
    y = x * lax.rsqrt(var + EPS) * gain
    return y * (1.0 + scale) + shift


def _norm_rows(x_ref, gain_ref, sh_ref, sc_ref, h_ref, copy_ref=None):
    def body(r, carry):
        rows = pl.ds(pl.multiple_of(r * NORM_ROWS, NORM_ROWS), NORM_ROWS)
        x = x_ref[rows, :]
        h_ref[rows, :] = _norm_modulate(x, gain_ref[...], sh_ref[0], sc_ref[0]).astype(h_ref.dtype)
        if copy_ref is not None:
            copy_ref[rows, :] = x
        return carry

    lax.fori_loop(0, x_ref.shape[0] // NORM_ROWS, body, 0)


def _norm_proj_kernel(x_ref, gain_ref, sh_ref, sc_ref, w_ref, waux_ref, o_ref, oaux_ref, h_ref):
    @pl.when(pl.program_id(1) == 0)
    def _():
        _norm_rows(x_ref, gain_ref, sh_ref, sc_ref, h_ref)
        oaux_ref[...] = _dot(h_ref[...], waux_ref[...])

    o_ref[...] = _dot(h_ref[...], w_ref[...]).astype(o_ref.dtype)


def norm_proj(x, gain, shift, scale, w, w_aux, seq, tm=PROJ_TM, tn=PROJ_TN):
    m, d = x.shape
    n = w.shape[1]
    tm = min(tm, seq)
    per_batch = seq // tm
    return pl.pallas_call(
        _norm_proj_kernel,
        grid=(m // tm, n // tn),
        in_specs=[
            pl.BlockSpec((tm, d), lambda i, j: (i, 0)),
            pl.BlockSpec((1, d), lambda i, j: (0, 0)),
            pl.BlockSpec((1, 1, d), lambda i, j: (i // per_batch, 0, 0)),
            pl.BlockSpec((1, 1, d), lambda i, j: (i // per_batch, 0, 0)),
            pl.BlockSpec((d, tn), lambda i, j: (0, j)),
            pl.BlockSpec((d, LANES), lambda i, j: (0, 0)),
        ],
        out_specs=[
            pl.BlockSpec((tm, tn), lambda i, j: (i, j)),
            pl.BlockSpec((tm, LANES), lambda i, j: (i, 0)),
        ],
        out_shape=[
            jax.ShapeDtypeStruct((m, n), BF16),
            jax.ShapeDtypeStruct((m, LANES), F32),
        ],
        scratch_shapes=[pltpu.VMEM((tm, d), BF16)],
        compiler_params=_params("parallel", "arbitrary"),
        name="norm_in_proj",
    )(x, gain.reshape(1, d), shift, scale, w, w_aux)


def _out_proj_kernel(*refs, n_lhs):
    lhs = refs[:n_lhs]
    w_ref, x_ref, gate_ref, o_ref = refs[n_lhs:]
    width = lhs[0].shape[1]
    y = _dot(lhs[0][...], w_ref[0:width, :])
    for part, a_ref in enumerate(lhs[1:], 1):
        y = y + _dot(a_ref[...], w_ref[part * width:(part + 1) * width, :])
    o_ref[...] = x_ref[...] + gate_ref[0] * y


def out_proj_residual(lhs_list, w, x, gate, seq, tm=OUT_TM):
    m, d = x.shape
    tm = min(tm, seq)
    per_batch = seq // tm
    n_lhs = len(lhs_list)
    width = lhs_list[0].shape[1]
    assert all(a.shape[1] == width for a in lhs_list) and n_lhs * width == w.shape[0]
    in_specs = (
        [pl.BlockSpec((tm, width), lambda i: (i, 0)) for _ in lhs_list]
        + [pl.BlockSpec(w.shape, lambda i: (0, 0), pipeline_mode=pl.Buffered(1)),
           pl.BlockSpec((tm, d), lambda i: (i, 0)),
           pl.BlockSpec((1, 1, d), lambda i: (i // per_batch, 0, 0))]
    )
    return pl.pallas_call(
        functools.partial(_out_proj_kernel, n_lhs=n_lhs),
        grid=(m // tm,),
        in_specs=in_specs,
        out_specs=pl.BlockSpec((tm, d), lambda i: (i, 0)),
        out_shape=jax.ShapeDtypeStruct((m, d), F32),
        compiler_params=_params("parallel"),
        name="out_proj_residual",
    )(*lhs_list, w, x, gate)


def _ffn_kernel(x_ref, gain_ref, sh_ref, sc_ref, gate_ref, wa_ref, wu_ref, wo_ref, o_ref, h_ref):
    @pl.when(pl.program_id(1) == 0)
    def _():
        _norm_rows(x_ref, gain_ref, sh_ref, sc_ref, h_ref, copy_ref=o_ref)

    h = h_ref[...]
    a = _dot(h, wa_ref[...])
    u = _dot(h, wu_ref[...])
    act = (_silu(a) * u).astype(BF16)
    d = o_ref.shape[1]
    for c0 in range(0, d, FFN_TN):
        cols = slice(c0, c0 + FFN_TN)
        o_ref[:, cols] += gate_ref[0][:, cols] * _dot(act, wo_ref[:, cols])


def ffn_residual(x, gain, shift, scale, gate, w_in, w_out, seq, tm=FFN_TM, tf=FFN_TF):
    m, d = x.shape
    d_ff = w_out.shape[0]
    tm = min(tm, seq)
    per_batch = seq // tm
    nf = d_ff // tf
    vec = pl.BlockSpec((1, 1, d), lambda i, f: (i // per_batch, 0, 0))
    return pl.pallas_call(
        _ffn_kernel,
        grid=(m // tm, nf),
        in_specs=[
            pl.BlockSpec((tm, d), lambda i, f: (i, 0)),
            pl.BlockSpec((1, d), lambda i, f: (0, 0)),
            vec, vec, vec,
            pl.BlockSpec((d, tf), lambda i, f: (0, f)),
            pl.BlockSpec((d, tf), lambda i, f: (0, nf + f)),
            pl.BlockSpec((tf, d), lambda i, f: (f, 0)),
        ],
        out_specs=pl.BlockSpec((tm, d), lambda i, f: (i, 0)),
        out_shape=jax.ShapeDtypeStruct((m, d), F32),
        scratch_shapes=[pltpu.VMEM((tm, d), BF16)],
        compiler_params=_params("parallel", "arbitrary"),
        name="swiglu_ffn_residual",
    )(x, gain.reshape(1, d), shift, scale, gate, w_in, w_in, w_out)


def _cumsum_rows(g, tril):
    n = g.shape[1]
    y = _dot(tril, jnp.concatenate(_split3(g), axis=1))
    return y[:, :n] + y[:, n:2 * n] + y[:, 2 * n:]


def _glr_heads(qs, ks, vs, cums, s_ref, sub):
    heads = range(len(qs))
    c, dk = qs[0].shape
    dv = vs[0].shape[1]
    vbs = [v.astype(BF16) for v in vs]
    lasts = [cum[c - 1:c, :] for cum in cums]

    inter = [_dot((qs[h] * jnp.exp(cums[h])).astype(BF16), s_ref[h].astype(BF16)) for h in heads]
    updates = [_dot_tn((ks[h] * jnp.exp(lasts[h] - cums[h])).astype(BF16), vbs[h]) for h in heads]
    for h in heads:
        decay_cols = jnp.transpose(jnp.broadcast_to(jnp.exp(lasts[h]), (LANES, dk)))
        s_ref[h] = jnp.concatenate([decay_cols] * (dv // LANES), axis=1) * s_ref[h] + updates[h]

    spans = [(r0, r0 + sub) for r0 in range(0, c, sub)]
    scores = []
    for h in heads:
        q, k, cum = qs[h], ks[h], cums[h]
        per_head = []
        for r0, r1 in spans:
            hi = cum[r1 - 1:r1, :]
            mid = 0.5 * hi if r0 == 0 else 0.5 * (cum[r0 - 1:r0, :] + hi)
            qi = (q[r0:r1] * jnp.exp(cum[r0:r1] - mid)).astype(BF16)
            ki = (k[:r1] * jnp.exp(mid - cum[:r1])).astype(BF16)
            a = _dot_nt(qi, ki)
            row = r0 + lax.broadcasted_iota(jnp.int32, (sub, r1), 0)
            col = lax.broadcasted_iota(jnp.int32, (sub, r1), 1)
            per_head.append(jnp.where(col <= row, a, 0.0).astype(BF16))
        scores.append(per_head)
    outs = []
    for h in heads:
        blocks = [_dot(a, vbs[h][:r1]) for a, (_, r1) in zip(scores[h], spans)]
        intra = blocks[0] if len(blocks) == 1 else jnp.concatenate(blocks, axis=0)
        outs.append(inter[h] + intra)
    return outs


def _tril_bf16(c):
    row = lax.broadcasted_iota(jnp.int32, (c, c), 0)
    col = lax.broadcasted_iota(jnp.int32, (c, c), 1)
    return jnp.where(col <= row, 1.0, 0.0).astype(BF16)


def _hgrn_kernel(q_ref, f_ref, i_ref, g_ref, lb_ref, gain_ref, o_ref, s_ref, *, chunk, sub):
    @pl.when(pl.program_id(1) == 0)
    def _():
        s_ref[...] = jnp.zeros_like(s_ref)

    heads, dk, _ = s_ref.shape
    tril = _tril_bf16(chunk)

    def body(ci, carry):
        rows = pl.ds(pl.multiple_of(ci * chunk, chunk), chunk)
        lb = lb_ref[...]
        sig = _sigmoid(f_ref[rows, :].astype(F32))
        forget = lb + (1.0 - lb) * sig
        cum = _cumsum_rows(jnp.log(jnp.maximum(forget, MIN_FORGET)), tril)
        key = (1.0 - lb) * (1.0 - sig)
        q = _silu(q_ref[rows, :].astype(F32))
        v = i_ref[rows, :]
        cols = [slice(hh * dk, (hh + 1) * dk) for hh in range(heads)]
        outs = _glr_heads([q[:, cs] for cs in cols], [key[:, cs] for cs in cols],
                          [v[:, cs] for cs in cols], [cum[:, cs] for cs in cols], s_ref, sub)
        for o, cs in zip(outs, cols):
            o = o * lax.rsqrt(jnp.mean(o * o, axis=-1, keepdims=True) + EPS) * gain_ref[:, cs]
            o_ref[rows, cs] = (o * _silu(g_ref[rows, cs].astype(F32))).astype(o_ref.dtype)
        return carry

    lax.fori_loop(0, q_ref.shape[0] // chunk, body, 0, unroll=2)


def hgrn2_heads(proj, lb, out_gain, batch, seq, tb=REC_TB, chunk=REC_CHUNK, sub=HGRN_SUB):
    m = proj.shape[0]
    tb = min(tb, seq)
    nt = seq // tb
    h, dk = A_HEADS, A_DK
    w = h * dk

    def col(group):
        return pl.BlockSpec((tb, w), lambda b, t: (b * nt + t, group))

    vec = pl.BlockSpec((1, w), lambda b, t: (0, 0))
    return pl.pallas_call(
        functools.partial(_hgrn_kernel, chunk=chunk, sub=sub),
        grid=(batch, nt),
        in_specs=[col(0), col(1), col(2), col(3), vec, vec],
        out_specs=pl.BlockSpec((tb, w), lambda b, t: (b * nt + t, 0)),
        out_shape=jax.ShapeDtypeStruct((m, w), BF16),
        scratch_shapes=[pltpu.VMEM((h, dk, dk), F32)],
        compiler_params=_params("parallel", "arbitrary"),
        name="hgrn2_recurrence",
    )(proj, proj, proj, proj, lb.reshape(1, w), out_gain.reshape(1, w))


def _gla_kernel(q_ref, k_ref, v_ref, g_ref, low_ref, wup_ref, bias_ref, gain_ref, o_ref, s_ref,
                *, chunk, sub):
    @pl.when(pl.program_id(1) == 0)
    def _():
        s_ref[...] = jnp.zeros_like(s_ref)

    heads, dk, dv = s_ref.shape
    tril = _tril_bf16(chunk)
    q_scale = dk ** -0.5

    def body(ci, carry):
        rows = pl.ds(pl.multiple_of(ci * chunk, chunk), chunk)
        z = _dot(low_ref[rows, :].astype(BF16), wup_ref[...]) + bias_ref[...]
        cum = _cumsum_rows(_log_sigmoid(z) * (1.0 / GLA_GATE_NORMALIZER), tril)
        kcs = [slice(hh * dk, (hh + 1) * dk) for hh in range(heads)]
        vcs = [slice(hh * dv, (hh + 1) * dv) for hh in range(heads)]
        outs = _glr_heads([q_ref[rows, kc].astype(F32) * q_scale for kc in kcs],
                          [k_ref[rows, kc].astype(F32) for kc in kcs],
                          [v_ref[rows, vc] for vc in vcs], [cum[:, kc] for kc in kcs], s_ref, sub)
        for o, vc in zip(outs, vcs):
            o = o * lax.rsqrt(jnp.mean(o * o, axis=-1, keepdims=True) + EPS) * gain_ref[...]
            o_ref[rows, vc] = (o * _silu(g_ref[rows, vc].astype(F32))).astype(o_ref.dtype)
        return carry

    lax.fori_loop(0, q_ref.shape[0] // chunk, body, 0)


def gla_heads(proj, low, w_up, b_gate, out_gain, batch, seq, tb=REC_TB, chunk=GLA_CHUNK,
              sub=GLA_SUB):
    m = proj.shape[0]
    tb = min(tb, seq)
    nt = seq // tb
    h, dk, dv = C_HEADS, C_DK, C_DV
    kw, vw = h * dk, h * dv
    v0 = 2 * kw // vw
    return pl.pallas_call(
        functools.partial(_gla_kernel, chunk=chunk, sub=sub),
        grid=(batch, nt),
        in_specs=[
            pl.BlockSpec((tb, kw), lambda b, t: (b * nt + t, 0)),
            pl.BlockSpec((tb, kw), lambda b, t: (b * nt + t, 1)),
            pl.BlockSpec((tb, vw), lambda b, t: (b * nt + t, v0)),
            pl.BlockSpec((tb, vw), lambda b, t: (b * nt + t, v0 + 1)),
            pl.BlockSpec((tb, LANES), lambda b, t: (b * nt + t, 0)),
            pl.BlockSpec((LANES, kw), lambda b, t: (0, 0)),
            pl.BlockSpec((1, kw), lambda b, t: (0, 0)),
            pl.BlockSpec((1, dv), lambda b, t: (0, 0)),
        ],
        out_specs=pl.BlockSpec((tb, vw), lambda b, t: (b * nt + t, 0)),
        out_shape=jax.ShapeDtypeStruct((m, vw), BF16),
        scratch_shapes=[pltpu.VMEM((h, dk, dv), F32)],
        compiler_params=_params("parallel", "arbitrary"),
        name="gla_recurrence",
    )(proj, proj, proj, proj, low, w_up, b_gate.reshape(1, kw), out_gain.reshape(1, dv))


LOG2E = 1.4426950408889634
F_PIECES = 3


def _fox_prep_kernel(q_ref, k_ref, logit_ref, bias_ref, qg_ref, kg_ref,
                     qo_ref, ko_ref, carry_ref, *, heads, head_dim):
    @pl.when(pl.program_id(1) == 0)
    def _():
        carry_ref[...] = jnp.zeros_like(carry_ref)

    tb = q_ref.shape[0]
    log_f = _log_sigmoid(logit_ref[...] + bias_ref[...])
    cum = _cumsum_rows(log_f, _tril_bf16(tb)) + carry_ref[0:1, :]
    carry_ref[...] = jnp.broadcast_to(cum[tb - 1:tb, :], carry_ref.shape)
    pieces = [p.astype(F32) for p in _split3(-LOG2E * cum)]
    lane = lax.broadcasted_iota(jnp.int32, (tb, head_dim), 1)
    one_cols = jnp.where(lane < F_PIECES, 1.0, 0.0).astype(qo_ref.dtype)

    qg = qg_ref[...] * (head_dim ** -0.5 * LOG2E)
    kg = kg_ref[...]
    for hh in range(heads):
        cols = slice(hh * head_dim, (hh + 1) * head_dim)
        lo = 2 * hh * head_dim
        main = slice(lo, lo + head_dim)
        extra = slice(lo + head_dim, lo + 2 * head_dim)
        q = q_ref[:, cols].astype(F32)
        qo_ref[:, main] = (q * lax.rsqrt(jnp.mean(q * q, axis=-1, keepdims=True) + EPS) * qg
                           ).astype(qo_ref.dtype)
        qo_ref[:, extra] = one_cols
        k = k_ref[:, cols].astype(F32)
        ko_ref[:, main] = (k * lax.rsqrt(jnp.mean(k * k, axis=-1, keepdims=True) + EPS) * kg
                           ).astype(ko_ref.dtype)
        bias_cols = jnp.zeros((tb, head_dim), F32)
        for j, piece in enumerate(pieces):
            bias_cols = jnp.where(lane == j, piece[:, hh:hh + 1], bias_cols)
        ko_ref[:, extra] = bias_cols.astype(ko_ref.dtype)


def fox_prep(proj, logits, f_bias, q_gain, k_gain, batch, seq, tb=PREP_TB):
    m = proj.shape[0]
    h, hd = B_HEADS, B_HEAD_DIM
    w = h * hd
    tb = min(tb, seq)
    nt = seq // tb
    base = (4 * A_HEADS * A_DK) // w
    bias = jnp.zeros((1, LANES), F32).at[0, :h].set(f_bias)

    def col(group):
        return pl.BlockSpec((tb, w), lambda b, t: (b * nt + t, base + group))

    wide_tok = pl.BlockSpec((tb, 2 * w), lambda b, t: (b * nt + t, 0))
    return pl.pallas_call(
        functools.partial(_fox_prep_kernel, heads=h, head_dim=hd),
        grid=(batch, nt),
        in_specs=[
            col(0), col(1),
            pl.BlockSpec((tb, LANES), lambda b, t: (b * nt + t, 0)),
            pl.BlockSpec((1, LANES), lambda b, t: (0, 0)),
            pl.BlockSpec((1, hd), lambda b, t: (0, 0)),
            pl.BlockSpec((1, hd), lambda b, t: (0, 0)),
        ],
        out_specs=[wide_tok, wide_tok],
        out_shape=[jax.ShapeDtypeStruct((m, 2 * w), BF16)] * 2,
        scratch_shapes=[pltpu.VMEM((8, LANES), F32)],
        compiler_params=_params("parallel", "arbitrary"),
        name="fox_prep",
    )(proj, proj, logits, bias, q_gain.reshape(1, hd), k_gain.reshape(1, hd))


def _fox_attn_kernel(q_ref, k_ref, v_ref, g_ref, o_ref, m_ref, acc_ref, *, tq, tk, heads, head_dim):
    qi = pl.program_id(2)
    q_start = pl.multiple_of(qi * tq, tq)
    wide = 2 * head_dim
    ones = jnp.ones((tk, head_dim), BF16)

    def tile(start, mask):
        rows = pl.ds(start, tk)
        scores = [_dot_nt(q_ref[:, hh * wide:(hh + 1) * wide], k_ref[rows, hh * wide:(hh + 1) * wide])
                  for hh in range(heads)]
        for hh, s in enumerate(scores):
            if mask is not None:
                s = jnp.where(mask, s, MASK_VALUE)
            m_prev = m_ref[hh]
            m_new = jnp.maximum(m_prev, jnp.max(s, axis=-1, keepdims=True))
            p = jnp.exp2(s - jnp.concatenate([m_new] * (tk // LANES), axis=1))
            alpha = jnp.exp2(m_prev - m_new)
            m_ref[hh] = m_new
            v_aug = jnp.concatenate([v_ref[rows, hh * head_dim:(hh + 1) * head_dim], ones], axis=1)
            acc_ref[hh] = (jnp.concatenate([alpha] * (wide // LANES), axis=1) * acc_ref[hh]
                           + _dot(p.astype(BF16), v_aug))

    m_ref[...] = jnp.full(m_ref.shape, MASK_VALUE, F32)
    acc_ref[...] = jnp.zeros_like(acc_ref)

    def body(ki, carry):
        tile(pl.multiple_of(ki * tk, tk), None)
        return carry

    lax.fori_loop(0, qi * (tq // tk), body, 0)
    for d in range(tq // tk):
        row = lax.broadcasted_iota(jnp.int32, (tq, tk), 0)
        col = lax.broadcasted_iota(jnp.int32, (tq, tk), 1) + d * tk
        tile(pl.multiple_of(q_start + d * tk, tk), col <= row)
    for hh in range(heads):
        acc = acc_ref[hh]
        cols = slice(hh * head_dim, (hh + 1) * head_dim)
        o_ref[:, cols] = (acc[:, :head_dim] / acc[:, head_dim:]
                          * _sigmoid(g_ref[:, cols].astype(F32))).astype(o_ref.dtype)


def fox_attention(q_aug, k_aug, proj, batch, seq, tq=ATT_TQ, tk=ATT_TK, group=ATT_HEADS):
    m = q_aug.shape[0]
    h, hd = B_HEADS, B_HEAD_DIM
    tq = min(tq, seq)
    tk = min(tk, tq)
    nq = seq // tq
    v_block = (4 * A_HEADS * A_DK + 2 * h * hd) // (group * hd)
    gate_block = v_block + h // group
    return pl.pallas_call(
        functools.partial(_fox_attn_kernel, tq=tq, tk=tk, heads=group, head_dim=hd),
        grid=(batch, h // group, nq),
        in_specs=[
            pl.BlockSpec((tq, 2 * group * hd), lambda b, hg, i: (b * nq + i, hg)),
            pl.BlockSpec((seq, 2 * group * hd), lambda b, hg, i: (b, hg),
                         pipeline_mode=pl.Buffered(1)),
            pl.BlockSpec((seq, group * hd), lambda b, hg, i: (b, v_block + hg),
                         pipeline_mode=pl.Buffered(1)),
            pl.BlockSpec((tq, group * hd), lambda b, hg, i: (b * nq + i, gate_block + hg)),
        ],
        out_specs=pl.BlockSpec((tq, group * hd), lambda b, hg, i: (b * nq + i, hg)),
        out_shape=jax.ShapeDtypeStruct((m, h * hd), BF16),
        scratch_shapes=[pltpu.VMEM((group, tq, LANES), F32), pltpu.VMEM((group, tq, 2 * hd), F32)],
        compiler_params=_params("parallel", "parallel", "arbitrary"),
        name="fox_attention",
    )(q_aug, k_aug, proj, proj)


def _pad_cols(w, n):
    return jnp.zeros((w.shape[0], n), w.dtype).at[:, :w.shape[1]].set(w)


def kernel(x, c, mod_w, mod_b, norm_mix_gain, norm_ffn_gain, ab_w_in, ab_w_out, hgrn_lb_logits,
           hgrn_out_gain, fox_q_gain, fox_k_gain, fox_f_bias, gla_w_in, gla_w_gate_up, gla_b_gate,
           gla_out_gain, gla_w_out, ffn_w_in, ffn_w_out):
    batch, seq, d = x.shape
    depth = mod_w.shape[0]
    ab_main = 4 * A_HEADS * A_DK + 4 * B_HEADS * B_HEAD_DIM
    c_main = 2 * C_HEADS * C_DK + 2 * C_HEADS * C_DV

    lbs = lower_bounds(hgrn_lb_logits)
    mod = modulation(c, mod_w, mod_b)
    xs = x.reshape(batch * seq, d)

    for layer in range(depth):
        sh1, sc1, g1, sh2, sc2, g2 = [
            mod[layer, :, i * d:(i + 1) * d].reshape(batch, 1, d) for i in range(6)]
        if layer % 2 == 0:
            i = layer // 2
            proj, logits = norm_proj(xs, norm_mix_gain[layer], sh1, sc1,
                                     ab_w_in[i, :, :ab_main].astype(BF16),
                                     _pad_cols(ab_w_in[i, :, ab_main:], LANES).astype(BF16), seq)
            o_a = hgrn2_heads(proj, lbs[i], hgrn_out_gain[i], batch, seq)
            q_aug, k_aug = fox_prep(proj, logits, fox_f_bias[i], fox_q_gain[i], fox_k_gain[i],
                                    batch, seq)
            o_b = fox_attention(q_aug, k_aug, proj, batch, seq)
            xs = out_proj_residual([o_a, o_b], ab_w_out[i].astype(BF16), xs, g1, seq)
        else:
            j = layer // 2
            proj, low = norm_proj(xs, norm_mix_gain[layer], sh1, sc1,
                                  gla_w_in[j, :, :c_main].astype(BF16),
                                  _pad_cols(gla_w_in[j, :, c_main:], LANES).astype(BF16), seq)
            w_up = jnp.zeros((LANES, gla_w_gate_up.shape[2]), BF16).at[:C_GATE_RANK].set(
                gla_w_gate_up[j].astype(BF16))
            o_c = gla_heads(proj, low, w_up, gla_b_gate[j], gla_out_gain[j], batch, seq)
            xs = out_proj_residual([o_c], gla_w_out[j].astype(BF16), xs, g1, seq)
        xs = ffn_residual(xs, norm_ffn_gain[layer], sh2, sc2, g2,
                          ffn_w_in[layer].astype(BF16), ffn_w_out[layer].astype(BF16), seq)
    return xs.reshape(batch, seq, d)
```

```python
import functools

import jax
import jax.numpy as jnp
from jax import lax
from jax.experimental import pallas as pl
from jax.experimental.pallas import tpu as pltpu

F32 = jnp.float32
BF16 = jnp.bfloat16

EPS = 1e-6
MIN_FORGET = 1e-30
MASK_VALUE = -1e30
GLA_GATE_NORMALIZER = 16.0

LANES = 128
VMEM_LIMIT = 60 * 1024 * 1024

A_HEADS = 8
A_DK = 128
B_HEADS = 8
B_HEAD_DIM = 128
C_HEADS = 4
C_DK = 256
C_DV = 512
C_GATE_RANK = 16

PROJ_TM = 1024
PROJ_TN = 1024
OUT_TM = 512
FFN_TM = 1024
FFN_TF = 512
FFN_TN = 512
NORM_ROWS = 128
REC_TB = 512
REC_CHUNK = 64
HGRN_SUB = 16
GLA_CHUNK = 256
GLA_SUB = 64
ATT_TQ = 512
ATT_TK = 512
ATT_HEADS = 4
PREP_TB = 512


def _params(*sem):
    return pltpu.CompilerParams(dimension_semantics=sem, vmem_limit_bytes=VMEM_LIMIT)


def _sigmoid(x):
    return 1.0 / (1.0 + jnp.exp(-x))


def _silu(x):
    return x * _sigmoid(x)


def _log_sigmoid(x):
    return jnp.minimum(x, 0.0) - jnp.log(1.0 + jnp.exp(-jnp.abs(x)))


def _split3(x):
    hi = x.astype(BF16)
    r = x - hi.astype(F32)
    mid = r.astype(BF16)
    lo = (r - mid.astype(F32)).astype(BF16)
    return hi, mid, lo


def _dot(a, b):
    return jnp.dot(a, b, preferred_element_type=F32)


def _dot_nt(a, b):
    return lax.dot_general(a, b, (((1,), (1,)), ((), ())), preferred_element_type=F32)


def _dot_tn(a, b):
    return lax.dot_general(a, b, (((0,), (0,)), ((), ())), preferred_element_type=F32)


def _lower_bound_kernel(logit_ref, o_ref):
    z = logit_ref[...]
    n = z.shape[0]
    e = jnp.exp(z - jnp.max(z, axis=0, keepdims=True))
    p = e / jnp.sum(e, axis=0, keepdims=True)
    run = jnp.zeros_like(p[0:1])
    rows = [run]
    for k in range(1, n):
        run = run + p[k:k + 1]
        rows.append(run)
    o_ref[...] = jnp.clip(jnp.concatenate(rows, axis=0), 0.0, 1.0 - 1e-6)


def lower_bounds(logits):
    return pl.pallas_call(
        _lower_bound_kernel,
        out_shape=jax.ShapeDtypeStruct(logits.shape, F32),
        name="hgrn_lower_bounds",
    )(logits.astype(F32))


def _mod_kernel(c_ref, w_ref, b_ref, o_ref):
    s = _silu(c_ref[...]).astype(BF16)
    o_ref[0] = _dot(s, w_ref[0].astype(BF16)) + b_ref[0]


def modulation(c, mod_w, mod_b, tn=2048):
    n_layers, d, n = mod_w.shape
    b = c.shape[0]
    rows = 8
    c_pad = jnp.zeros((rows, d), F32).at[:b].set(c)
    out = pl.pallas_call(
        _mod_kernel,
        grid=(n_layers, n // tn),
        in_specs=[
            pl.BlockSpec((rows, d), lambda l, j: (0, 0)),
            pl.BlockSpec((1, d, tn), lambda l, j: (l, 0, j)),
            pl.BlockSpec((1, 1, tn), lambda l, j: (l, 0, j)),
        ],
        out_specs=pl.BlockSpec((1, rows, tn), lambda l, j: (l, 0, j)),
        out_shape=jax.ShapeDtypeStruct((n_layers, rows, n), F32),
        compiler_params=_params("parallel", "parallel"),
        name="adaln_modulation",
    )(c_pad, mod_w, mod_b.reshape(n_layers, 1, n))
    return out[:, :b]


def _norm_modulate(x, gain, shift, scale):
    var = jnp.mean(x * x, axis=-1, keepdims=True)
    y = x * lax.rsqrt(var + EPS) * gain
    return y * (1.0 + scale) + shift


def _norm_rows(x_ref, gain_ref, sh_ref, sc_ref, h_ref, copy_ref=None):
    def body(r, carry):
        rows = pl.ds(pl.multiple_of(r * NORM_ROWS, NORM_ROWS), NORM_ROWS)
        x = x_ref[rows, :]
        h_ref[rows, :] = _norm_modulate(x, gain_ref[...], sh_ref[0], sc_ref[0]).astype(h_ref.dtype)
        if copy_ref is not None:
            copy_ref[rows, :] = x
        return carry

    lax.fori_loop(0, x_ref.shape[0] // NORM_ROWS, body, 0)


def _norm_proj_kernel(x_ref, gain_ref, sh_ref, sc_ref, w_ref, waux_ref, o_ref, oaux_ref, h_ref):
    @pl.when(pl.program_id(1) == 0)
    def _():
        _norm_rows(x_ref, gain_ref, sh_ref, sc_ref, h_ref)
        oaux_ref[...] = _dot(h_ref[...], waux_ref[...])

    o_ref[...] = _dot(h_ref[...], w_ref[0]).astype(o_ref.dtype)


def norm_proj(x, gain, shift, scale, w_all, layer, n, w_aux, seq, tm=PROJ_TM, tn=PROJ_TN):
    m, d = x.shape
    tm = min(tm, seq)
    per_batch = seq // tm
    return pl.pallas_call(
        _norm_proj_kernel,
        grid=(m // tm, n // tn),
        in_specs=[
            pl.BlockSpec((tm, d), lambda i, j: (i, 0)),
            pl.BlockSpec((1, d), lambda i, j: (0, 0)),
            pl.BlockSpec((1, 1, d), lambda i, j: (i // per_batch, 0, 0)),
            pl.BlockSpec((1, 1, d), lambda i, j: (i // per_batch, 0, 0)),
            pl.BlockSpec((1, d, tn), lambda i, j: (layer, 0, j)),
            pl.BlockSpec((d, LANES), lambda i, j: (0, 0)),
        ],
        out_specs=[
            pl.BlockSpec((tm, tn), lambda i, j: (i, j)),
            pl.BlockSpec((tm, LANES), lambda i, j: (i, 0)),
        ],
        out_shape=[
            jax.ShapeDtypeStruct((m, n), BF16),
            jax.ShapeDtypeStruct((m, LANES), F32),
        ],
        scratch_shapes=[pltpu.VMEM((tm, d), BF16)],
        compiler_params=_params("parallel", "arbitrary"),
        name="norm_in_proj",
    )(x, gain.reshape(1, d), shift, scale, w_all, w_aux)


def _out_proj_kernel(*refs, n_lhs):
    lhs = refs[:n_lhs]
    w_ref, x_ref, gate_ref, o_ref = refs[n_lhs:]
    width = lhs[0].shape[1]
    y = _dot(lhs[0][...], w_ref[0, 0:width, :])
    for part, a_ref in enumerate(lhs[1:], 1):
        y = y + _dot(a_ref[...], w_ref[0, part * width:(part + 1) * width, :])
    o_ref[...] = x_ref[...] + gate_ref[0] * y


def out_proj_residual(lhs_list, w_all, layer, x, gate, seq, tm=OUT_TM):
    m, d = x.shape
    tm = min(tm, seq)
    per_batch = seq // tm
    n_lhs = len(lhs_list)
    width = lhs_list[0].shape[1]
    k = w_all.shape[1]
    assert all(a.shape[1] == width for a in lhs_list) and n_lhs * width == k
    in_specs = (
        [pl.BlockSpec((tm, width), lambda i: (i, 0)) for _ in lhs_list]
        + [pl.BlockSpec((1, k, d), lambda i: (layer, 0, 0), pipeline_mode=pl.Buffered(1)),
           pl.BlockSpec((tm, d), lambda i: (i, 0)),
           pl.BlockSpec((1, 1, d), lambda i: (i // per_batch, 0, 0))]
    )
    return pl.pallas_call(
        functools.partial(_out_proj_kernel, n_lhs=n_lhs),
        grid=(m // tm,),
        in_specs=in_specs,
        out_specs=pl.BlockSpec((tm, d), lambda i: (i, 0)),
        out_shape=jax.ShapeDtypeStruct((m, d), F32),
        compiler_params=_params("parallel"),
        name="out_proj_residual",
    )(*lhs_list, w_all, x, gate)


def _ffn_kernel(x_ref, gain_ref, sh_ref, sc_ref, gate_ref, wa_ref, wu_ref, wo_ref, o_ref, h_ref):
    @pl.when(pl.program_id(1) == 0)
    def _():
        _norm_rows(x_ref, gain_ref, sh_ref, sc_ref, h_ref, copy_ref=o_ref)

    h = h_ref[...]
    a = _dot(h, wa_ref[0])
    u = _dot(h, wu_ref[0])
    act = (_silu(a) * u).astype(BF16)
    d = o_ref.shape[1]
    for c0 in range(0, d, FFN_TN):
        cols = slice(c0, c0 + FFN_TN)
        o_ref[:, cols] += gate_ref[0][:, cols] * _dot(act, wo_ref[0, :, cols])


def ffn_residual(x, gain, shift, scale, gate, w_in, w_out, layer, seq, tm=FFN_TM, tf=FFN_TF):
    m, d = x.shape
    d_ff = w_out.shape[1]
    tm = min(tm, seq)
    per_batch = seq // tm
    nf = d_ff // tf
    vec = pl.BlockSpec((1, 1, d), lambda i, f: (i // per_batch, 0, 0))
    return pl.pallas_call(
        _ffn_kernel,
        grid=(m // tm, nf),
        in_specs=[
            pl.BlockSpec((tm, d), lambda i, f: (i, 0)),
            pl.BlockSpec((1, d), lambda i, f: (0, 0)),
            vec, vec, vec,
            pl.BlockSpec((1, d, tf), lambda i, f: (layer, 0, f)),
            pl.BlockSpec((1, d, tf), lambda i, f: (layer, 0, nf + f)),
            pl.BlockSpec((1, tf, d), lambda i, f: (layer, f, 0)),
        ],
        out_specs=pl.BlockSpec((tm, d), lambda i, f: (i, 0)),
        out_shape=jax.ShapeDtypeStruct((m, d), F32),
        scratch_shapes=[pltpu.VMEM((tm, d), BF16)],
        compiler_params=_params("parallel", "arbitrary"),
        name="swiglu_ffn_residual",
    )(x, gain.reshape(1, d), shift, scale, gate, w_in, w_in, w_out)


def _cumsum_rows(g, tril):
    n = g.shape[1]
    y = _dot(tril, jnp.concatenate(_split3(g), axis=1))
    return y[:, :n] + y[:, n:2 * n] + y[:, 2 * n:]


def _glr_heads(qs, ks, vs, cums, s_ref, sub):
    heads = range(len(qs))
    c, dk = qs[0].shape
    dv = vs[0].shape[1]
    vbs = [v.astype(BF16) for v in vs]
    lasts = [cum[c - 1:c, :] for cum in cums]

    inter = [_dot((qs[h] * jnp.exp(cums[h])).astype(BF16), s_ref[h].astype(BF16)) for h in heads]
    updates = [_dot_tn((ks[h] * jnp.exp(lasts[h] - cums[h])).astype(BF16), vbs[h]) for h in heads]
    for h in heads:
        decay_cols = jnp.transpose(jnp.broadcast_to(jnp.exp(lasts[h]), (LANES, dk)))
        s_ref[h] = jnp.concatenate([decay_cols] * (dv // LANES), axis=1) * s_ref[h] + updates[h]

    spans = [(r0, r0 + sub) for r0 in range(0, c, sub)]
    scores = []
    for h in heads:
        q, k, cum = qs[h], ks[h], cums[h]
        per_head = []
        for r0, r1 in spans:
            hi = cum[r1 - 1:r1, :]
            mid = 0.5 * hi if r0 == 0 else 0.5 * (cum[r0 - 1:r0, :] + hi)
            qi = (q[r0:r1] * jnp.exp(cum[r0:r1] - mid)).astype(BF16)
            ki = (k[:r1] * jnp.exp(mid - cum[:r1])).astype(BF16)
            a = _dot_nt(qi, ki)
            row = r0 + lax.broadcasted_iota(jnp.int32, (sub, r1), 0)
            col = lax.broadcasted_iota(jnp.int32, (sub, r1), 1)
            per_head.append(jnp.where(col <= row, a, 0.0).astype(BF16))
        scores.append(per_head)
    outs = []
    for h in heads:
        blocks = [_dot(a, vbs[h][:r1]) for a, (_, r1) in zip(scores[h], spans)]
        intra = blocks[0] if len(blocks) == 1 else jnp.concatenate(blocks, axis=0)
        outs.append(inter[h] + intra)
    return outs


def _tril_bf16(c):
    row = lax.broadcasted_iota(jnp.int32, (c, c), 0)
    col = lax.broadcasted_iota(jnp.int32, (c, c), 1)
    return jnp.where(col <= row, 1.0, 0.0).astype(BF16)


def _hgrn_kernel(q_ref, f_ref, i_ref, g_ref, lb_ref, gain_ref, o_ref, s_ref, *, chunk, sub):
    @pl.when(pl.program_id(1) == 0)
    def _():
        s_ref[...] = jnp.zeros_like(s_ref)

    heads, dk, _ = s_ref.shape
    tril = _tril_bf16(chunk)

    def body(ci, carry):
        rows = pl.ds(pl.multiple_of(ci * chunk, chunk), chunk)
        lb = lb_ref[...]
        sig = _sigmoid(f_ref[rows, :].astype(F32))
        forget = lb + (1.0 - lb) * sig
        cum = _cumsum_rows(jnp.log(jnp.maximum(forget, MIN_FORGET)), tril)
        key = (1.0 - lb) * (1.0 - sig)
        q = _silu(q_ref[rows, :].astype(F32))
        v = i_ref[rows, :]
        cols = [slice(hh * dk, (hh + 1) * dk) for hh in range(heads)]
        outs = _glr_heads([q[:, cs] for cs in cols], [key[:, cs] for cs in cols],
                          [v[:, cs] for cs in cols], [cum[:, cs] for cs in cols], s_ref, sub)
        for o, cs in zip(outs, cols):
            o = o * lax.rsqrt(jnp.mean(o * o, axis=-1, keepdims=True) + EPS) * gain_ref[:, cs]
            o_ref[rows, cs] = (o * _silu(g_ref[rows, cs].astype(F32))).astype(o_ref.dtype)
        return carry

    lax.fori_loop(0, q_ref.shape[0] // chunk, body, 0, unroll=2)


def hgrn2_heads(proj, lb, out_gain, batch, seq, tb=REC_TB, chunk=REC_CHUNK, sub=HGRN_SUB):
    m = proj.shape[0]
    tb = min(tb, seq)
    nt = seq // tb
    h, dk = A_HEADS, A_DK
    w = h * dk

    def col(group):
        return pl.BlockSpec((tb, w), lambda b, t: (b * nt + t, group))

    vec = pl.BlockSpec((1, w), lambda b, t: (0, 0))
    return pl.pallas_call(
        functools.partial(_hgrn_kernel, chunk=chunk, sub=sub),
        grid=(batch, nt),
        in_specs=[col(0), col(1), col(2), col(3), vec, vec],
        out_specs=pl.BlockSpec((tb, w), lambda b, t: (b * nt + t, 0)),
        out_shape=jax.ShapeDtypeStruct((m, w), BF16),
        scratch_shapes=[pltpu.VMEM((h, dk, dk), F32)],
        compiler_params=_params("parallel", "arbitrary"),
        name="hgrn2_recurrence",
    )(proj, proj, proj, proj, lb.reshape(1, w), out_gain.reshape(1, w))


def _gla_kernel(q_ref, k_ref, v_ref, g_ref, low_ref, wup_ref, bias_ref, gain_ref, o_ref, s_ref,
                *, chunk, sub):
    @pl.when(pl.program_id(1) == 0)
    def _():
        s_ref[...] = jnp.zeros_like(s_ref)

    heads, dk, dv = s_ref.shape
    tril = _tril_bf16(chunk)
    q_scale = dk ** -0.5

    def body(ci, carry):
        rows = pl.ds(pl.multiple_of(ci * chunk, chunk), chunk)
        z = _dot(low_ref[rows, :].astype(BF16), wup_ref[...]) + bias_ref[...]
        cum = _cumsum_rows(_log_sigmoid(z) * (1.0 / GLA_GATE_NORMALIZER), tril)
        kcs = [slice(hh * dk, (hh + 1) * dk) for hh in range(heads)]
        vcs = [slice(hh * dv, (hh + 1) * dv) for hh in range(heads)]
        outs = _glr_heads([q_ref[rows, kc].astype(F32) * q_scale for kc in kcs],
                          [k_ref[rows, kc].astype(F32) for kc in kcs],
                          [v_ref[rows, vc] for vc in vcs], [cum[:, kc] for kc in kcs], s_ref, sub)
        for o, vc in zip(outs, vcs):
            o = o * lax.rsqrt(jnp.mean(o * o, axis=-1, keepdims=True) + EPS) * gain_ref[...]
            o_ref[rows, vc] = (o * _silu(g_ref[rows, vc].astype(F32))).astype(o_ref.dtype)
        return carry

    lax.fori_loop(0, q_ref.shape[0] // chunk, body, 0)


def gla_heads(proj, low, w_up, b_gate, out_gain, batch, seq, tb=REC_TB, chunk=GLA_CHUNK,
              sub=GLA_SUB):
    m = proj.shape[0]
    tb = min(tb, seq)
    nt = seq // tb
    h, dk, dv = C_HEADS, C_DK, C_DV
    kw, vw = h * dk, h * dv
    v0 = 2 * kw // vw
    return pl.pallas_call(
        functools.partial(_gla_kernel, chunk=chunk, sub=sub),
        grid=(batch, nt),
        in_specs=[
            pl.BlockSpec((tb, kw), lambda b, t: (b * nt + t, 0)),
            pl.BlockSpec((tb, kw), lambda b, t: (b * nt + t, 1)),
            pl.BlockSpec((tb, vw), lambda b, t: (b * nt + t, v0)),
            pl.BlockSpec((tb, vw), lambda b, t: (b * nt + t, v0 + 1)),
            pl.BlockSpec((tb, LANES), lambda b, t: (b * nt + t, 0)),
            pl.BlockSpec((LANES, kw), lambda b, t: (0, 0)),
            pl.BlockSpec((1, kw), lambda b, t: (0, 0)),
            pl.BlockSpec((1, dv), lambda b, t: (0, 0)),
        ],
        out_specs=pl.BlockSpec((tb, vw), lambda b, t: (b * nt + t, 0)),
        out_shape=jax.ShapeDtypeStruct((m, vw), BF16),
        scratch_shapes=[pltpu.VMEM((h, dk, dv), F32)],
        compiler_params=_params("parallel", "arbitrary"),
        name="gla_recurrence",
    )(proj, proj, proj, proj, low, w_up, b_gate.reshape(1, kw), out_gain.reshape(1, dv))


LOG2E = 1.4426950408889634
F_PIECES = 3


def _fox_prep_kernel(q_ref, k_ref, logit_ref, bias_ref, qg_ref, kg_ref,
                     qo_ref, ko_ref, carry_ref, *, heads, head_dim):
    @pl.when(pl.program_id(1) == 0)
    def _():
        carry_ref[...] = jnp.zeros_like(carry_ref)

    tb = q_ref.shape[0]
    log_f = _log_sigmoid(logit_ref[...] + bias_ref[...])
    cum = _cumsum_rows(log_f, _tril_bf16(tb)) + carry_ref[0:1, :]
    carry_ref[...] = jnp.broadcast_to(cum[tb - 1:tb, :], carry_ref.shape)
    pieces = [p.astype(F32) for p in _split3(-LOG2E * cum)]
    lane = lax.broadcasted_iota(jnp.int32, (tb, head_dim), 1)
    one_cols = jnp.where(lane < F_PIECES, 1.0, 0.0).astype(qo_ref.dtype)

    qg = qg_ref[...] * (head_dim ** -0.5 * LOG2E)
    kg = kg_ref[...]
    for hh in range(heads):
        cols = slice(hh * head_dim, (hh + 1) * head_dim)
        lo = 2 * hh * head_dim
        main = slice(lo, lo + head_dim)
        extra = slice(lo + head_dim, lo + 2 * head_dim)
        q = q_ref[:, cols].astype(F32)
        qo_ref[:, main] = (q * lax.rsqrt(jnp.mean(q * q, axis=-1, keepdims=True) + EPS) * qg
                           ).astype(qo_ref.dtype)
        qo_ref[:, extra] = one_cols
        k = k_ref[:, cols].astype(F32)
        ko_ref[:, main] = (k * lax.rsqrt(jnp.mean(k * k, axis=-1, keepdims=True) + EPS) * kg
                           ).astype(ko_ref.dtype)
        bias_cols = jnp.zeros((tb, head_dim), F32)
        for j, piece in enumerate(pieces):
            bias_cols = jnp.where(lane == j, piece[:, hh:hh + 1], bias_cols)
        ko_ref[:, extra] = bias_cols.astype(ko_ref.dtype)


def fox_prep(proj, logits, f_bias, q_gain, k_gain, batch, seq, tb=PREP_TB):
    m = proj.shape[0]
    h, hd = B_HEADS, B_HEAD_DIM
    w = h * hd
    tb = min(tb, seq)
    nt = seq // tb
    base = (4 * A_HEADS * A_DK) // w
    bias = jnp.zeros((1, LANES), F32).at[0, :h].set(f_bias)

    def col(group):
        return pl.BlockSpec((tb, w), lambda b, t: (b * nt + t, base + group))

    wide_tok = pl.BlockSpec((tb, 2 * w), lambda b, t: (b * nt + t, 0))
    return pl.pallas_call(
        functools.partial(_fox_prep_kernel, heads=h, head_dim=hd),
        grid=(batch, nt),
        in_specs=[
            col(0), col(1),
            pl.BlockSpec((tb, LANES), lambda b, t: (b * nt + t, 0)),
            pl.BlockSpec((1, LANES), lambda b, t: (0, 0)),
            pl.BlockSpec((1, hd), lambda b, t: (0, 0)),
            pl.BlockSpec((1, hd), lambda b, t: (0, 0)),
        ],
        out_specs=[wide_tok, wide_tok],
        out_shape=[jax.ShapeDtypeStruct((m, 2 * w), BF16)] * 2,
        scratch_shapes=[pltpu.VMEM((8, LANES), F32)],
        compiler_params=_params("parallel", "arbitrary"),
        name="fox_prep",
    )(proj, proj, logits, bias, q_gain.reshape(1, hd), k_gain.reshape(1, hd))


def _fox_attn_kernel(q_ref, k_ref, v_ref, g_ref, o_ref, m_ref, acc_ref, *, tq, tk, heads, head_dim):
    qi = pl.program_id(2)
    q_start = pl.multiple_of(qi * tq, tq)
    wide = 2 * head_dim
    ones = jnp.ones((tk, head_dim), BF16)

    def tile(start, mask):
        rows = pl.ds(start, tk)
        scores = [_dot_nt(q_ref[:, hh * wide:(hh + 1) * wide], k_ref[rows, hh * wide:(hh + 1) * wide])
                  for hh in range(heads)]
        for hh, s in enumerate(scores):
            if mask is not None:
                s = jnp.where(mask, s, MASK_VALUE)
            m_prev = m_ref[hh]
            m_new = jnp.maximum(m_prev, jnp.max(s, axis=-1, keepdims=True))
            p = jnp.exp2(s - jnp.concatenate([m_new] * (tk // LANES), axis=1))
            alpha = jnp.exp2(m_prev - m_new)
            m_ref[hh] = m_new
            v_aug = jnp.concatenate([v_ref[rows, hh * head_dim:(hh + 1) * head_dim], ones], axis=1)
            acc_ref[hh] = (jnp.concatenate([alpha] * (wide // LANES), axis=1) * acc_ref[hh]
                           + _dot(p.astype(BF16), v_aug))

    m_ref[...] = jnp.full(m_ref.shape, MASK_VALUE, F32)
    acc_ref[...] = jnp.zeros_like(acc_ref)

    def body(ki, carry):
        tile(pl.multiple_of(ki * tk, tk), None)
        return carry

    lax.fori_loop(0, qi * (tq // tk), body, 0)
    for d in range(tq // tk):
        row = lax.broadcasted_iota(jnp.int32, (tq, tk), 0)
        col = lax.broadcasted_iota(jnp.int32, (tq, tk), 1) + d * tk
        tile(pl.multiple_of(q_start + d * tk, tk), col <= row)
    for hh in range(heads):
        acc = acc_ref[hh]
        cols = slice(hh * head_dim, (hh + 1) * head_dim)
        o_ref[:, cols] = (acc[:, :head_dim] / acc[:, head_dim:]
                          * _sigmoid(g_ref[:, cols].astype(F32))).astype(o_ref.dtype)


def fox_attention(q_aug, k_aug, proj, batch, seq, tq=ATT_TQ, tk=ATT_TK, group=ATT_HEADS):
    m = q_aug.shape[0]
    h, hd = B_HEADS, B_HEAD_DIM
    tq = min(tq, seq)
    tk = min(tk, tq)
    nq = seq // tq
    v_block = (4 * A_HEADS * A_DK + 2 * h * hd) // (group * hd)
    gate_block = v_block + h // group
    return pl.pallas_call(
        functools.partial(_fox_attn_kernel, tq=tq, tk=tk, heads=group, head_dim=hd),
        grid=(batch, h // group, nq),
        in_specs=[
            pl.BlockSpec((tq, 2 * group * hd), lambda b, hg, i: (b * nq + i, hg)),
            pl.BlockSpec((seq, 2 * group * hd), lambda b, hg, i: (b, hg),
                         pipeline_mode=pl.Buffered(1)),
            pl.BlockSpec((seq, group * hd), lambda b, hg, i: (b, v_block + hg),
                         pipeline_mode=pl.Buffered(1)),
            pl.BlockSpec((tq, group * hd), lambda b, hg, i: (b * nq + i, gate_block + hg)),
        ],
        out_specs=pl.BlockSpec((tq, group * hd), lambda b, hg, i: (b * nq + i, hg)),
        out_shape=jax.ShapeDtypeStruct((m, h * hd), BF16),
        scratch_shapes=[pltpu.VMEM((group, tq, LANES), F32), pltpu.VMEM((group, tq, 2 * hd), F32)],
        compiler_params=_params("parallel", "parallel", "arbitrary"),
        name="fox_attention",
    )(q_aug, k_aug, proj, proj)


def _pad_cols(w, n):
    return jnp.zeros((w.shape[0], n), w.dtype).at[:, :w.shape[1]].set(w)


def kernel(x, c, mod_w, mod_b, norm_mix_gain, norm_ffn_gain, ab_w_in, ab_w_out, hgrn_lb_logits,
           hgrn_out_gain, fox_q_gain, fox_k_gain, fox_f_bias, gla_w_in, gla_w_gate_up, gla_b_gate,
           gla_out_gain, gla_w_out, ffn_w_in, ffn_w_out):
    batch, seq, d = x.shape
    depth = mod_w.shape[0]
    ab_main = 4 * A_HEADS * A_DK + 4 * B_HEADS * B_HEAD_DIM
    c_main = 2 * C_HEADS * C_DK + 2 * C_HEADS * C_DV

    lbs = lower_bounds(hgrn_lb_logits)
    mod = modulation(c, mod_w, mod_b)
    xs = x.reshape(batch * seq, d)
    ab_w_in_b, ab_w_out_b = ab_w_in.astype(BF16), ab_w_out.astype(BF16)
    gla_w_in_b, gla_w_out_b = gla_w_in.astype(BF16), gla_w_out.astype(BF16)
    ffn_w_in_b, ffn_w_out_b = ffn_w_in.astype(BF16), ffn_w_out.astype(BF16)

    for layer in range(depth):
        sh1, sc1, g1, sh2, sc2, g2 = [
            mod[layer, :, i * d:(i + 1) * d].reshape(batch, 1, d) for i in range(6)]
        if layer % 2 == 0:
            i = layer // 2
            proj, logits = norm_proj(xs, norm_mix_gain[layer], sh1, sc1, ab_w_in_b, i, ab_main,
                                     _pad_cols(ab_w_in[i, :, ab_main:], LANES).astype(BF16), seq)
            o_a = hgrn2_heads(proj, lbs[i], hgrn_out_gain[i], batch, seq)
            q_aug, k_aug = fox_prep(proj, logits, fox_f_bias[i], fox_q_gain[i], fox_k_gain[i],
                                    batch, seq)
            o_b = fox_attention(q_aug, k_aug, proj, batch, seq)
            xs = out_proj_residual([o_a, o_b], ab_w_out_b, i, xs, g1, seq)
        else:
            j = layer // 2
            proj, low = norm_proj(xs, norm_mix_gain[layer], sh1, sc1, gla_w_in_b, j, c_main,
                                  _pad_cols(gla_w_in[j, :, c_main:], LANES).astype(BF16), seq)
            w_up = jnp.zeros((LANES, gla_w_gate_up.shape[2]), BF16).at[:C_GATE_RANK].set(
                gla_w_gate_up[j].astype(BF16))
            o_c = gla_heads(proj, low, w_up, gla_b_gate[j], gla_out_gain[j], batch, seq)
            xs = out_proj_residual([o_c], gla_w_out_b, j, xs, g1, seq)
        xs = ffn_residual(xs, norm_ffn_gain[layer], sh2, sc2, g2, ffn_w_in_b, ffn_w_out_b, layer, seq)
    return xs.reshape(batch, seq, d)
```

```python
import functools

import jax
import jax.numpy as jnp
from jax import lax
from jax.experimental import pallas as pl
from jax.experimental.pallas import tpu as pltpu

F32 = jnp.float32
BF16 = jnp.bfloat16

EPS = 1e-6
MIN_FORGET = 1e-30
MASK_VALUE = -1e30
GLA_GATE_NORMALIZER = 16.0

LANES = 128
VMEM_LIMIT = 60 * 1024 * 1024

A_HEADS = 8
A_DK = 128
B_HEADS = 8
B_HEAD_DIM = 128
C_HEADS = 4
C_DK = 256
C_DV = 512
C_GATE_RANK = 16

PROJ_TM = 1024
PROJ_TN = 1024
OUT_TM = 512
FFN_TM = 1024
FFN_TF = 512
FFN_TN = 512
NORM_ROWS = 128
REC_TB = 512
REC_CHUNK = 64
HGRN_SUB = 16
GLA_CHUNK = 256
GLA_SUB = 64
ATT_TQ = 512
ATT_TK = 512
ATT_HEADS = 4
PREP_TB = 512


def _params(*sem):
    return pltpu.CompilerParams(dimension_semantics=sem, vmem_limit_bytes=VMEM_LIMIT)


def _sigmoid(x):
    return 1.0 / (1.0 + jnp.exp(-x))


def _silu(x):
    return x * _sigmoid(x)


def _log_sigmoid(x):
    return jnp.minimum(x, 0.0) - jnp.log(1.0 + jnp.exp(-jnp.abs(x)))


def _split3(x):
    hi = x.astype(BF16)
    r = x - hi.astype(F32)
    mid = r.astype(BF16)
    lo = (r - mid.astype(F32)).astype(BF16)
    return hi, mid, lo


def _dot(a, b):
    return jnp.dot(a, b, preferred_element_type=F32)


def _dot_nt(a, b):
    return lax.dot_general(a, b, (((1,), (1,)), ((), ())), preferred_element_type=F32)


def _dot_tn(a, b):
    return lax.dot_general(a, b, (((0,), (0,)), ((), ())), preferred_element_type=F32)


def _lower_bound_kernel(logit_ref, o_ref):
    z = logit_ref[...]
    n = z.shape[0]
    e = jnp.exp(z - jnp.max(z, axis=0, keepdims=True))
    p = e / jnp.sum(e, axis=0, keepdims=True)
    run = jnp.zeros_like(p[0:1])
    rows = [run]
    for k in range(1, n):
        run = run + p[k:k + 1]
        rows.append(run)
    o_ref[...] = jnp.clip(jnp.concatenate(rows, axis=0), 0.0, 1.0 - 1e-6)


def lower_bounds(logits):
    return pl.pallas_call(
        _lower_bound_kernel,
        out_shape=jax.ShapeDtypeStruct(logits.shape, F32),
        name="hgrn_lower_bounds",
    )(logits.astype(F32))


def _mod_kernel(c_ref, w_ref, b_ref, o_ref):
    s = _silu(c_ref[...]).astype(BF16)
    o_ref[0] = _dot(s, w_ref[0].astype(BF16)) + b_ref[0]


def modulation(c, mod_w, mod_b, tn=2048):
    n_layers, d, n = mod_w.shape
    b = c.shape[0]
    rows = 8
    c_pad = jnp.zeros((rows, d), F32).at[:b].set(c)
    out = pl.pallas_call(
        _mod_kernel,
        grid=(n_layers, n // tn),
        in_specs=[
            pl.BlockSpec((rows, d), lambda l, j: (0, 0)),
            pl.BlockSpec((1, d, tn), lambda l, j: (l, 0, j)),
            pl.BlockSpec((1, 1, tn), lambda l, j: (l, 0, j)),
        ],
        out_specs=pl.BlockSpec((1, rows, tn), lambda l, j: (l, 0, j)),
        out_shape=jax.ShapeDtypeStruct((n_layers, rows, n), F32),
        compiler_params=_params("parallel", "parallel"),
        name="adaln_modulation",
    )(c_pad, mod_w, mod_b.reshape(n_layers, 1, n))
    return out[:, :b]


def _norm_modulate(x, gain, shift, scale):
    var = jnp.mean(x * x, axis=-1, keepdims=True)
    y = x * lax.rsqrt(var + EPS) * gain
    return y * (1.0 + scale) + shift


def _norm_rows(x_ref, gain_ref, sh_ref, sc_ref, h_ref, copy_ref=None):
    def body(r, carry):
        rows = pl.ds(pl.multiple_of(r * NORM_ROWS, NORM_ROWS), NORM_ROWS)
        x = x_ref[rows, :]
        h_ref[rows, :] = _norm_modulate(x, gain_ref[...], sh_ref[0], sc_ref[0]).astype(h_ref.dtype)
        if copy_ref is not None:
            copy_ref[rows, :] = x
        return carry

    lax.fori_loop(0, x_ref.shape[0] // NORM_ROWS, body, 0)


def _norm_proj_kernel(x_ref, gain_ref, sh_ref, sc_ref, w_ref, waux_ref, o_ref, oaux_ref, h_ref):
    @pl.when(pl.program_id(1) == 0)
    def _():
        _norm_rows(x_ref, gain_ref, sh_ref, sc_ref, h_ref)
        oaux_ref[...] = _dot(h_ref[...], waux_ref[...])

    o_ref[...] = _dot(h_ref[...], w_ref[0]).astype(o_ref.dtype)


def norm_proj(x, gain, shift, scale, w_all, layer, n, w_aux, seq, tm=PROJ_TM, tn=PROJ_TN):
    m, d = x.shape
    tm = min(tm, seq)
    per_batch = seq // tm
    return pl.pallas_call(
        _norm_proj_kernel,
        grid=(m // tm, n // tn),
        in_specs=[
            pl.BlockSpec((tm, d), lambda i, j: (i, 0)),
            pl.BlockSpec((1, d), lambda i, j: (0, 0)),
            pl.BlockSpec((1, 1, d), lambda i, j: (i // per_batch, 0, 0)),
            pl.BlockSpec((1, 1, d), lambda i, j: (i // per_batch, 0, 0)),
            pl.BlockSpec((1, d, tn), lambda i, j: (layer, 0, j)),
            pl.BlockSpec((d, LANES), lambda i, j: (0, 0)),
        ],
        out_specs=[
            pl.BlockSpec((tm, tn), lambda i, j: (i, j)),
            pl.BlockSpec((tm, LANES), lambda i, j: (i, 0)),
        ],
        out_shape=[
            jax.ShapeDtypeStruct((m, n), BF16),
            jax.ShapeDtypeStruct((m, LANES), F32),
        ],
        scratch_shapes=[pltpu.VMEM((tm, d), BF16)],
        compiler_params=_params("parallel", "arbitrary"),
        name="norm_in_proj",
    )(x, gain.reshape(1, d), shift, scale, w_all, w_aux)


def _out_proj_kernel(*refs, n_lhs):
    lhs = refs[:n_lhs]
    w_ref, x_ref, gate_ref, o_ref = refs[n_lhs:]
    width = lhs[0].shape[1]
    y = _dot(lhs[0][...], w_ref[0, 0:width, :])
    for part, a_ref in enumerate(lhs[1:], 1):
        y = y + _dot(a_ref[...], w_ref[0, part * width:(part + 1) * width, :])
    o_ref[...] = x_ref[...] + gate_ref[0] * y


def out_proj_residual(lhs_list, w_all, layer, x, gate, seq, tm=OUT_TM):
    m, d = x.shape
    tm = min(tm, seq)
    per_batch = seq // tm
    n_lhs = len(lhs_list)
    width = lhs_list[0].shape[1]
    k = w_all.shape[1]
    assert all(a.shape[1] == width for a in lhs_list) and n_lhs * width == k
    in_specs = (
        [pl.BlockSpec((tm, width), lambda i: (i, 0)) for _ in lhs_list]
        + [pl.BlockSpec((1, k, d), lambda i: (layer, 0, 0), pipeline_mode=pl.Buffered(1)),
           pl.BlockSpec((tm, d), lambda i: (i, 0)),
           pl.BlockSpec((1, 1, d), lambda i: (i // per_batch, 0, 0))]
    )
    return pl.pallas_call(
        functools.partial(_out_proj_kernel, n_lhs=n_lhs),
        grid=(m // tm,),
        in_specs=in_specs,
        out_specs=pl.BlockSpec((tm, d), lambda i: (i, 0)),
        out_shape=jax.ShapeDtypeStruct((m, d), F32),
        compiler_params=_params("parallel"),
        name="out_proj_residual",
    )(*lhs_list, w_all, x, gate)


def _ffn_kernel(x_ref, gain_ref, sh_ref, sc_ref, gate_ref, wa_ref, wu_ref, wo_ref, o_ref, h_ref):
    @pl.when(pl.program_id(1) == 0)
    def _():
        _norm_rows(x_ref, gain_ref, sh_ref, sc_ref, h_ref, copy_ref=o_ref)

    h = h_ref[...]
    a = _dot(h, wa_ref[0])
    u = _dot(h, wu_ref[0])
    act = (_silu(a) * u).astype(BF16)
    d = o_ref.shape[1]
    for c0 in range(0, d, FFN_TN):
        cols = slice(c0, c0 + FFN_TN)
        o_ref[:, cols] += gate_ref[0][:, cols] * _dot(act, wo_ref[0, :, cols])


def ffn_residual(x, gain, shift, scale, gate, w_in, w_out, layer, seq, tm=FFN_TM, tf=FFN_TF):
    m, d = x.shape
    d_ff = w_out.shape[1]
    tm = min(tm, seq)
    per_batch = seq // tm
    nf = d_ff // tf
    vec = pl.BlockSpec((1, 1, d), lambda i, f: (i // per_batch, 0, 0))
    return pl.pallas_call(
        _ffn_kernel,
        grid=(m // tm, nf),
        in_specs=[
            pl.BlockSpec((tm, d), lambda i, f: (i, 0)),
            pl.BlockSpec((1, d), lambda i, f: (0, 0)),
            vec, vec, vec,
            pl.BlockSpec((1, d, tf), lambda i, f: (layer, 0, f)),
            pl.BlockSpec((1, d, tf), lambda i, f: (layer, 0, nf + f)),
            pl.BlockSpec((1, tf, d), lambda i, f: (layer, f, 0)),
        ],
        out_specs=pl.BlockSpec((tm, d), lambda i, f: (i, 0)),
        out_shape=jax.ShapeDtypeStruct((m, d), F32),
        scratch_shapes=[pltpu.VMEM((tm, d), BF16)],
        compiler_params=_params("parallel", "arbitrary"),
        name="swiglu_ffn_residual",
    )(x, gain.reshape(1, d), shift, scale, gate, w_in, w_in, w_out)


def _cumsum_rows(g, tril):
    n = g.shape[1]
    y = _dot(tril, jnp.concatenate(_split3(g), axis=1))
    return y[:, :n] + y[:, n:2 * n] + y[:, 2 * n:]


def _glr_heads(qs, ks, vs, cums, s_ref, sub):
    heads = range(len(qs))
    c, dk = qs[0].shape
    dv = vs[0].shape[1]
    vbs = [v.astype(BF16) for v in vs]
    lasts = [cum[c - 1:c, :] for cum in cums]

    inter = [_dot((qs[h] * jnp.exp(cums[h])).astype(BF16), s_ref[h].astype(BF16)) for h in heads]
    updates = [_dot_tn((ks[h] * jnp.exp(lasts[h] - cums[h])).astype(BF16), vbs[h]) for h in heads]
    for h in heads:
        decay_cols = jnp.transpose(jnp.broadcast_to(jnp.exp(lasts[h]), (LANES, dk)))
        s_ref[h] = jnp.concatenate([decay_cols] * (dv // LANES), axis=1) * s_ref[h] + updates[h]

    spans = [(r0, r0 + sub) for r0 in range(0, c, sub)]
    scores = []
    for h in heads:
        q, k, cum = qs[h], ks[h], cums[h]
        per_head = []
        for r0, r1 in spans:
            hi = cum[r1 - 1:r1, :]
            mid = 0.5 * hi if r0 == 0 else 0.5 * (cum[r0 - 1:r0, :] + hi)
            qi = (q[r0:r1] * jnp.exp(cum[r0:r1] - mid)).astype(BF16)
            ki = (k[:r1] * jnp.exp(mid - cum[:r1])).astype(BF16)
            a = _dot_nt(qi, ki)
            row = r0 + lax.broadcasted_iota(jnp.int32, (sub, r1), 0)
            col = lax.broadcasted_iota(jnp.int32, (sub, r1), 1)
            per_head.append(jnp.where(col <= row, a, 0.0).astype(BF16))
        scores.append(per_head)
    outs = []
    for h in heads:
        blocks = [_dot(a, vbs[h][:r1]) for a, (_, r1) in zip(scores[h], spans)]
        intra = blocks[0] if len(blocks) == 1 else jnp.concatenate(blocks, axis=0)
        outs.append(inter[h] + intra)
    return outs


def _tril_bf16(c):
    row = lax.broadcasted_iota(jnp.int32, (c, c), 0)
    col = lax.broadcasted_iota(jnp.int32, (c, c), 1)
    return jnp.where(col <= row, 1.0, 0.0).astype(BF16)


def _hgrn_kernel(q_ref, f_ref, i_ref, g_ref, lb_ref, gain_ref, o_ref, s_ref, *, chunk, sub):
    @pl.when(pl.program_id(1) == 0)
    def _():
        s_ref[...] = jnp.zeros_like(s_ref)

    heads, dk, _ = s_ref.shape
    tril = _tril_bf16(chunk)

    def body(ci, carry):
        rows = pl.ds(pl.multiple_of(ci * chunk, chunk), chunk)
        lb = lb_ref[...]
        sig = _sigmoid(f_ref[rows, :].astype(F32))
        forget = lb + (1.0 - lb) * sig
        cum = _cumsum_rows(jnp.log(jnp.maximum(forget, MIN_FORGET)), tril)
        key = (1.0 - lb) * (1.0 - sig)
        q = _silu(q_ref[rows, :].astype(F32))
        v = i_ref[rows, :]
        cols = [slice(hh * dk, (hh + 1) * dk) for hh in range(heads)]
        outs = _glr_heads([q[:, cs] for cs in cols], [key[:, cs] for cs in cols],
                          [v[:, cs] for cs in cols], [cum[:, cs] for cs in cols], s_ref, sub)
        for o, cs in zip(outs, cols):
            o = o * lax.rsqrt(jnp.mean(o * o, axis=-1, keepdims=True) + EPS) * gain_ref[:, cs]
            o_ref[rows, cs] = (o * _silu(g_ref[rows, cs].astype(F32))).astype(o_ref.dtype)
        return carry

    lax.fori_loop(0, q_ref.shape[0] // chunk, body, 0, unroll=2)


def hgrn2_heads(proj, lb, out_gain, batch, seq, tb=REC_TB, chunk=REC_CHUNK, sub=HGRN_SUB):
    m = proj.shape[0]
    tb = min(tb, seq)
    nt = seq // tb
    h, dk = A_HEADS, A_DK
    w = h * dk

    def col(group):
        return pl.BlockSpec((tb, w), lambda b, t: (b * nt + t, group))

    vec = pl.BlockSpec((1, w), lambda b, t: (0, 0))
    return pl.pallas_call(
        functools.partial(_hgrn_kernel, chunk=chunk, sub=sub),
        grid=(batch, nt),
        in_specs=[col(0), col(1), col(2), col(3), vec, vec],
        out_specs=pl.BlockSpec((tb, w), lambda b, t: (b * nt + t, 0)),
        out_shape=jax.ShapeDtypeStruct((m, w), BF16),
        scratch_shapes=[pltpu.VMEM((h, dk, dk), F32)],
        compiler_params=_params("parallel", "arbitrary"),
        name="hgrn2_recurrence",
    )(proj, proj, proj, proj, lb.reshape(1, w), out_gain.reshape(1, w))


def _gla_kernel(q_ref, k_ref, v_ref, g_ref, low_ref, wup_ref, bias_ref, gain_ref, o_ref, s_ref,
                *, chunk, sub):
    @pl.when(pl.program_id(1) == 0)
    def _():
        s_ref[...] = jnp.zeros_like(s_ref)

    heads, dk, dv = s_ref.shape
    tril = _tril_bf16(chunk)
    q_scale = dk ** -0.5

    def body(ci, carry):
        rows = pl.ds(pl.multiple_of(ci * chunk, chunk), chunk)
        z = _dot(low_ref[rows, :].astype(BF16), wup_ref[...]) + bias_ref[...]
        cum = _cumsum_rows(_log_sigmoid(z) * (1.0 / GLA_GATE_NORMALIZER), tril)
        kcs = [slice(hh * dk, (hh + 1) * dk) for hh in range(heads)]
        vcs = [slice(hh * dv, (hh + 1) * dv) for hh in range(heads)]
        outs = _glr_heads([q_ref[rows, kc].astype(F32) * q_scale for kc in kcs],
                          [k_ref[rows, kc].astype(F32) for kc in kcs],
                          [v_ref[rows, vc] for vc in vcs], [cum[:, kc] for kc in kcs], s_ref, sub)
        for o, vc in zip(outs, vcs):
            o = o * lax.rsqrt(jnp.mean(o * o, axis=-1, keepdims=True) + EPS) * gain_ref[...]
            o_ref[rows, vc] = (o * _silu(g_ref[rows, vc].astype(F32))).astype(o_ref.dtype)
        return carry

    lax.fori_loop(0, q_ref.shape[0] // chunk, body, 0)


def gla_heads(proj, low, w_up, b_gate, out_gain, batch, seq, tb=REC_TB, chunk=GLA_CHUNK,
              sub=GLA_SUB):
    m = proj.shape[0]
    tb = min(tb, seq)
    nt = seq // tb
    h, dk, dv = C_HEADS, C_DK, C_DV
    kw, vw = h * dk, h * dv
    v0 = 2 * kw // vw
    return pl.pallas_call(
        functools.partial(_gla_kernel, chunk=chunk, sub=sub),
        grid=(batch, nt),
        in_specs=[
            pl.BlockSpec((tb, kw), lambda b, t: (b * nt + t, 0)),
            pl.BlockSpec((tb, kw), lambda b, t: (b * nt + t, 1)),
            pl.BlockSpec((tb, vw), lambda b, t: (b * nt + t, v0)),
            pl.BlockSpec((tb, vw), lambda b, t: (b * nt + t, v0 + 1)),
            pl.BlockSpec((tb, LANES), lambda b, t: (b * nt + t, 0)),
            pl.BlockSpec((LANES, kw), lambda b, t: (0, 0)),
            pl.BlockSpec((1, kw), lambda b, t: (0, 0)),
            pl.BlockSpec((1, dv), lambda b, t: (0, 0)),
        ],
        out_specs=pl.BlockSpec((tb, vw), lambda b, t: (b * nt + t, 0)),
        out_shape=jax.ShapeDtypeStruct((m, vw), BF16),
        scratch_shapes=[pltpu.VMEM((h, dk, dv), F32)],
        compiler_params=_params("parallel", "arbitrary"),
        name="gla_recurrence",
    )(proj, proj, proj, proj, low, w_up, b_gate.reshape(1, kw), out_gain.reshape(1, dv))


LOG2E = 1.4426950408889634
F_PIECES = 3


def _fox_prep_kernel(q_ref, k_ref, logit_ref, bias_ref, qg_ref, kg_ref,
                     qo_ref, ko_ref, carry_ref, *, heads, head_dim):
    @pl.when(pl.program_id(1) == 0)
    def _():
        carry_ref[...] = jnp.zeros_like(carry_ref)

    tb = q_ref.shape[0]
    log_f = _log_sigmoid(logit_ref[...] + bias_ref[...])
    cum = _cumsum_rows(log_f, _tril_bf16(tb)) + carry_ref[0:1, :]
    carry_ref[...] = jnp.broadcast_to(cum[tb - 1:tb, :], carry_ref.shape)
    pieces = [p.astype(F32) for p in _split3(-LOG2E * cum)]
    lane = lax.broadcasted_iota(jnp.int32, (tb, head_dim), 1)
    one_cols = jnp.where(lane < F_PIECES, 1.0, 0.0).astype(qo_ref.dtype)

    qg = qg_ref[...] * (head_dim ** -0.5 * LOG2E)
    kg = kg_ref[...]
    for hh in range(heads):
        cols = slice(hh * head_dim, (hh + 1) * head_dim)
        lo = 2 * hh * head_dim
        main = slice(lo, lo + head_dim)
        extra = slice(lo + head_dim, lo + 2 * head_dim)
        q = q_ref[:, cols].astype(F32)
        qo_ref[:, main] = (q * lax.rsqrt(jnp.mean(q * q, axis=-1, keepdims=True) + EPS) * qg
                           ).astype(qo_ref.dtype)
        qo_ref[:, extra] = one_cols
        k = k_ref[:, cols].astype(F32)
        ko_ref[:, main] = (k * lax.rsqrt(jnp.mean(k * k, axis=-1, keepdims=True) + EPS) * kg
                           ).astype(ko_ref.dtype)
        bias_cols = jnp.zeros((tb, head_dim), F32)
        for j, piece in enumerate(pieces):
            bias_cols = jnp.where(lane == j, piece[:, hh:hh + 1], bias_cols)
        ko_ref[:, extra] = bias_cols.astype(ko_ref.dtype)


def fox_prep(proj, logits, f_bias, q_gain, k_gain, batch, seq, tb=PREP_TB):
    m = proj.shape[0]
    h, hd = B_HEADS, B_HEAD_DIM
    w = h * hd
    tb = min(tb, seq)
    nt = seq // tb
    base = (4 * A_HEADS * A_DK) // w
    bias = jnp.zeros((1, LANES), F32).at[0, :h].set(f_bias)

    def col(group):
        return pl.BlockSpec((tb, w), lambda b, t: (b * nt + t, base + group))

    wide_tok = pl.BlockSpec((tb, 2 * w), lambda b, t: (b * nt + t, 0))
    return pl.pallas_call(
        functools.partial(_fox_prep_kernel, heads=h, head_dim=hd),
        grid=(batch, nt),
        in_specs=[
            col(0), col(1),
            pl.BlockSpec((tb, LANES), lambda b, t: (b * nt + t, 0)),
            pl.BlockSpec((1, LANES), lambda b, t: (0, 0)),
            pl.BlockSpec((1, hd), lambda b, t: (0, 0)),
            pl.BlockSpec((1, hd), lambda b, t: (0, 0)),
        ],
        out_specs=[wide_tok, wide_tok],
        out_shape=[jax.ShapeDtypeStruct((m, 2 * w), BF16)] * 2,
        scratch_shapes=[pltpu.VMEM((8, LANES), F32)],
        compiler_params=_params("parallel", "arbitrary"),
        name="fox_prep",
    )(proj, proj, logits, bias, q_gain.reshape(1, hd), k_gain.reshape(1, hd))


def _fox_attn_kernel(q_ref, k_ref, v_ref, g_ref, o_ref, m_ref, acc_ref, *, tq, tk, heads, head_dim):
    qi = pl.program_id(2)
    q_start = pl.multiple_of(qi * tq, tq)
    wide = 2 * head_dim
    ones = jnp.ones((tk, head_dim), BF16)

    def tile(start, mask):
        rows = pl.ds(start, tk)
        scores = [_dot_nt(q_ref[:, hh * wide:(hh + 1) * wide], k_ref[rows, hh * wide:(hh + 1) * wide])
                  for hh in range(heads)]
        for hh, s in enumerate(scores):
            if mask is not None:
                s = jnp.where(mask, s, MASK_VALUE)
            m_prev = m_ref[hh]
            m_new = jnp.maximum(m_prev, jnp.max(s, axis=-1, keepdims=True))
            p = jnp.exp2(s - jnp.concatenate([m_new] * (tk // LANES), axis=1))
            alpha = jnp.exp2(m_prev - m_new)
            m_ref[hh] = m_new
            v_aug = jnp.concatenate([v_ref[rows, hh * head_dim:(hh + 1) * head_dim], ones], axis=1)
            acc_ref[hh] = (jnp.concatenate([alpha] * (wide // LANES), axis=1) * acc_ref[hh]
                           + _dot(p.astype(BF16), v_aug))

    m_ref[...] = jnp.full(m_ref.shape, MASK_VALUE, F32)
    acc_ref[...] = jnp.zeros_like(acc_ref)

    n_full = qi * (tq // tk)

    def pair(pi, carry):
        tile(pl.multiple_of(2 * pi * tk, tk), None)
        tile(pl.multiple_of((2 * pi + 1) * tk, tk), None)
        return carry

    lax.fori_loop(0, n_full // 2, pair, 0)

    @pl.when(n_full % 2 == 1)
    def _():
        tile(pl.multiple_of((n_full - 1) * tk, tk), None)

    for d in range(tq // tk):
        row = lax.broadcasted_iota(jnp.int32, (tq, tk), 0)
        col = lax.broadcasted_iota(jnp.int32, (tq, tk), 1) + d * tk
        tile(pl.multiple_of(q_start + d * tk, tk), col <= row)
    for hh in range(heads):
        acc = acc_ref[hh]
        cols = slice(hh * head_dim, (hh + 1) * head_dim)
        o_ref[:, cols] = (acc[:, :head_dim] / acc[:, head_dim:]
                          * _sigmoid(g_ref[:, cols].astype(F32))).astype(o_ref.dtype)


def fox_attention(q_aug, k_aug, proj, batch, seq, tq=ATT_TQ, tk=ATT_TK, group=ATT_HEADS):
    m = q_aug.shape[0]
    h, hd = B_HEADS, B_HEAD_DIM
    tq = min(tq, seq)
    tk = min(tk, tq)
    nq = seq // tq
    v_block = (4 * A_HEADS * A_DK + 2 * h * hd) // (group * hd)
    gate_block = v_block + h // group
    return pl.pallas_call(
        functools.partial(_fox_attn_kernel, tq=tq, tk=tk, heads=group, head_dim=hd),
        grid=(batch, h // group, nq),
        in_specs=[
            pl.BlockSpec((tq, 2 * group * hd), lambda b, hg, i: (b * nq + i, hg)),
            pl.BlockSpec((seq, 2 * group * hd), lambda b, hg, i: (b, hg),
                         pipeline_mode=pl.Buffered(1)),
            pl.BlockSpec((seq, group * hd), lambda b, hg, i: (b, v_block + hg),
                         pipeline_mode=pl.Buffered(1)),
            pl.BlockSpec((tq, group * hd), lambda b, hg, i: (b * nq + i, gate_block + hg)),
        ],
        out_specs=pl.BlockSpec((tq, group * hd), lambda b, hg, i: (b * nq + i, hg)),
        out_shape=jax.ShapeDtypeStruct((m, h * hd), BF16),
        scratch_shapes=[pltpu.VMEM((group, tq, LANES), F32), pltpu.VMEM((group, tq, 2 * hd), F32)],
        compiler_params=_params("parallel", "parallel", "arbitrary"),
        name="fox_attention",
    )(q_aug, k_aug, proj, proj)


def _pad_cols(w, n):
    return jnp.zeros((w.shape[0], n), w.dtype).at[:, :w.shape[1]].set(w)


def kernel(x, c, mod_w, mod_b, norm_mix_gain, norm_ffn_gain, ab_w_in, ab_w_out, hgrn_lb_logits,
           hgrn_out_gain, fox_q_gain, fox_k_gain, fox_f_bias, gla_w_in, gla_w_gate_up, gla_b_gate,
           gla_out_gain, gla_w_out, ffn_w_in, ffn_w_out):
    batch, seq, d = x.shape
    depth = mod_w.shape[0]
    ab_main = 4 * A_HEADS * A_DK + 4 * B_HEADS * B_HEAD_DIM
    c_main = 2 * C_HEADS * C_DK + 2 * C_HEADS * C_DV

    lbs = lower_bounds(hgrn_lb_logits)
    mod = modulation(c, mod_w, mod_b)
    xs = x.reshape(batch * seq, d)
    ab_w_in_b, ab_w_out_b = ab_w_in.astype(BF16), ab_w_out.astype(BF16)
    gla_w_in_b, gla_w_out_b = gla_w_in.astype(BF16), gla_w_out.astype(BF16)
    ffn_w_in_b, ffn_w_out_b = ffn_w_in.astype(BF16), ffn_w_out.astype(BF16)

    for layer in range(depth):
        sh1, sc1, g1, sh2, sc2, g2 = [
            mod[layer, :, i * d:(i + 1) * d].reshape(batch, 1, d) for i in range(6)]
        if layer % 2 == 0:
            i = layer // 2
            proj, logits = norm_proj(xs, norm_mix_gain[layer], sh1, sc1, ab_w_in_b, i, ab_main,
                                     _pad_cols(ab_w_in[i, :, ab_main:], LANES).astype(BF16), seq)
            o_a = hgrn2_heads(proj, lbs[i], hgrn_out_gain[i], batch, seq)
            q_aug, k_aug = fox_prep(proj, logits, fox_f_bias[i], fox_q_gain[i], fox_k_gain[i],
                                    batch, seq)
            o_b = fox_attention(q_aug, k_aug, proj, batch, seq)
            xs = out_proj_residual([o_a, o_b], ab_w_out_b, i, xs, g1, seq)
        else:
            j = layer // 2
            proj, low = norm_proj(xs, norm_mix_gain[layer], sh1, sc1, gla_w_in_b, j, c_main,
                                  _pad_cols(gla_w_in[j, :, c_main:], LANES).astype(BF16), seq)
            w_up = jnp.zeros((LANES, gla_w_gate_up.shape[2]), BF16).at[:C_GATE_RANK].set(
                gla_w_gate_up[j].astype(BF16))
            o_c = gla_heads(proj, low, w_up, gla_b_gate[j], gla_out_gain[j], batch, seq)
            xs = out_proj_residual([o_c], gla_w_out_b, j, xs, g1, seq)
        xs = ffn_residual(xs, norm_ffn_gain[layer], sh2, sc2, g2, ffn_w_in_b, ffn_w_out_b, layer, seq)
    return xs.reshape(batch, seq, d)
```

```python
import functools

import jax
import jax.numpy as jnp
from jax import lax
from jax.experimental import pallas as pl
from jax.experimental.pallas import tpu as pltpu

F32 = jnp.float32
BF16 = jnp.bfloat16

EPS = 1e-6
MIN_FORGET = 1e-30
MASK_VALUE = -1e30
GLA_GATE_NORMALIZER = 16.0

LANES = 128
VMEM_LIMIT = 60 * 1024 * 1024

A_HEADS = 8
A_DK = 128
B_HEADS = 8
B_HEAD_DIM = 128
C_HEADS = 4
C_DK = 256
C_DV = 512
C_GATE_RANK = 16

PROJ_TM = 1024
PROJ_TN = 1024
OUT_TM = 512
FFN_TM = 1024
FFN_TF = 512
FFN_TN = 512
NORM_ROWS = 128
REC_TB = 512
REC_CHUNK = 64
HGRN_SUB = 16
GLA_CHUNK = 256
GLA_SUB = 64
ATT_TQ = 512
ATT_TK = 512
ATT_HEADS = 4
PREP_TB = 512


def _params(*sem):
    return pltpu.CompilerParams(dimension_semantics=sem, vmem_limit_bytes=VMEM_LIMIT)


def _sigmoid(x):
    return 1.0 / (1.0 + jnp.exp(-x))


def _silu(x):
    return x * _sigmoid(x)


def _log_sigmoid(x):
    return jnp.minimum(x, 0.0) - jnp.log(1.0 + jnp.exp(-jnp.abs(x)))


def _split3(x):
    hi = x.astype(BF16)
    r = x - hi.astype(F32)
    mid = r.astype(BF16)
    lo = (r - mid.astype(F32)).astype(BF16)
    return hi, mid, lo


def _dot(a, b):
    return jnp.dot(a, b, preferred_element_type=F32)


def _dot_nt(a, b):
    return lax.dot_general(a, b, (((1,), (1,)), ((), ())), preferred_element_type=F32)


def _dot_tn(a, b):
    return lax.dot_general(a, b, (((0,), (0,)), ((), ())), preferred_element_type=F32)


def _lower_bound_kernel(logit_ref, o_ref):
    z = logit_ref[...]
    n = z.shape[0]
    e = jnp.exp(z - jnp.max(z, axis=0, keepdims=True))
    p = e / jnp.sum(e, axis=0, keepdims=True)
    run = jnp.zeros_like(p[0:1])
    rows = [run]
    for k in range(1, n):
        run = run + p[k:k + 1]
        rows.append(run)
    o_ref[...] = jnp.clip(jnp.concatenate(rows, axis=0), 0.0, 1.0 - 1e-6)


def lower_bounds(logits):
    return pl.pallas_call(
        _lower_bound_kernel,
        out_shape=jax.ShapeDtypeStruct(logits.shape, F32),
        name="hgrn_lower_bounds",
    )(logits.astype(F32))


def _mod_kernel(c_ref, w_ref, b_ref, o_ref):
    s = _silu(c_ref[...]).astype(BF16)
    o_ref[0] = _dot(s, w_ref[0].astype(BF16)) + b_ref[0]


def modulation(c, mod_w, mod_b, tn=2048):
    n_layers, d, n = mod_w.shape
    b = c.shape[0]
    rows = 8
    c_pad = jnp.zeros((rows, d), F32).at[:b].set(c)
    out = pl.pallas_call(
        _mod_kernel,
        grid=(n_layers, n // tn),
        in_specs=[
            pl.BlockSpec((rows, d), lambda l, j: (0, 0)),
            pl.BlockSpec((1, d, tn), lambda l, j: (l, 0, j)),
            pl.BlockSpec((1, 1, tn), lambda l, j: (l, 0, j)),
        ],
        out_specs=pl.BlockSpec((1, rows, tn), lambda l, j: (l, 0, j)),
        out_shape=jax.ShapeDtypeStruct((n_layers, rows, n), F32),
        compiler_params=_params("parallel", "parallel"),
        name="adaln_modulation",
    )(c_pad, mod_w, mod_b.reshape(n_layers, 1, n))
    return out[:, :b]


def _norm_modulate(x, gain, shift, scale):
    var = jnp.mean(x * x, axis=-1, keepdims=True)
    y = x * lax.rsqrt(var + EPS) * gain
    return y * (1.0 + scale) + shift


def _norm_rows(x_ref, gain_ref, sh_ref, sc_ref, h_ref, copy_ref=None):
    def body(r, carry):
        rows = pl.ds(pl.multiple_of(r * NORM_ROWS, NORM_ROWS), NORM_ROWS)
        x = x_ref[rows, :]
        h_ref[rows, :] = _norm_modulate(x, gain_ref[...], sh_ref[0], sc_ref[0]).astype(h_ref.dtype)
        if copy_ref is not None:
            copy_ref[rows, :] = x
        return carry

    lax.fori_loop(0, x_ref.shape[0] // NORM_ROWS, body, 0)


def _norm_proj_kernel(x_ref, gain_ref, sh_ref, sc_ref, w_ref, waux_ref, o_ref, oaux_ref, h_ref):
    @pl.when(pl.program_id(1) == 0)
    def _():
        _norm_rows(x_ref, gain_ref, sh_ref, sc_ref, h_ref)
        oaux_ref[...] = _dot(h_ref[...], waux_ref[...])

    o_ref[...] = _dot(h_ref[...], w_ref[0]).astype(o_ref.dtype)


def norm_proj(x, gain, shift, scale, w_all, layer, n, w_aux, seq, tm=PROJ_TM, tn=PROJ_TN):
    m, d = x.shape
    tm = min(tm, seq)
    per_batch = seq // tm
    return pl.pallas_call(
        _norm_proj_kernel,
        grid=(m // tm, n // tn),
        in_specs=[
            pl.BlockSpec((tm, d), lambda i, j: (i, 0)),
            pl.BlockSpec((1, d), lambda i, j: (0, 0)),
            pl.BlockSpec((1, 1, d), lambda i, j: (i // per_batch, 0, 0)),
            pl.BlockSpec((1, 1, d), lambda i, j: (i // per_batch, 0, 0)),
            pl.BlockSpec((1, d, tn), lambda i, j: (layer, 0, j)),
            pl.BlockSpec((d, LANES), lambda i, j: (0, 0)),
        ],
        out_specs=[
            pl.BlockSpec((tm, tn), lambda i, j: (i, j)),
            pl.BlockSpec((tm, LANES), lambda i, j: (i, 0)),
        ],
        out_shape=[
            jax.ShapeDtypeStruct((m, n), BF16),
            jax.ShapeDtypeStruct((m, LANES), F32),
        ],
        scratch_shapes=[pltpu.VMEM((tm, d), BF16)],
        compiler_params=_params("parallel", "arbitrary"),
        name="norm_in_proj",
    )(x, gain.reshape(1, d), shift, scale, w_all, w_aux)


def _out_proj_kernel(*refs, n_lhs):
    lhs = refs[:n_lhs]
    w_ref, x_ref, gate_ref, o_ref = refs[n_lhs:]
    width = lhs[0].shape[1]
    y = _dot(lhs[0][...], w_ref[0, 0:width, :])
    for part, a_ref in enumerate(lhs[1:], 1):
        y = y + _dot(a_ref[...], w_ref[0, part * width:(part + 1) * width, :])
    o_ref[...] = x_ref[...] + gate_ref[0] * y


def out_proj_residual(lhs_list, w_all, layer, x, gate, seq, tm=OUT_TM):
    m, d = x.shape
    tm = min(tm, seq)
    per_batch = seq // tm
    n_lhs = len(lhs_list)
    width = lhs_list[0].shape[1]
    k = w_all.shape[1]
    assert all(a.shape[1] == width for a in lhs_list) and n_lhs * width == k
    in_specs = (
        [pl.BlockSpec((tm, width), lambda i: (i, 0)) for _ in lhs_list]
        + [pl.BlockSpec((1, k, d), lambda i: (layer, 0, 0), pipeline_mode=pl.Buffered(1)),
           pl.BlockSpec((tm, d), lambda i: (i, 0)),
           pl.BlockSpec((1, 1, d), lambda i: (i // per_batch, 0, 0))]
    )
    return pl.pallas_call(
        functools.partial(_out_proj_kernel, n_lhs=n_lhs),
        grid=(m // tm,),
        in_specs=in_specs,
        out_specs=pl.BlockSpec((tm, d), lambda i: (i, 0)),
        out_shape=jax.ShapeDtypeStruct((m, d), F32),
        compiler_params=_params("parallel"),
        name="out_proj_residual",
    )(*lhs_list, w_all, x, gate)


def _ffn_kernel(x_ref, gain_ref, sh_ref, sc_ref, gate_ref, wa_ref, wu_ref, wo_ref, o_ref, h_ref):
    @pl.when(pl.program_id(1) == 0)
    def _():
        _norm_rows(x_ref, gain_ref, sh_ref, sc_ref, h_ref, copy_ref=o_ref)

    h = h_ref[...]
    a = _dot(h, wa_ref[0])
    u = _dot(h, wu_ref[0])
    act = (_silu(a) * u).astype(BF16)
    d = o_ref.shape[1]
    for c0 in range(0, d, FFN_TN):
        cols = slice(c0, c0 + FFN_TN)
        o_ref[:, cols] += gate_ref[0][:, cols] * _dot(act, wo_ref[0, :, cols])


def ffn_residual(x, gain, shift, scale, gate, w_in, w_out, layer, seq, tm=FFN_TM, tf=FFN_TF):
    m, d = x.shape
    d_ff = w_out.shape[1]
    tm = min(tm, seq)
    per_batch = seq // tm
    nf = d_ff // tf
    vec = pl.BlockSpec((1, 1, d), lambda i, f: (i // per_batch, 0, 0))
    return pl.pallas_call(
        _ffn_kernel,
        grid=(m // tm, nf),
        in_specs=[
            pl.BlockSpec((tm, d), lambda i, f: (i, 0)),
            pl.BlockSpec((1, d), lambda i, f: (0, 0)),
            vec, vec, vec,
            pl.BlockSpec((1, d, tf), lambda i, f: (layer, 0, f)),
            pl.BlockSpec((1, d, tf), lambda i, f: (layer, 0, nf + f)),
            pl.BlockSpec((1, tf, d), lambda i, f: (layer, f, 0)),
        ],
        out_specs=pl.BlockSpec((tm, d), lambda i, f: (i, 0)),
        out_shape=jax.ShapeDtypeStruct((m, d), F32),
        scratch_shapes=[pltpu.VMEM((tm, d), BF16)],
        compiler_params=_params("parallel", "arbitrary"),
        name="swiglu_ffn_residual",
    )(x, gain.reshape(1, d), shift, scale, gate, w_in, w_in, w_out)


def _cumsum_rows(g, tril):
    n = g.shape[1]
    y = _dot(tril, jnp.concatenate(_split3(g), axis=1))
    return y[:, :n] + y[:, n:2 * n] + y[:, 2 * n:]


def _glr_heads(qs, ks, vs, cums, s_ref, sub):
    heads = range(len(qs))
    c, dk = qs[0].shape
    dv = vs[0].shape[1]
    vbs = [v.astype(BF16) for v in vs]
    lasts = [cum[c - 1:c, :] for cum in cums]

    inter = [_dot((qs[h] * jnp.exp(cums[h])).astype(BF16), s_ref[h].astype(BF16)) for h in heads]
    updates = [_dot_tn((ks[h] * jnp.exp(lasts[h] - cums[h])).astype(BF16), vbs[h]) for h in heads]
    for h in heads:
        decay_cols = jnp.transpose(jnp.broadcast_to(jnp.exp(lasts[h]), (LANES, dk)))
        s_ref[h] = jnp.concatenate([decay_cols] * (dv // LANES), axis=1) * s_ref[h] + updates[h]

    spans = [(r0, r0 + sub) for r0 in range(0, c, sub)]
    scores = []
    for h in heads:
        q, k, cum = qs[h], ks[h], cums[h]
        per_head = []
        for r0, r1 in spans:
            hi = cum[r1 - 1:r1, :]
            mid = 0.5 * hi if r0 == 0 else 0.5 * (cum[r0 - 1:r0, :] + hi)
            qi = (q[r0:r1] * jnp.exp(cum[r0:r1] - mid)).astype(BF16)
            ki = (k[:r1] * jnp.exp(mid - cum[:r1])).astype(BF16)
            a = _dot_nt(qi, ki)
            row = r0 + lax.broadcasted_iota(jnp.int32, (sub, r1), 0)
            col = lax.broadcasted_iota(jnp.int32, (sub, r1), 1)
            per_head.append(jnp.where(col <= row, a, 0.0).astype(BF16))
        scores.append(per_head)
    outs = []
    for h in heads:
        blocks = [_dot(a, vbs[h][:r1]) for a, (_, r1) in zip(scores[h], spans)]
        intra = blocks[0] if len(blocks) == 1 else jnp.concatenate(blocks, axis=0)
        outs.append(inter[h] + intra)
    return outs


def _tril_bf16(c):
    row = lax.broadcasted_iota(jnp.int32, (c, c), 0)
    col = lax.broadcasted_iota(jnp.int32, (c, c), 1)
    return jnp.where(col <= row, 1.0, 0.0).astype(BF16)


def _hgrn_kernel(q_ref, f_ref, i_ref, g_ref, lb_ref, gain_ref, o_ref, s_ref, *, chunk, sub):
    @pl.when(pl.program_id(1) == 0)
    def _():
        s_ref[...] = jnp.zeros_like(s_ref)

    heads, dk, _ = s_ref.shape
    tril = _tril_bf16(chunk)

    def body(ci, carry):
        rows = pl.ds(pl.multiple_of(ci * chunk, chunk), chunk)
        lb = lb_ref[...]
        sig = _sigmoid(f_ref[rows, :].astype(F32))
        forget = lb + (1.0 - lb) * sig
        cum = _cumsum_rows(jnp.log(jnp.maximum(forget, MIN_FORGET)), tril)
        key = (1.0 - lb) * (1.0 - sig)
        q = _silu(q_ref[rows, :].astype(F32))
        v = i_ref[rows, :]
        cols = [slice(hh * dk, (hh + 1) * dk) for hh in range(heads)]
        outs = _glr_heads([q[:, cs] for cs in cols], [key[:, cs] for cs in cols],
                          [v[:, cs] for cs in cols], [cum[:, cs] for cs in cols], s_ref, sub)
        for o, cs in zip(outs, cols):
            o = o * lax.rsqrt(jnp.mean(o * o, axis=-1, keepdims=True) + EPS) * gain_ref[:, cs]
            o_ref[rows, cs] = (o * _silu(g_ref[rows, cs].astype(F32))).astype(o_ref.dtype)
        return carry

    lax.fori_loop(0, q_ref.shape[0] // chunk, body, 0, unroll=2)


def hgrn2_heads(proj, lb, out_gain, batch, seq, tb=REC_TB, chunk=REC_CHUNK, sub=HGRN_SUB):
    m = proj.shape[0]
    tb = min(tb, seq)
    nt = seq // tb
    h, dk = A_HEADS, A_DK
    w = h * dk

    def col(group):
        return pl.BlockSpec((tb, w), lambda b, t: (b * nt + t, group))

    vec = pl.BlockSpec((1, w), lambda b, t: (0, 0))
    return pl.pallas_call(
        functools.partial(_hgrn_kernel, chunk=chunk, sub=sub),
        grid=(batch, nt),
        in_specs=[col(0), col(1), col(2), col(3), vec, vec],
        out_specs=pl.BlockSpec((tb, w), lambda b, t: (b * nt + t, 0)),
        out_shape=jax.ShapeDtypeStruct((m, w), BF16),
        scratch_shapes=[pltpu.VMEM((h, dk, dk), F32)],
        compiler_params=_params("parallel", "arbitrary"),
        name="hgrn2_recurrence",
    )(proj, proj, proj, proj, lb.reshape(1, w), out_gain.reshape(1, w))


def _gla_kernel(q_ref, k_ref, v_ref, g_ref, low_ref, wup_ref, bias_ref, gain_ref, o_ref, s_ref,
                *, chunk, sub):
    @pl.when(pl.program_id(1) == 0)
    def _():
        s_ref[...] = jnp.zeros_like(s_ref)

    heads, dk, dv = s_ref.shape
    tril = _tril_bf16(chunk)
    q_scale = dk ** -0.5

    def body(ci, carry):
        rows = pl.ds(pl.multiple_of(ci * chunk, chunk), chunk)
        z = _dot(low_ref[rows, :].astype(BF16), wup_ref[...]) + bias_ref[...]
        cum = _cumsum_rows(_log_sigmoid(z) * (1.0 / GLA_GATE_NORMALIZER), tril)
        kcs = [slice(hh * dk, (hh + 1) * dk) for hh in range(heads)]
        vcs = [slice(hh * dv, (hh + 1) * dv) for hh in range(heads)]
        outs = _glr_heads([q_ref[rows, kc].astype(F32) * q_scale for kc in kcs],
                          [k_ref[rows, kc].astype(F32) for kc in kcs],
                          [v_ref[rows, vc] for vc in vcs], [cum[:, kc] for kc in kcs], s_ref, sub)
        for o, vc in zip(outs, vcs):
            o = o * lax.rsqrt(jnp.mean(o * o, axis=-1, keepdims=True) + EPS) * gain_ref[...]
            o_ref[rows, vc] = (o * _silu(g_ref[rows, vc].astype(F32))).astype(o_ref.dtype)
        return carry

    lax.fori_loop(0, q_ref.shape[0] // chunk, body, 0)


def gla_heads(proj, low, w_up, b_gate, out_gain, batch, seq, tb=REC_TB, chunk=GLA_CHUNK,
              sub=GLA_SUB):
    m = proj.shape[0]
    tb = min(tb, seq)
    nt = seq // tb
    h, dk, dv = C_HEADS, C_DK, C_DV
    kw, vw = h * dk, h * dv
    v0 = 2 * kw // vw
    return pl.pallas_call(
        functools.partial(_gla_kernel, chunk=chunk, sub=sub),
        grid=(batch, nt),
        in_specs=[
            pl.BlockSpec((tb, kw), lambda b, t: (b * nt + t, 0)),
            pl.BlockSpec((tb, kw), lambda b, t: (b * nt + t, 1)),
            pl.BlockSpec((tb, vw), lambda b, t: (b * nt + t, v0)),
            pl.BlockSpec((tb, vw), lambda b, t: (b * nt + t, v0 + 1)),
            pl.BlockSpec((tb, LANES), lambda b, t: (b * nt + t, 0)),
            pl.BlockSpec((LANES, kw), lambda b, t: (0, 0)),
            pl.BlockSpec((1, kw), lambda b, t: (0, 0)),
            pl.BlockSpec((1, dv), lambda b, t: (0, 0)),
        ],
        out_specs=pl.BlockSpec((tb, vw), lambda b, t: (b * nt + t, 0)),
        out_shape=jax.ShapeDtypeStruct((m, vw), BF16),
        scratch_shapes=[pltpu.VMEM((h, dk, dv), F32)],
        compiler_params=_params("parallel", "arbitrary"),
        name="gla_recurrence",
    )(proj, proj, proj, proj, low, w_up, b_gate.reshape(1, kw), out_gain.reshape(1, dv))


LOG2E = 1.4426950408889634
F_PIECES = 3


def _fox_prep_kernel(q_ref, k_ref, logit_ref, bias_ref, qg_ref, kg_ref,
                     qo_ref, ko_ref, carry_ref, *, heads, head_dim):
    @pl.when(pl.program_id(1) == 0)
    def _():
        carry_ref[...] = jnp.zeros_like(carry_ref)

    tb = q_ref.shape[0]
    log_f = _log_sigmoid(logit_ref[...] + bias_ref[...])
    cum = _cumsum_rows(log_f, _tril_bf16(tb)) + carry_ref[0:1, :]
    carry_ref[...] = jnp.broadcast_to(cum[tb - 1:tb, :], carry_ref.shape)
    pieces = [p.astype(F32) for p in _split3(-LOG2E * cum)]
    lane = lax.broadcasted_iota(jnp.int32, (tb, head_dim), 1)
    one_cols = jnp.where(lane < F_PIECES, 1.0, 0.0).astype(qo_ref.dtype)

    qg = qg_ref[...] * (head_dim ** -0.5 * LOG2E)
    kg = kg_ref[...]
    for hh in range(heads):
        cols = slice(hh * head_dim, (hh + 1) * head_dim)
        lo = 2 * hh * head_dim
        main = slice(lo, lo + head_dim)
        extra = slice(lo + head_dim, lo + 2 * head_dim)
        q = q_ref[:, cols].astype(F32)
        qo_ref[:, main] = (q * lax.rsqrt(jnp.mean(q * q, axis=-1, keepdims=True) + EPS) * qg
                           ).astype(qo_ref.dtype)
        qo_ref[:, extra] = one_cols
        k = k_ref[:, cols].astype(F32)
        ko_ref[:, main] = (k * lax.rsqrt(jnp.mean(k * k, axis=-1, keepdims=True) + EPS) * kg
                           ).astype(ko_ref.dtype)
        bias_cols = jnp.zeros((tb, head_dim), F32)
        for j, piece in enumerate(pieces):
            bias_cols = jnp.where(lane == j, piece[:, hh:hh + 1], bias_cols)
        ko_ref[:, extra] = bias_cols.astype(ko_ref.dtype)


def fox_prep(proj, logits, f_bias, q_gain, k_gain, batch, seq, tb=PREP_TB):
    m = proj.shape[0]
    h, hd = B_HEADS, B_HEAD_DIM
    w = h * hd
    tb = min(tb, seq)
    nt = seq // tb
    base = (4 * A_HEADS * A_DK) // w
    bias = jnp.zeros((1, LANES), F32).at[0, :h].set(f_bias)

    def col(group):
        return pl.BlockSpec((tb, w), lambda b, t: (b * nt + t, base + group))

    wide_tok = pl.BlockSpec((tb, 2 * w), lambda b, t: (b * nt + t, 0))
    return pl.pallas_call(
        functools.partial(_fox_prep_kernel, heads=h, head_dim=hd),
        grid=(batch, nt),
        in_specs=[
            col(0), col(1),
            pl.BlockSpec((tb, LANES), lambda b, t: (b * nt + t, 0)),
            pl.BlockSpec((1, LANES), lambda b, t: (0, 0)),
            pl.BlockSpec((1, hd), lambda b, t: (0, 0)),
            pl.BlockSpec((1, hd), lambda b, t: (0, 0)),
        ],
        out_specs=[wide_tok, wide_tok],
        out_shape=[jax.ShapeDtypeStruct((m, 2 * w), BF16)] * 2,
        scratch_shapes=[pltpu.VMEM((8, LANES), F32)],
        compiler_params=_params("parallel", "arbitrary"),
        name="fox_prep",
    )(proj, proj, logits, bias, q_gain.reshape(1, hd), k_gain.reshape(1, hd))


def _fox_attn_kernel(q_ref, k_ref, v_ref, g_ref, o_ref, m_ref, acc_ref, *, tq, tk, heads, head_dim):
    qi = pl.program_id(2)
    q_start = pl.multiple_of(qi * tq, tq)
    wide = 2 * head_dim
    def tile(start, mask, width=tk):
        rows = pl.ds(start, width)
        ones = jnp.ones((width, head_dim), BF16)
        scores = [_dot_nt(q_ref[:, hh * wide:(hh + 1) * wide], k_ref[rows, hh * wide:(hh + 1) * wide])
                  for hh in range(heads)]
        for hh, s in enumerate(scores):
            if mask is not None:
                s = jnp.where(mask, s, MASK_VALUE)
            m_prev = m_ref[hh]
            m_new = jnp.maximum(m_prev, jnp.max(s, axis=-1, keepdims=True))
            p = jnp.exp2(s - jnp.concatenate([m_new] * (width // LANES), axis=1))
            alpha = jnp.exp2(m_prev - m_new)
            m_ref[hh] = m_new
            v_aug = jnp.concatenate([v_ref[rows, hh * head_dim:(hh + 1) * head_dim], ones], axis=1)
            acc_ref[hh] = (jnp.concatenate([alpha] * (wide // LANES), axis=1) * acc_ref[hh]
                           + _dot(p.astype(BF16), v_aug))

    m_ref[...] = jnp.full(m_ref.shape, MASK_VALUE, F32)
    acc_ref[...] = jnp.zeros_like(acc_ref)

    n_full = qi * (tq // tk)

    def quad(gi, carry):
        tile(pl.multiple_of(4 * gi * tk, 4 * tk), None, width=4 * tk)
        return carry

    lax.fori_loop(0, n_full // 4, quad, 0)

    @pl.when(n_full % 4 >= 2)
    def _():
        tile(pl.multiple_of((n_full // 4) * 4 * tk, 2 * tk), None, width=2 * tk)

    @pl.when(n_full % 2 == 1)
    def _():
        tile(pl.multiple_of((n_full - 1) * tk, tk), None)

    for d in range(tq // tk):
        row = lax.broadcasted_iota(jnp.int32, (tq, tk), 0)
        col = lax.broadcasted_iota(jnp.int32, (tq, tk), 1) + d * tk
        tile(pl.multiple_of(q_start + d * tk, tk), col <= row)
    for hh in range(heads):
        acc = acc_ref[hh]
        cols = slice(hh * head_dim, (hh + 1) * head_dim)
        o_ref[:, cols] = (acc[:, :head_dim] / acc[:, head_dim:]
                          * _sigmoid(g_ref[:, cols].astype(F32))).astype(o_ref.dtype)


def fox_attention(q_aug, k_aug, proj, batch, seq, tq=ATT_TQ, tk=ATT_TK, group=ATT_HEADS):
    m = q_aug.shape[0]
    h, hd = B_HEADS, B_HEAD_DIM
    tq = min(tq, seq)
    tk = min(tk, tq)
    nq = seq // tq
    v_block = (4 * A_HEADS * A_DK + 2 * h * hd) // (group * hd)
    gate_block = v_block + h // group
    return pl.pallas_call(
        functools.partial(_fox_attn_kernel, tq=tq, tk=tk, heads=group, head_dim=hd),
        grid=(batch, h // group, nq),
        in_specs=[
            pl.BlockSpec((tq, 2 * group * hd), lambda b, hg, i: (b * nq + i, hg)),
            pl.BlockSpec((seq, 2 * group * hd), lambda b, hg, i: (b, hg),
                         pipeline_mode=pl.Buffered(1)),
            pl.BlockSpec((seq, group * hd), lambda b, hg, i: (b, v_block + hg),
                         pipeline_mode=pl.Buffered(1)),
            pl.BlockSpec((tq, group * hd), lambda b, hg, i: (b * nq + i, gate_block + hg)),
        ],
        out_specs=pl.BlockSpec((tq, group * hd), lambda b, hg, i: (b * nq + i, hg)),
        out_shape=jax.ShapeDtypeStruct((m, h * hd), BF16),
        scratch_shapes=[pltpu.VMEM((group, tq, LANES), F32), pltpu.VMEM((group, tq, 2 * hd), F32)],
        compiler_params=_params("parallel", "parallel", "arbitrary"),
        name="fox_attention",
    )(q_aug, k_aug, proj, proj)


def _pad_cols(w, n):
    return jnp.zeros((w.shape[0], n), w.dtype).at[:, :w.shape[1]].set(w)


def kernel(x, c, mod_w, mod_b, norm_mix_gain, norm_ffn_gain, ab_w_in, ab_w_out, hgrn_lb_logits,
           hgrn_out_gain, fox_q_gain, fox_k_gain, fox_f_bias, gla_w_in, gla_w_gate_up, gla_b_gate,
           gla_out_gain, gla_w_out, ffn_w_in, ffn_w_out):
    batch, seq, d = x.shape
    depth = mod_w.shape[0]
    ab_main = 4 * A_HEADS * A_DK + 4 * B_HEADS * B_HEAD_DIM
    c_main = 2 * C_HEADS * C_DK + 2 * C_HEADS * C_DV

    lbs = lower_bounds(hgrn_lb_logits)
    mod = modulation(c, mod_w, mod_b)
    xs = x.reshape(batch * seq, d)
    ab_w_in_b, ab_w_out_b = ab_w_in.astype(BF16), ab_w_out.astype(BF16)
    gla_w_in_b, gla_w_out_b = gla_w_in.astype(BF16), gla_w_out.astype(BF16)
    ffn_w_in_b, ffn_w_out_b = ffn_w_in.astype(BF16), ffn_w_out.astype(BF16)

    for layer in range(depth):
        sh1, sc1, g1, sh2, sc2, g2 = [
            mod[layer, :, i * d:(i + 1) * d].reshape(batch, 1, d) for i in range(6)]
        if layer % 2 == 0:
            i = layer // 2
            proj, logits = norm_proj(xs, norm_mix_gain[layer], sh1, sc1, ab_w_in_b, i, ab_main,
                                     _pad_cols(ab_w_in[i, :, ab_main:], LANES).astype(BF16), seq)
            o_a = hgrn2_heads(proj, lbs[i], hgrn_out_gain[i], batch, seq)
            q_aug, k_aug = fox_prep(proj, logits, fox_f_bias[i], fox_q_gain[i], fox_k_gain[i],
                                    batch, seq)
            o_b = fox_attention(q_aug, k_aug, proj, batch, seq)
            xs = out_proj_residual([o_a, o_b], ab_w_out_b, i, xs, g1, seq)
        else:
            j = layer // 2
            proj, low = norm_proj(xs, norm_mix_gain[layer], sh1, sc1, gla_w_in_b, j, c_main,
                                  _pad_cols(gla_w_in[j, :, c_main:], LANES).astype(BF16), seq)
            w_up = jnp.zeros((LANES, gla_w_gate_up.shape[2]), BF16).at[:C_GATE_RANK].set(
                gla_w_gate_up[j].astype(BF16))
            o_c = gla_heads(proj, low, w_up, gla_b_gate[j], gla_out_gain[j], batch, seq)
            xs = out_proj_residual([o_c], gla_w_out_b, j, xs, g1, seq)
        xs = ffn_residual(xs, norm_ffn_gain[layer], sh2, sc2, g2, ffn_w_in_b, ffn_w_out_b, layer, seq)
    return xs.reshape(batch, seq, d)
```

```python
import functools

import jax
import jax.numpy as jnp
from jax import lax
from jax.experimental import pallas as pl
from jax.experimental.pallas import tpu as pltpu

F32 = jnp.float32
BF16 = jnp.bfloat16

EPS = 1e-6
MIN_FORGET = 1e-30
MASK_VALUE = -1e30
GLA_GATE_NORMALIZER = 16.0
LOG2E = 1.4426950408889634

LANES = 128
VMEM_LIMIT = 60 * 1024 * 1024

A_HEADS = 8
A_DK = 128
B_HEADS = 8
B_HEAD_DIM = 128
C_HEADS = 4
C_DK = 256
C_DV = 512
C_GATE_RANK = 16

PROJ_TM = 1024
PROJ_TN = 2048
PROJ_TC = 512
OUT_TM = 512
FFN_TM = 1024
FFN_TF = 512
FFN_TN = 512
NORM_ROWS = 16
NORM_UNROLL = 8
REC_TB = 512
REC_CHUNK = 64
HGRN_SUB = 16
GLA_CHUNK = 256
GLA_SUB = 64
ATT_TQ = 512
ATT_TK = 512
ATT_HEADS = 4
PREP_TB = 512


def _params(*sem):
    return pltpu.CompilerParams(dimension_semantics=sem, vmem_limit_bytes=VMEM_LIMIT)


def _sigmoid(x):
    return 1.0 / (1.0 + jnp.exp2(x * (-LOG2E)))


def _silu(x):
    return x * _sigmoid(x)


def _log_sigmoid(x):
    return jnp.minimum(x, 0.0) - jnp.log(1.0 + jnp.exp(-jnp.abs(x)))


def _split3(x):
    hi = x.astype(BF16)
    r = x - hi.astype(F32)
    mid = r.astype(BF16)
    lo = (r - mid.astype(F32)).astype(BF16)
    return hi, mid, lo


def _dot(a, b):
    return jnp.dot(a, b, preferred_element_type=F32)


def _dot_nt(a, b):
    return lax.dot_general(a, b, (((1,), (1,)), ((), ())), preferred_element_type=F32)


def _dot_tn(a, b):
    return lax.dot_general(a, b, (((0,), (0,)), ((), ())), preferred_element_type=F32)


def _lower_bound_kernel(logit_ref, o_ref):
    z = logit_ref[...]
    n = z.shape[0]
    e = jnp.exp(z - jnp.max(z, axis=0, keepdims=True))
    p = e / jnp.sum(e, axis=0, keepdims=True)
    run = jnp.zeros_like(p[0:1])
    rows = [run]
    for k in range(1, n):
        run = run + p[k:k + 1]
        rows.append(run)
    o_ref[...] = jnp.clip(jnp.concatenate(rows, axis=0), 0.0, 1.0 - 1e-6)


def lower_bounds(logits):
    return pl.pallas_call(
        _lower_bound_kernel,
        out_shape=jax.ShapeDtypeStruct(logits.shape, F32),
        name="hgrn_lower_bounds",
    )(logits.astype(F32))


def _mod_kernel(c_ref, w_ref, b_ref, o_ref):
    s = _silu(c_ref[...]).astype(BF16)
    o_ref[0] = _dot(s, w_ref[0].astype(BF16)) + b_ref[0]


def modulation(c, mod_w, mod_b, tn=2048):
    n_layers, d, n = mod_w.shape
    b = c.shape[0]
    rows = 8
    c_pad = jnp.zeros((rows, d), F32).at[:b].set(c)
    out = pl.pallas_call(
        _mod_kernel,
        grid=(n_layers, n // tn),
        in_specs=[
            pl.BlockSpec((rows, d), lambda l, j: (0, 0)),
            pl.BlockSpec((1, d, tn), lambda l, j: (l, 0, j)),
            pl.BlockSpec((1, 1, tn), lambda l, j: (l, 0, j)),
        ],
        out_specs=pl.BlockSpec((1, rows, tn), lambda l, j: (l, 0, j)),
        out_shape=jax.ShapeDtypeStruct((n_layers, rows, n), F32),
        compiler_params=_params("parallel", "parallel"),
        name="adaln_modulation",
    )(c_pad, mod_w, mod_b.reshape(n_layers, 1, n))
    return out[:, :b]


def _norm_rows(x_ref, gain_ref, sh_ref, sc_ref, h_ref, copy_ref=None):
    col_scale = gain_ref[...] * (1.0 + sc_ref[0])
    shift = sh_ref[0]

    def body(r, carry):
        rows = pl.ds(pl.multiple_of(r * NORM_ROWS, NORM_ROWS), NORM_ROWS)
        x = x_ref[rows, :]
        inv = lax.rsqrt(jnp.mean(x * x, axis=-1, keepdims=True) + EPS)
        h_ref[rows, :] = (x * inv * col_scale + shift).astype(h_ref.dtype)
        if copy_ref is not None:
            copy_ref[rows, :] = x
        return carry

    lax.fori_loop(0, x_ref.shape[0] // NORM_ROWS, body, 0, unroll=NORM_UNROLL)


def _norm_proj_kernel(x_ref, gain_ref, sh_ref, sc_ref, w_ref, waux_ref, o_ref, oaux_ref, h_ref):
    @pl.when(pl.program_id(1) == 0)
    def _():
        _norm_rows(x_ref, gain_ref, sh_ref, sc_ref, h_ref)
        oaux_ref[...] = _dot(h_ref[...], waux_ref[...])

    h = h_ref[...]
    for c0 in range(0, o_ref.shape[1], PROJ_TC):
        cols = slice(c0, c0 + PROJ_TC)
        o_ref[:, cols] = _dot(h, w_ref[0, :, cols]).astype(o_ref.dtype)


def norm_proj(x, gain, shift, scale, w_all, layer, n, w_aux, seq, tm=PROJ_TM, tn=PROJ_TN):
    m, d = x.shape
    tm = min(tm, seq)
    per_batch = seq // tm
    return pl.pallas_call(
        _norm_proj_kernel,
        grid=(m // tm, n // tn),
        in_specs=[
            pl.BlockSpec((tm, d), lambda i, j: (i, 0)),
            pl.BlockSpec((1, d), lambda i, j: (0, 0)),
            pl.BlockSpec((1, 1, d), lambda i, j: (i // per_batch, 0, 0)),
            pl.BlockSpec((1, 1, d), lambda i, j: (i // per_batch, 0, 0)),
            pl.BlockSpec((1, d, tn), lambda i, j: (layer, 0, j)),
            pl.BlockSpec((d, LANES), lambda i, j: (0, 0)),
        ],
        out_specs=[
            pl.BlockSpec((tm, tn), lambda i, j: (i, j)),
            pl.BlockSpec((tm, LANES), lambda i, j: (i, 0)),
        ],
        out_shape=[
            jax.ShapeDtypeStruct((m, n), BF16),
            jax.ShapeDtypeStruct((m, LANES), F32),
        ],
        scratch_shapes=[pltpu.VMEM((tm, d), BF16)],
        compiler_params=_params("parallel", "arbitrary"),
        name="norm_in_proj",
    )(x, gain.reshape(1, d), shift, scale, w_all, w_aux)


def _out_proj_kernel(*refs, n_lhs):
    lhs = refs[:n_lhs]
    w_ref, x_ref, gate_ref, o_ref = refs[n_lhs:]
    width = lhs[0].shape[1]
    y = _dot(lhs[0][...], w_ref[0, 0:width, :])
    for part, a_ref in enumerate(lhs[1:], 1):
        y = y + _dot(a_ref[...], w_ref[0, part * width:(part + 1) * width, :])
    o_ref[...] = x_ref[...] + gate_ref[0] * y


def out_proj_residual(lhs_list, w_all, layer, x, gate, seq, tm=OUT_TM):
    m, d = x.shape
    tm = min(tm, seq)
    per_batch = seq // tm
    n_lhs = len(lhs_list)
    width = lhs_list[0].shape[1]
    k = w_all.shape[1]
    assert all(a.shape[1] == width for a in lhs_list) and n_lhs * width == k
    in_specs = (
        [pl.BlockSpec((tm, width), lambda i: (i, 0)) for _ in lhs_list]
        + [pl.BlockSpec((1, k, d), lambda i: (layer, 0, 0), pipeline_mode=pl.Buffered(1)),
           pl.BlockSpec((tm, d), lambda i: (i, 0)),
           pl.BlockSpec((1, 1, d), lambda i: (i // per_batch, 0, 0))]
    )
    return pl.pallas_call(
        functools.partial(_out_proj_kernel, n_lhs=n_lhs),
        grid=(m // tm,),
        in_specs=in_specs,
        out_specs=pl.BlockSpec((tm, d), lambda i: (i, 0)),
        out_shape=jax.ShapeDtypeStruct((m, d), F32),
        compiler_params=_params("parallel"),
        name="out_proj_residual",
    )(*lhs_list, w_all, x, gate)


def _ffn_kernel(x_ref, gain_ref, sh_ref, sc_ref, gate_ref, wa_ref, wu_ref, wo_ref, o_ref, h_ref):
    @pl.when(pl.program_id(1) == 0)
    def _():
        _norm_rows(x_ref, gain_ref, sh_ref, sc_ref, h_ref, copy_ref=o_ref)

    h = h_ref[...]
    a = _dot(h, wa_ref[0])
    u = _dot(h, wu_ref[0])
    act = (_silu(a) * u).astype(BF16)
    d = o_ref.shape[1]
    for c0 in range(0, d, FFN_TN):
        cols = slice(c0, c0 + FFN_TN)
        o_ref[:, cols] += gate_ref[0][:, cols] * _dot(act, wo_ref[0, :, cols])


def ffn_residual(x, gain, shift, scale, gate, w_in, w_out, layer, seq, tm=FFN_TM, tf=FFN_TF):
    m, d = x.shape
    d_ff = w_out.shape[1]
    tm = min(tm, seq)
    per_batch = seq // tm
    nf = d_ff // tf
    vec = pl.BlockSpec((1, 1, d), lambda i, f: (i // per_batch, 0, 0))
    return pl.pallas_call(
        _ffn_kernel,
        grid=(m // tm, nf),
        in_specs=[
            pl.BlockSpec((tm, d), lambda i, f: (i, 0)),
            pl.BlockSpec((1, d), lambda i, f: (0, 0)),
            vec, vec, vec,
            pl.BlockSpec((1, d, tf), lambda i, f: (layer, 0, f)),
            pl.BlockSpec((1, d, tf), lambda i, f: (layer, 0, nf + f)),
            pl.BlockSpec((1, tf, d), lambda i, f: (layer, f, 0)),
        ],
        out_specs=pl.BlockSpec((tm, d), lambda i, f: (i, 0)),
        out_shape=jax.ShapeDtypeStruct((m, d), F32),
        scratch_shapes=[pltpu.VMEM((tm, d), BF16)],
        compiler_params=_params("parallel", "arbitrary"),
        name="swiglu_ffn_residual",
    )(x, gain.reshape(1, d), shift, scale, gate, w_in, w_in, w_out)


def _cumsum_rows(g, tril):
    n = g.shape[1]
    y = _dot(tril, jnp.concatenate(_split3(g), axis=1))
    return y[:, :n] + y[:, n:2 * n] + y[:, 2 * n:]


def _glr_heads(qs, ks, vs, cums, s_ref, sub):
    heads = range(len(qs))
    c, dk = qs[0].shape
    dv = vs[0].shape[1]
    vbs = [v.astype(BF16) for v in vs]
    lasts = [cum[c - 1:c, :] for cum in cums]

    inter = [_dot((qs[h] * jnp.exp2(cums[h])).astype(BF16), s_ref[h].astype(BF16)) for h in heads]
    updates = [_dot_tn((ks[h] * jnp.exp2(lasts[h] - cums[h])).astype(BF16), vbs[h]) for h in heads]
    for h in heads:
        decay_cols = jnp.transpose(jnp.broadcast_to(jnp.exp2(lasts[h]), (LANES, dk)))
        s_ref[h] = jnp.concatenate([decay_cols] * (dv // LANES), axis=1) * s_ref[h] + updates[h]

    spans = [(r0, r0 + sub) for r0 in range(0, c, sub)]
    scores = []
    for h in heads:
        q, k, cum = qs[h], ks[h], cums[h]
        per_head = []
        for r0, r1 in spans:
            hi = cum[r1 - 1:r1, :]
            mid = 0.5 * hi if r0 == 0 else 0.5 * (cum[r0 - 1:r0, :] + hi)
            qi = (q[r0:r1] * jnp.exp2(cum[r0:r1] - mid)).astype(BF16)
            ki = (k[:r1] * jnp.exp2(mid - cum[:r1])).astype(BF16)
            a = _dot_nt(qi, ki)
            row = r0 + lax.broadcasted_iota(jnp.int32, (sub, r1), 0)
            col = lax.broadcasted_iota(jnp.int32, (sub, r1), 1)
            per_head.append(jnp.where(col <= row, a, 0.0).astype(BF16))
        scores.append(per_head)
    outs = []
    for h in heads:
        blocks = [_dot(a, vbs[h][:r1]) for a, (_, r1) in zip(scores[h], spans)]
        intra = blocks[0] if len(blocks) == 1 else jnp.concatenate(blocks, axis=0)
        outs.append(inter[h] + intra)
    return outs


def _tril_bf16(c):
    row = lax.broadcasted_iota(jnp.int32, (c, c), 0)
    col = lax.broadcasted_iota(jnp.int32, (c, c), 1)
    return jnp.where(col <= row, 1.0, 0.0).astype(BF16)


def _hgrn_kernel(q_ref, f_ref, i_ref, g_ref, lb_ref, gain_ref, o_ref, s_ref, *, chunk, sub):
    @pl.when(pl.program_id(1) == 0)
    def _():
        s_ref[...] = jnp.zeros_like(s_ref)

    heads, dk, _ = s_ref.shape
    tril = _tril_bf16(chunk)

    def body(ci, carry):
        rows = pl.ds(pl.multiple_of(ci * chunk, chunk), chunk)
        lb = lb_ref[...]
        sig = _sigmoid(f_ref[rows, :].astype(F32))
        forget = lb + (1.0 - lb) * sig
        cum = _cumsum_rows(jnp.log2(jnp.maximum(forget, MIN_FORGET)), tril)
        key = (1.0 - lb) * (1.0 - sig)
        q = _silu(q_ref[rows, :].astype(F32))
        v = i_ref[rows, :]
        cols = [slice(hh * dk, (hh + 1) * dk) for hh in range(heads)]
        outs = _glr_heads([q[:, cs] for cs in cols], [key[:, cs] for cs in cols],
                          [v[:, cs] for cs in cols], [cum[:, cs] for cs in cols], s_ref, sub)
        for o, cs in zip(outs, cols):
            o = o * lax.rsqrt(jnp.mean(o * o, axis=-1, keepdims=True) + EPS) * gain_ref[:, cs]
            o_ref[rows, cs] = (o * _silu(g_ref[rows, cs].astype(F32))).astype(o_ref.dtype)
        return carry

    lax.fori_loop(0, q_ref.shape[0] // chunk, body, 0, unroll=2)


def hgrn2_heads(proj, lb, out_gain, batch, seq, tb=REC_TB, chunk=REC_CHUNK, sub=HGRN_SUB):
    m = proj.shape[0]
    tb = min(tb, seq)
    nt = seq // tb
    h, dk = A_HEADS, A_DK
    w = h * dk

    def col(group):
        return pl.BlockSpec((tb, w), lambda b, t: (b * nt + t, group))

    vec = pl.BlockSpec((1, w), lambda b, t: (0, 0))
    return pl.pallas_call(
        functools.partial(_hgrn_kernel, chunk=chunk, sub=sub),
        grid=(batch, nt),
        in_specs=[col(0), col(1), col(2), col(3), vec, vec],
        out_specs=pl.BlockSpec((tb, w), lambda b, t: (b * nt + t, 0)),
        out_shape=jax.ShapeDtypeStruct((m, w), BF16),
        scratch_shapes=[pltpu.VMEM((h, dk, dk), F32)],
        compiler_params=_params("parallel", "arbitrary"),
        name="hgrn2_recurrence",
    )(proj, proj, proj, proj, lb.reshape(1, w), out_gain.reshape(1, w))


def _gla_kernel(q_ref, k_ref, v_ref, g_ref, low_ref, wup_ref, bias_ref, gain_ref, o_ref, s_ref,
                *, chunk, sub):
    @pl.when(pl.program_id(1) == 0)
    def _():
        s_ref[...] = jnp.zeros_like(s_ref)

    heads, dk, dv = s_ref.shape
    tril = _tril_bf16(chunk)
    q_scale = dk ** -0.5

    def body(ci, carry):
        rows = pl.ds(pl.multiple_of(ci * chunk, chunk), chunk)
        z = _dot(low_ref[rows, :].astype(BF16), wup_ref[...]) + bias_ref[...]
        cum = _cumsum_rows(_log_sigmoid(z) * (LOG2E / GLA_GATE_NORMALIZER), tril)
        kcs = [slice(hh * dk, (hh + 1) * dk) for hh in range(heads)]
        vcs = [slice(hh * dv, (hh + 1) * dv) for hh in range(heads)]
        outs = _glr_heads([q_ref[rows, kc].astype(F32) * q_scale for kc in kcs],
                          [k_ref[rows, kc].astype(F32) for kc in kcs],
                          [v_ref[rows, vc] for vc in vcs], [cum[:, kc] for kc in kcs], s_ref, sub)
        for o, vc in zip(outs, vcs):
            o = o * lax.rsqrt(jnp.mean(o * o, axis=-1, keepdims=True) + EPS) * gain_ref[...]
            o_ref[rows, vc] = (o * _silu(g_ref[rows, vc].astype(F32))).astype(o_ref.dtype)
        return carry

    lax.fori_loop(0, q_ref.shape[0] // chunk, body, 0)


def gla_heads(proj, low, w_up, b_gate, out_gain, batch, seq, tb=REC_TB, chunk=GLA_CHUNK,
              sub=GLA_SUB):
    m = proj.shape[0]
    tb = min(tb, seq)
    nt = seq // tb
    h, dk, dv = C_HEADS, C_DK, C_DV
    kw, vw = h * dk, h * dv
    v0 = 2 * kw // vw
    return pl.pallas_call(
        functools.partial(_gla_kernel, chunk=chunk, sub=sub),
        grid=(batch, nt),
        in_specs=[
            pl.BlockSpec((tb, kw), lambda b, t: (b * nt + t, 0)),
            pl.BlockSpec((tb, kw), lambda b, t: (b * nt + t, 1)),
            pl.BlockSpec((tb, vw), lambda b, t: (b * nt + t, v0)),
            pl.BlockSpec((tb, vw), lambda b, t: (b * nt + t, v0 + 1)),
            pl.BlockSpec((tb, LANES), lambda b, t: (b * nt + t, 0)),
            pl.BlockSpec((LANES, kw), lambda b, t: (0, 0)),
            pl.BlockSpec((1, kw), lambda b, t: (0, 0)),
            pl.BlockSpec((1, dv), lambda b, t: (0, 0)),
        ],
        out_specs=pl.BlockSpec((tb, vw), lambda b, t: (b * nt + t, 0)),
        out_shape=jax.ShapeDtypeStruct((m, vw), BF16),
        scratch_shapes=[pltpu.VMEM((h, dk, dv), F32)],
        compiler_params=_params("parallel", "arbitrary"),
        name="gla_recurrence",
    )(proj, proj, proj, proj, low, w_up, b_gate.reshape(1, kw), out_gain.reshape(1, dv))


F_PIECES = 3


def _fox_prep_kernel(q_ref, k_ref, logit_ref, bias_ref, qg_ref, kg_ref,
                     qo_ref, ko_ref, carry_ref, *, heads, head_dim):
    @pl.when(pl.program_id(1) == 0)
    def _():
        carry_ref[...] = jnp.zeros_like(carry_ref)

    tb = q_ref.shape[0]
    log_f = _log_sigmoid(logit_ref[...] + bias_ref[...])
    cum = _cumsum_rows(log_f, _tril_bf16(tb)) + carry_ref[0:1, :]
    carry_ref[...] = jnp.broadcast_to(cum[tb - 1:tb, :], carry_ref.shape)
    pieces = [p.astype(F32) for p in _split3(-LOG2E * cum)]
    lane = lax.broadcasted_iota(jnp.int32, (tb, head_dim), 1)
    one_cols = jnp.where(lane < F_PIECES, 1.0, 0.0).astype(qo_ref.dtype)

    qg = qg_ref[...] * (head_dim ** -0.5 * LOG2E)
    kg = kg_ref[...]
    for hh in range(heads):
        cols = slice(hh * head_dim, (hh + 1) * head_dim)
        lo = 2 * hh * head_dim
        main = slice(lo, lo + head_dim)
        extra = slice(lo + head_dim, lo + 2 * head_dim)
        q = q_ref[:, cols].astype(F32)
        qo_ref[:, main] = (q * lax.rsqrt(jnp.mean(q * q, axis=-1, keepdims=True) + EPS) * qg
                           ).astype(qo_ref.dtype)
        qo_ref[:, extra] = one_cols
        k = k_ref[:, cols].astype(F32)
        ko_ref[:, main] = (k * lax.rsqrt(jnp.mean(k * k, axis=-1, keepdims=True) + EPS) * kg
                           ).astype(ko_ref.dtype)
        bias_cols = jnp.zeros((tb, head_dim), F32)
        for j, piece in enumerate(pieces):
            bias_cols = jnp.where(lane == j, piece[:, hh:hh + 1], bias_cols)
        ko_ref[:, extra] = bias_cols.astype(ko_ref.dtype)


def fox_prep(proj, logits, f_bias, q_gain, k_gain, batch, seq, tb=PREP_TB):
    m = proj.shape[0]
    h, hd = B_HEADS, B_HEAD_DIM
    w = h * hd
    tb = min(tb, seq)
    nt = seq // tb
    base = (4 * A_HEADS * A_DK) // w
    bias = jnp.zeros((1, LANES), F32).at[0, :h].set(f_bias)

    def col(group):
        return pl.BlockSpec((tb, w), lambda b, t: (b * nt + t, base + group))

    wide_tok = pl.BlockSpec((tb, 2 * w), lambda b, t: (b * nt + t, 0))
    return pl.pallas_call(
        functools.partial(_fox_prep_kernel, heads=h, head_dim=hd),
        grid=(batch, nt),
        in_specs=[
            col(0), col(1),
            pl.BlockSpec((tb, LANES), lambda b, t: (b * nt + t, 0)),
            pl.BlockSpec((1, LANES), lambda b, t: (0, 0)),
            pl.BlockSpec((1, hd), lambda b, t: (0, 0)),
            pl.BlockSpec((1, hd), lambda b, t: (0, 0)),
        ],
        out_specs=[wide_tok, wide_tok],
        out_shape=[jax.ShapeDtypeStruct((m, 2 * w), BF16)] * 2,
        scratch_shapes=[pltpu.VMEM((8, LANES), F32)],
        compiler_params=_params("parallel", "arbitrary"),
        name="fox_prep",
    )(proj, proj, logits, bias, q_gain.reshape(1, hd), k_gain.reshape(1, hd))


def _fox_attn_kernel(q_ref, k_ref, v_ref, g_ref, o_ref, m_ref, acc_ref, *, tq, tk, heads, head_dim):
    qi = pl.program_id(2)
    q_start = pl.multiple_of(qi * tq, tq)
    wide = 2 * head_dim
    def tile(start, mask, width=tk):
        rows = pl.ds(start, width)
        ones = jnp.ones((width, head_dim), BF16)
        scores = [_dot_nt(q_ref[:, hh * wide:(hh + 1) * wide], k_ref[rows, hh * wide:(hh + 1) * wide])
                  for hh in range(heads)]
        for hh, s in enumerate(scores):
            if mask is not None:
                s = jnp.where(mask, s, MASK_VALUE)
            m_prev = m_ref[hh]
            m_new = jnp.maximum(m_prev, jnp.max(s, axis=-1, keepdims=True))
            p = jnp.exp2(s - jnp.concatenate([m_new] * (width // LANES), axis=1))
            alpha = jnp.exp2(m_prev - m_new)
            m_ref[hh] = m_new
            v_aug = jnp.concatenate([v_ref[rows, hh * head_dim:(hh + 1) * head_dim], ones], axis=1)
            acc_ref[hh] = (jnp.concatenate([alpha] * (wide // LANES), axis=1) * acc_ref[hh]
                           + _dot(p.astype(BF16), v_aug))

    m_ref[...] = jnp.full(m_ref.shape, MASK_VALUE, F32)
    acc_ref[...] = jnp.zeros_like(acc_ref)

    n_full = qi * (tq // tk)

    def quad(gi, carry):
        tile(pl.multiple_of(4 * gi * tk, 4 * tk), None, width=4 * tk)
        return carry

    lax.fori_loop(0, n_full // 4, quad, 0)

    @pl.when(n_full % 4 >= 2)
    def _():
        tile(pl.multiple_of((n_full // 4) * 4 * tk, 2 * tk), None, width=2 * tk)

    @pl.when(n_full % 2 == 1)
    def _():
        tile(pl.multiple_of((n_full - 1) * tk, tk), None)

    for d in range(tq // tk):
        row = lax.broadcasted_iota(jnp.int32, (tq, tk), 0)
        col = lax.broadcasted_iota(jnp.int32, (tq, tk), 1) + d * tk
        tile(pl.multiple_of(q_start + d * tk, tk), col <= row)
    for hh in range(heads):
        acc = acc_ref[hh]
        cols = slice(hh * head_dim, (hh + 1) * head_dim)
        o_ref[:, cols] = (acc[:, :head_dim] / acc[:, head_dim:]
                          * _sigmoid(g_ref[:, cols].astype(F32))).astype(o_ref.dtype)


def fox_attention(q_aug, k_aug, proj, batch, seq, tq=ATT_TQ, tk=ATT_TK, group=ATT_HEADS):
    m = q_aug.shape[0]
    h, hd = B_HEADS, B_HEAD_DIM
    tq = min(tq, seq)
    tk = min(tk, tq)
    nq = seq // tq
    v_block = (4 * A_HEADS * A_DK + 2 * h * hd) // (group * hd)
    gate_block = v_block + h // group
    return pl.pallas_call(
        functools.partial(_fox_attn_kernel, tq=tq, tk=tk, heads=group, head_dim=hd),
        grid=(batch, h // group, nq),
        in_specs=[
            pl.BlockSpec((tq, 2 * group * hd), lambda b, hg, i: (b * nq + i, hg)),
            pl.BlockSpec((seq, 2 * group * hd), lambda b, hg, i: (b, hg),
                         pipeline_mode=pl.Buffered(1)),
            pl.BlockSpec((seq, group * hd), lambda b, hg, i: (b, v_block + hg),
                         pipeline_mode=pl.Buffered(1)),
            pl.BlockSpec((tq, group * hd), lambda b, hg, i: (b * nq + i, gate_block + hg)),
        ],
        out_specs=pl.BlockSpec((tq, group * hd), lambda b, hg, i: (b * nq + i, hg)),
        out_shape=jax.ShapeDtypeStruct((m, h * hd), BF16),
        scratch_shapes=[pltpu.VMEM((group, tq, LANES), F32), pltpu.VMEM((group, tq, 2 * hd), F32)],
        compiler_params=_params("parallel", "parallel", "arbitrary"),
        name="fox_attention",
    )(q_aug, k_aug, proj, proj)


def _pad_cols(w, n):
    return jnp.zeros((w.shape[0], n), w.dtype).at[:, :w.shape[1]].set(w)


def kernel(x, c, mod_w, mod_b, norm_mix_gain, norm_ffn_gain, ab_w_in, ab_w_out, hgrn_lb_logits,
           hgrn_out_gain, fox_q_gain, fox_k_gain, fox_f_bias, gla_w_in, gla_w_gate_up, gla_b_gate,
           gla_out_gain, gla_w_out, ffn_w_in, ffn_w_out):
    batch, seq, d = x.shape
    depth = mod_w.shape[0]
    ab_main = 4 * A_HEADS * A_DK + 4 * B_HEADS * B_HEAD_DIM
    c_main = 2 * C_HEADS * C_DK + 2 * C_HEADS * C_DV

    lbs = lower_bounds(hgrn_lb_logits)
    mod = modulation(c, mod_w, mod_b)
    xs = x.reshape(batch * seq, d)
    ab_w_in_b, ab_w_out_b = ab_w_in.astype(BF16), ab_w_out.astype(BF16)
    gla_w_in_b, gla_w_out_b = gla_w_in.astype(BF16), gla_w_out.astype(BF16)
    ffn_w_in_b, ffn_w_out_b = ffn_w_in.astype(BF16), ffn_w_out.astype(BF16)

    for layer in range(depth):
        sh1, sc1, g1, sh2, sc2, g2 = [
            mod[layer, :, i * d:(i + 1) * d].reshape(batch, 1, d) for i in range(6)]
        if layer % 2 == 0:
            i = layer // 2
            proj, logits = norm_proj(xs, norm_mix_gain[layer], sh1, sc1, ab_w_in_b, i, ab_main,
                                     _pad_cols(ab_w_in[i, :, ab_main:], LANES).astype(BF16), seq)
            o_a = hgrn2_heads(proj, lbs[i], hgrn_out_gain[i], batch, seq)
            q_aug, k_aug = fox_prep(proj, logits, fox_f_bias[i], fox_q_gain[i], fox_k_gain[i],
                                    batch, seq)
            o_b = fox_attention(q_aug, k_aug, proj, batch, seq)
            xs = out_proj_residual([o_a, o_b], ab_w_out_b, i, xs, g1, seq)
        else:
            j = layer // 2
            proj, low = norm_proj(xs, norm_mix_gain[layer], sh1, sc1, gla_w_in_b, j, c_main,
                                  _pad_cols(gla_w_in[j, :, c_main:], LANES).astype(BF16), seq)
            w_up = jnp.zeros((LANES, gla_w_gate_up.shape[2]), BF16).at[:C_GATE_RANK].set(
                gla_w_gate_up[j].astype(BF16))
            o_c = gla_heads(proj, low, w_up, gla_b_gate[j], gla_out_gain[j], batch, seq)
            xs = out_proj_residual([o_c], gla_w_out_b, j, xs, g1, seq)
        xs = ffn_residual(xs, norm_ffn_gain[layer], sh2, sc2, g2, ffn_w_in_b, ffn_w_out_b, layer, seq)
    return xs.reshape(batch, seq, d)
```

```python
import functools

import jax
import jax.numpy as jnp
from jax import lax
from jax.experimental import pallas as pl
from jax.experimental.pallas import tpu as pltpu

F32 = jnp.float32
BF16 = jnp.bfloat16

EPS = 1e-6
MIN_FORGET = 1e-30
MASK_VALUE = -1e30
GLA_GATE_NORMALIZER = 16.0
LOG2E = 1.4426950408889634

LANES = 128
VMEM_LIMIT = 60 * 1024 * 1024

A_HEADS = 8
A_DK = 128
B_HEADS = 8
B_HEAD_DIM = 128
C_HEADS = 4
C_DK = 256
C_DV = 512
C_GATE_RANK = 16

PROJ_TM = 1024
PROJ_TN = 2048
PROJ_TC = 512
OUT_TM = 512
FFN_TM = 1024
FFN_TF = 512
FFN_TN = 512
NORM_ROWS = 16
NORM_UNROLL = 8
REC_TB = 512
REC_CHUNK = 64
HGRN_SUB = 16
GLA_CHUNK = 256
GLA_SUB = 64
ATT_TQ = 512
ATT_TK = 512
ATT_HEADS = 4
PREP_TB = 512


def _params(*sem):
    return pltpu.CompilerParams(dimension_semantics=sem, vmem_limit_bytes=VMEM_LIMIT)


def _sigmoid(x):
    return 1.0 / (1.0 + jnp.exp2(x * (-LOG2E)))


def _silu(x):
    return x * _sigmoid(x)


def _log_sigmoid(x):
    return jnp.minimum(x, 0.0) - jnp.log(1.0 + jnp.exp(-jnp.abs(x)))


def _split3(x):
    hi = x.astype(BF16)
    r = x - hi.astype(F32)
    mid = r.astype(BF16)
    lo = (r - mid.astype(F32)).astype(BF16)
    return hi, mid, lo


def _dot(a, b):
    return jnp.dot(a, b, preferred_element_type=F32)


def _dot_nt(a, b):
    return lax.dot_general(a, b, (((1,), (1,)), ((), ())), preferred_element_type=F32)


def _dot_tn(a, b):
    return lax.dot_general(a, b, (((0,), (0,)), ((), ())), preferred_element_type=F32)


def _lower_bound_kernel(logit_ref, o_ref):
    z = logit_ref[...]
    n = z.shape[0]
    e = jnp.exp(z - jnp.max(z, axis=0, keepdims=True))
    p = e / jnp.sum(e, axis=0, keepdims=True)
    run = jnp.zeros_like(p[0:1])
    rows = [run]
    for k in range(1, n):
        run = run + p[k:k + 1]
        rows.append(run)
    o_ref[...] = jnp.clip(jnp.concatenate(rows, axis=0), 0.0, 1.0 - 1e-6)


def lower_bounds(logits):
    return pl.pallas_call(
        _lower_bound_kernel,
        out_shape=jax.ShapeDtypeStruct(logits.shape, F32),
        name="hgrn_lower_bounds",
    )(logits.astype(F32))


def _mod_kernel(c_ref, w_ref, b_ref, o_ref):
    s = _silu(c_ref[...]).astype(BF16)
    o_ref[0] = _dot(s, w_ref[0].astype(BF16)) + b_ref[0]


def modulation(c, mod_w, mod_b, tn=2048):
    n_layers, d, n = mod_w.shape
    b = c.shape[0]
    rows = 8
    c_pad = jnp.zeros((rows, d), F32).at[:b].set(c)
    out = pl.pallas_call(
        _mod_kernel,
        grid=(n_layers, n // tn),
        in_specs=[
            pl.BlockSpec((rows, d), lambda l, j: (0, 0)),
            pl.BlockSpec((1, d, tn), lambda l, j: (l, 0, j)),
            pl.BlockSpec((1, 1, tn), lambda l, j: (l, 0, j)),
        ],
        out_specs=pl.BlockSpec((1, rows, tn), lambda l, j: (l, 0, j)),
        out_shape=jax.ShapeDtypeStruct((n_layers, rows, n), F32),
        compiler_params=_params("parallel", "parallel"),
        name="adaln_modulation",
    )(c_pad, mod_w, mod_b.reshape(n_layers, 1, n))
    return out[:, :b]


def _norm_rows(x_ref, gain_ref, sh_ref, sc_ref, h_ref, copy_ref=None):
    col_scale = gain_ref[...] * (1.0 + sc_ref[0])
    shift = sh_ref[0]

    def body(r, carry):
        rows = pl.ds(pl.multiple_of(r * NORM_ROWS, NORM_ROWS), NORM_ROWS)
        x = x_ref[rows, :]
        inv = lax.rsqrt(jnp.mean(x * x, axis=-1, keepdims=True) + EPS)
        h_ref[rows, :] = (x * inv * col_scale + shift).astype(h_ref.dtype)
        if copy_ref is not None:
            copy_ref[rows, :] = x
        return carry

    lax.fori_loop(0, x_ref.shape[0] // NORM_ROWS, body, 0, unroll=NORM_UNROLL)


def _norm_proj_kernel(x_ref, gain_ref, sh_ref, sc_ref, w_ref, waux_ref, o_ref, oaux_ref, h_ref):
    @pl.when(pl.program_id(1) == 0)
    def _():
        _norm_rows(x_ref, gain_ref, sh_ref, sc_ref, h_ref)
        oaux_ref[...] = _dot(h_ref[...], waux_ref[...])

    h = h_ref[...]
    for c0 in range(0, o_ref.shape[1], PROJ_TC):
        cols = slice(c0, c0 + PROJ_TC)
        o_ref[:, cols] = _dot(h, w_ref[0, :, cols]).astype(o_ref.dtype)


def norm_proj(x, gain, shift, scale, w_all, layer, n, w_aux, seq, tm=PROJ_TM, tn=PROJ_TN):
    m, d = x.shape
    tm = min(tm, seq)
    per_batch = seq // tm
    return pl.pallas_call(
        _norm_proj_kernel,
        grid=(m // tm, n // tn),
        in_specs=[
            pl.BlockSpec((tm, d), lambda i, j: (i, 0)),
            pl.BlockSpec((1, d), lambda i, j: (0, 0)),
            pl.BlockSpec((1, 1, d), lambda i, j: (i // per_batch, 0, 0)),
            pl.BlockSpec((1, 1, d), lambda i, j: (i // per_batch, 0, 0)),
            pl.BlockSpec((1, d, tn), lambda i, j: (layer, 0, j)),
            pl.BlockSpec((d, LANES), lambda i, j: (0, 0)),
        ],
        out_specs=[
            pl.BlockSpec((tm, tn), lambda i, j: (i, j)),
            pl.BlockSpec((tm, LANES), lambda i, j: (i, 0)),
        ],
        out_shape=[
            jax.ShapeDtypeStruct((m, n), BF16),
            jax.ShapeDtypeStruct((m, LANES), F32),
        ],
        scratch_shapes=[pltpu.VMEM((tm, d), BF16)],
        compiler_params=_params("parallel", "arbitrary"),
        name="norm_in_proj",
    )(x, gain.reshape(1, d), shift, scale, w_all, w_aux)


def _out_proj_kernel(*refs, n_lhs):
    lhs = refs[:n_lhs]
    w_ref, x_ref, gate_ref, o_ref = refs[n_lhs:]
    width = lhs[0].shape[1]
    y = _dot(lhs[0][...], w_ref[0, 0:width, :])
    for part, a_ref in enumerate(lhs[1:], 1):
        y = y + _dot(a_ref[...], w_ref[0, part * width:(part + 1) * width, :])
    o_ref[...] = x_ref[...] + gate_ref[0] * y


def out_proj_residual(lhs_list, w_all, layer, x, gate, seq, tm=OUT_TM):
    m, d = x.shape
    tm = min(tm, seq)
    per_batch = seq // tm
    n_lhs = len(lhs_list)
    width = lhs_list[0].shape[1]
    k = w_all.shape[1]
    assert all(a.shape[1] == width for a in lhs_list) and n_lhs * width == k
    in_specs = (
        [pl.BlockSpec((tm, width), lambda i: (i, 0)) for _ in lhs_list]
        + [pl.BlockSpec((1, k, d), lambda i: (layer, 0, 0), pipeline_mode=pl.Buffered(1)),
           pl.BlockSpec((tm, d), lambda i: (i, 0)),
           pl.BlockSpec((1, 1, d), lambda i: (i // per_batch, 0, 0))]
    )
    return pl.pallas_call(
        functools.partial(_out_proj_kernel, n_lhs=n_lhs),
        grid=(m // tm,),
        in_specs=in_specs,
        out_specs=pl.BlockSpec((tm, d), lambda i: (i, 0)),
        out_shape=jax.ShapeDtypeStruct((m, d), F32),
        compiler_params=_params("parallel"),
        name="out_proj_residual",
    )(*lhs_list, w_all, x, gate)


def _ffn_kernel(x_ref, gain_ref, sh_ref, sc_ref, gate_ref, wa_ref, wu_ref, wo_ref, o_ref, h_ref):
    @pl.when(pl.program_id(1) == 0)
    def _():
        _norm_rows(x_ref, gain_ref, sh_ref, sc_ref, h_ref, copy_ref=o_ref)

    h = h_ref[...]
    a = _dot(h, wa_ref[0])
    u = _dot(h, wu_ref[0])
    act = (_silu(a) * u).astype(BF16)
    d = o_ref.shape[1]
    for c0 in range(0, d, FFN_TN):
        cols = slice(c0, c0 + FFN_TN)
        o_ref[:, cols] += gate_ref[0][:, cols] * _dot(act, wo_ref[0, :, cols])


def ffn_residual(x, gain, shift, scale, gate, w_in, w_out, layer, seq, tm=FFN_TM, tf=FFN_TF):
    m, d = x.shape
    d_ff = w_out.shape[1]
    tm = min(tm, seq)
    per_batch = seq // tm
    nf = d_ff // tf
    vec = pl.BlockSpec((1, 1, d), lambda i, f: (i // per_batch, 0, 0))
    return pl.pallas_call(
        _ffn_kernel,
        grid=(m // tm, nf),
        in_specs=[
            pl.BlockSpec((tm, d), lambda i, f: (i, 0)),
            pl.BlockSpec((1, d), lambda i, f: (0, 0)),
            vec, vec, vec,
            pl.BlockSpec((1, d, tf), lambda i, f: (layer, 0, f)),
            pl.BlockSpec((1, d, tf), lambda i, f: (layer, 0, nf + f)),
            pl.BlockSpec((1, tf, d), lambda i, f: (layer, f, 0)),
        ],
        out_specs=pl.BlockSpec((tm, d), lambda i, f: (i, 0)),
        out_shape=jax.ShapeDtypeStruct((m, d), F32),
        scratch_shapes=[pltpu.VMEM((tm, d), BF16)],
        compiler_params=_params("parallel", "arbitrary"),
        name="swiglu_ffn_residual",
    )(x, gain.reshape(1, d), shift, scale, gate, w_in, w_in, w_out)


def _cumsum_rows(g, tril):
    n = g.shape[1]
    y = _dot(tril, jnp.concatenate(_split3(g), axis=1))
    return y[:, :n] + y[:, n:2 * n] + y[:, 2 * n:]


def _glr_heads(qs, ks, vs, cums, s_ref, sub):
    heads = range(len(qs))
    c, dk = qs[0].shape
    dv = vs[0].shape[1]
    vbs = [v.astype(BF16) for v in vs]
    lasts = [cum[c - 1:c, :] for cum in cums]

    inter = [_dot((qs[h] * jnp.exp2(cums[h])).astype(BF16), s_ref[h].astype(BF16)) for h in heads]
    updates = [_dot_tn((ks[h] * jnp.exp2(lasts[h] - cums[h])).astype(BF16), vbs[h]) for h in heads]
    for h in heads:
        decay_cols = jnp.transpose(jnp.broadcast_to(jnp.exp2(lasts[h]), (LANES, dk)))
        s_ref[h] = jnp.concatenate([decay_cols] * (dv // LANES), axis=1) * s_ref[h] + updates[h]

    spans = [(r0, r0 + sub) for r0 in range(0, c, sub)]
    scores = []
    for h in heads:
        q, k, cum = qs[h], ks[h], cums[h]
        per_head = []
        for r0, r1 in spans:
            hi = cum[r1 - 1:r1, :]
            mid = 0.5 * hi if r0 == 0 else 0.5 * (cum[r0 - 1:r0, :] + hi)
            qi = (q[r0:r1] * jnp.exp2(cum[r0:r1] - mid)).astype(BF16)
            ki = (k[:r1] * jnp.exp2(mid - cum[:r1])).astype(BF16)
            a = _dot_nt(qi, ki)
            row = r0 + lax.broadcasted_iota(jnp.int32, (sub, r1), 0)
            col = lax.broadcasted_iota(jnp.int32, (sub, r1), 1)
            per_head.append(jnp.where(col <= row, a, 0.0).astype(BF16))
        scores.append(per_head)
    outs = []
    for h in heads:
        blocks = [_dot(a, vbs[h][:r1]) for a, (_, r1) in zip(scores[h], spans)]
        intra = blocks[0] if len(blocks) == 1 else jnp.concatenate(blocks, axis=0)
        outs.append(inter[h] + intra)
    return outs


def _tril_bf16(c):
    row = lax.broadcasted_iota(jnp.int32, (c, c), 0)
    col = lax.broadcasted_iota(jnp.int32, (c, c), 1)
    return jnp.where(col <= row, 1.0, 0.0).astype(BF16)


def _hgrn_kernel(q_ref, f_ref, i_ref, g_ref, lb_ref, gain_ref, o_ref, s_ref, *, chunk, sub):
    @pl.when(pl.program_id(1) == 0)
    def _():
        s_ref[...] = jnp.zeros_like(s_ref)

    heads, dk, _ = s_ref.shape
    tril = _tril_bf16(chunk)

    def body(ci, carry):
        rows = pl.ds(pl.multiple_of(ci * chunk, chunk), chunk)
        lb = lb_ref[...]
        sig = _sigmoid(f_ref[rows, :].astype(F32))
        forget = lb + (1.0 - lb) * sig
        cum = _cumsum_rows(jnp.log2(jnp.maximum(forget, MIN_FORGET)), tril)
        key = (1.0 - lb) * (1.0 - sig)
        q = _silu(q_ref[rows, :].astype(F32))
        v = i_ref[rows, :]
        cols = [slice(hh * dk, (hh + 1) * dk) for hh in range(heads)]
        outs = _glr_heads([q[:, cs] for cs in cols], [key[:, cs] for cs in cols],
                          [v[:, cs] for cs in cols], [cum[:, cs] for cs in cols], s_ref, sub)
        for o, cs in zip(outs, cols):
            o = o * lax.rsqrt(jnp.mean(o * o, axis=-1, keepdims=True) + EPS) * gain_ref[:, cs]
            o_ref[rows, cs] = (o * _silu(g_ref[rows, cs].astype(F32))).astype(o_ref.dtype)
        return carry

    lax.fori_loop(0, q_ref.shape[0] // chunk, body, 0, unroll=2)


def hgrn2_heads(proj, lb, out_gain, batch, seq, tb=REC_TB, chunk=REC_CHUNK, sub=HGRN_SUB):
    m = proj.shape[0]
    tb = min(tb, seq)
    nt = seq // tb
    h, dk = A_HEADS, A_DK
    w = h * dk

    def col(group):
        return pl.BlockSpec((tb, w), lambda b, t: (b * nt + t, group))

    vec = pl.BlockSpec((1, w), lambda b, t: (0, 0))
    return pl.pallas_call(
        functools.partial(_hgrn_kernel, chunk=chunk, sub=sub),
        grid=(batch, nt),
        in_specs=[col(0), col(1), col(2), col(3), vec, vec],
        out_specs=pl.BlockSpec((tb, w), lambda b, t: (b * nt + t, 0)),
        out_shape=jax.ShapeDtypeStruct((m, w), BF16),
        scratch_shapes=[pltpu.VMEM((h, dk, dk), F32)],
        compiler_params=_params("parallel", "arbitrary"),
        name="hgrn2_recurrence",
    )(proj, proj, proj, proj, lb.reshape(1, w), out_gain.reshape(1, w))


def _gla_kernel(q_ref, k_ref, v_ref, g_ref, low_ref, wup_ref, bias_ref, gain_ref, o_ref, s_ref,
                *, chunk, sub):
    @pl.when(pl.program_id(1) == 0)
    def _():
        s_ref[...] = jnp.zeros_like(s_ref)

    heads, dk, dv = s_ref.shape
    tril = _tril_bf16(chunk)
    q_scale = dk ** -0.5

    def body(ci, carry):
        rows = pl.ds(pl.multiple_of(ci * chunk, chunk), chunk)
        z = _dot(low_ref[rows, :].astype(BF16), wup_ref[...]) + bias_ref[...]
        cum = _cumsum_rows(_log_sigmoid(z) * (LOG2E / GLA_GATE_NORMALIZER), tril)
        kcs = [slice(hh * dk, (hh + 1) * dk) for hh in range(heads)]
        vcs = [slice(hh * dv, (hh + 1) * dv) for hh in range(heads)]
        outs = _glr_heads([q_ref[rows, kc].astype(F32) * q_scale for kc in kcs],
                          [k_ref[rows, kc].astype(F32) for kc in kcs],
                          [v_ref[rows, vc] for vc in vcs], [cum[:, kc] for kc in kcs], s_ref, sub)
        for o, vc in zip(outs, vcs):
            o = o * lax.rsqrt(jnp.mean(o * o, axis=-1, keepdims=True) + EPS) * gain_ref[...]
            o_ref[rows, vc] = (o * _silu(g_ref[rows, vc].astype(F32))).astype(o_ref.dtype)
        return carry

    lax.fori_loop(0, q_ref.shape[0] // chunk, body, 0)


def gla_heads(proj, low, w_up, b_gate, out_gain, batch, seq, tb=REC_TB, chunk=GLA_CHUNK,
              sub=GLA_SUB):
    m = proj.shape[0]
    tb = min(tb, seq)
    nt = seq // tb
    h, dk, dv = C_HEADS, C_DK, C_DV
    kw, vw = h * dk, h * dv
    v0 = 2 * kw // vw
    return pl.pallas_call(
        functools.partial(_gla_kernel, chunk=chunk, sub=sub),
        grid=(batch, nt),
        in_specs=[
            pl.BlockSpec((tb, kw), lambda b, t: (b * nt + t, 0)),
            pl.BlockSpec((tb, kw), lambda b, t: (b * nt + t, 1)),
            pl.BlockSpec((tb, vw), lambda b, t: (b * nt + t, v0)),
            pl.BlockSpec((tb, vw), lambda b, t: (b * nt + t, v0 + 1)),
            pl.BlockSpec((tb, LANES), lambda b, t: (b * nt + t, 0)),
            pl.BlockSpec((LANES, kw), lambda b, t: (0, 0)),
            pl.BlockSpec((1, kw), lambda b, t: (0, 0)),
            pl.BlockSpec((1, dv), lambda b, t: (0, 0)),
        ],
        out_specs=pl.BlockSpec((tb, vw), lambda b, t: (b * nt + t, 0)),
        out_shape=jax.ShapeDtypeStruct((m, vw), BF16),
        scratch_shapes=[pltpu.VMEM((h, dk, dv), F32)],
        compiler_params=_params("parallel", "arbitrary"),
        name="gla_recurrence",
    )(proj, proj, proj, proj, low, w_up, b_gate.reshape(1, kw), out_gain.reshape(1, dv))


F_PIECES = 3


def _fox_prep_kernel(q_ref, k_ref, logit_ref, bias_ref, qg_ref, kg_ref,
                     qo_ref, ko_ref, carry_ref, *, heads, head_dim):
    @pl.when(pl.program_id(1) == 0)
    def _():
        carry_ref[...] = jnp.zeros_like(carry_ref)

    tb = q_ref.shape[0]
    log_f = _log_sigmoid(logit_ref[...] + bias_ref[...])
    cum = _cumsum_rows(log_f, _tril_bf16(tb)) + carry_ref[0:1, :]
    carry_ref[...] = jnp.broadcast_to(cum[tb - 1:tb, :], carry_ref.shape)
    pieces = jnp.concatenate(_split3(-LOG2E * cum), axis=1)
    src = lax.broadcasted_iota(jnp.int32, (F_PIECES * LANES, heads * head_dim), 0)
    dst = lax.broadcasted_iota(jnp.int32, (F_PIECES * LANES, heads * head_dim), 1)
    select = jnp.where((dst // head_dim == src % LANES) & (dst % head_dim == src // LANES), 1.0, 0.0)
    bias_all = _dot(pieces, select.astype(BF16))
    lane = lax.broadcasted_iota(jnp.int32, (tb, head_dim), 1)
    one_cols = jnp.where(lane < F_PIECES, 1.0, 0.0).astype(qo_ref.dtype)

    qg = qg_ref[...] * (head_dim ** -0.5 * LOG2E)
    kg = kg_ref[...]
    for hh in range(heads):
        cols = slice(hh * head_dim, (hh + 1) * head_dim)
        lo = 2 * hh * head_dim
        main = slice(lo, lo + head_dim)
        extra = slice(lo + head_dim, lo + 2 * head_dim)
        q = q_ref[:, cols].astype(F32)
        qo_ref[:, main] = (q * lax.rsqrt(jnp.mean(q * q, axis=-1, keepdims=True) + EPS) * qg
                           ).astype(qo_ref.dtype)
        qo_ref[:, extra] = one_cols
        k = k_ref[:, cols].astype(F32)
        ko_ref[:, main] = (k * lax.rsqrt(jnp.mean(k * k, axis=-1, keepdims=True) + EPS) * kg
                           ).astype(ko_ref.dtype)
        ko_ref[:, extra] = bias_all[:, cols].astype(ko_ref.dtype)


def fox_prep(proj, logits, f_bias, q_gain, k_gain, batch, seq, tb=PREP_TB):
    m = proj.shape[0]
    h, hd = B_HEADS, B_HEAD_DIM
    w = h * hd
    tb = min(tb, seq)
    nt = seq // tb
    base = (4 * A_HEADS * A_DK) // w
    bias = jnp.zeros((1, LANES), F32).at[0, :h].set(f_bias)

    def col(group):
        return pl.BlockSpec((tb, w), lambda b, t: (b * nt + t, base + group))

    wide_tok = pl.BlockSpec((tb, 2 * w), lambda b, t: (b * nt + t, 0))
    return pl.pallas_call(
        functools.partial(_fox_prep_kernel, heads=h, head_dim=hd),
        grid=(batch, nt),
        in_specs=[
            col(0), col(1),
            pl.BlockSpec((tb, LANES), lambda b, t: (b * nt + t, 0)),
            pl.BlockSpec((1, LANES), lambda b, t: (0, 0)),
            pl.BlockSpec((1, hd), lambda b, t: (0, 0)),
            pl.BlockSpec((1, hd), lambda b, t: (0, 0)),
        ],
        out_specs=[wide_tok, wide_tok],
        out_shape=[jax.ShapeDtypeStruct((m, 2 * w), BF16)] * 2,
        scratch_shapes=[pltpu.VMEM((8, LANES), F32)],
        compiler_params=_params("parallel", "arbitrary"),
        name="fox_prep",
    )(proj, proj, logits, bias, q_gain.reshape(1, hd), k_gain.reshape(1, hd))


def _fox_attn_kernel(q_ref, k_ref, v_ref, g_ref, o_ref, m_ref, acc_ref, *, tq, tk, heads, head_dim):
    qi = pl.program_id(2)
    q_start = pl.multiple_of(qi * tq, tq)
    wide = 2 * head_dim
    def tile(start, mask, width=tk):
        rows = pl.ds(start, width)
        ones = jnp.ones((width, head_dim), BF16)
        scores = [_dot_nt(q_ref[:, hh * wide:(hh + 1) * wide], k_ref[rows, hh * wide:(hh + 1) * wide])
                  for hh in range(heads)]
        for hh, s in enumerate(scores):
            if mask is not None:
                s = jnp.where(mask, s, MASK_VALUE)
            m_prev = m_ref[hh]
            m_new = jnp.maximum(m_prev, jnp.max(s, axis=-1, keepdims=True))
            p = jnp.exp2(s - jnp.concatenate([m_new] * (width // LANES), axis=1))
            alpha = jnp.exp2(m_prev - m_new)
            m_ref[hh] = m_new
            v_aug = jnp.concatenate([v_ref[rows, hh * head_dim:(hh + 1) * head_dim], ones], axis=1)
            acc_ref[hh] = (jnp.concatenate([alpha] * (wide // LANES), axis=1) * acc_ref[hh]
                           + _dot(p.astype(BF16), v_aug))

    m_ref[...] = jnp.full(m_ref.shape, MASK_VALUE, F32)
    acc_ref[...] = jnp.zeros_like(acc_ref)

    n_full = qi * (tq // tk)

    def quad(gi, carry):
        tile(pl.multiple_of(4 * gi * tk, 4 * tk), None, width=4 * tk)
        return carry

    lax.fori_loop(0, n_full // 4, quad, 0)

    @pl.when(n_full % 4 >= 2)
    def _():
        tile(pl.multiple_of((n_full // 4) * 4 * tk, 2 * tk), None, width=2 * tk)

    @pl.when(n_full % 2 == 1)
    def _():
        tile(pl.multiple_of((n_full - 1) * tk, tk), None)

    for d in range(tq // tk):
        row = lax.broadcasted_iota(jnp.int32, (tq, tk), 0)
        col = lax.broadcasted_iota(jnp.int32, (tq, tk), 1) + d * tk
        tile(pl.multiple_of(q_start + d * tk, tk), col <= row)
    for hh in range(heads):
        acc = acc_ref[hh]
        cols = slice(hh * head_dim, (hh + 1) * head_dim)
        o_ref[:, cols] = (acc[:, :head_dim] / acc[:, head_dim:]
                          * _sigmoid(g_ref[:, cols].astype(F32))).astype(o_ref.dtype)


def fox_attention(q_aug, k_aug, proj, batch, seq, tq=ATT_TQ, tk=ATT_TK, group=ATT_HEADS):
    m = q_aug.shape[0]
    h, hd = B_HEADS, B_HEAD_DIM
    tq = min(tq, seq)
    tk = min(tk, tq)
    nq = seq // tq
    v_block = (4 * A_HEADS * A_DK + 2 * h * hd) // (group * hd)
    gate_block = v_block + h // group
    return pl.pallas_call(
        functools.partial(_fox_attn_kernel, tq=tq, tk=tk, heads=group, head_dim=hd),
        grid=(batch, h // group, nq),
        in_specs=[
            pl.BlockSpec((tq, 2 * group * hd), lambda b, hg, i: (b * nq + i, hg)),
            pl.BlockSpec((seq, 2 * group * hd), lambda b, hg, i: (b, hg),
                         pipeline_mode=pl.Buffered(1)),
            pl.BlockSpec((seq, group * hd), lambda b, hg, i: (b, v_block + hg),
                         pipeline_mode=pl.Buffered(1)),
            pl.BlockSpec((tq, group * hd), lambda b, hg, i: (b * nq + i, gate_block + hg)),
        ],
        out_specs=pl.BlockSpec((tq, group * hd), lambda b, hg, i: (b * nq + i, hg)),
        out_shape=jax.ShapeDtypeStruct((m, h * hd), BF16),
        scratch_shapes=[pltpu.VMEM((group, tq, LANES), F32), pltpu.VMEM((group, tq, 2 * hd), F32)],
        compiler_params=_params("parallel", "parallel", "arbitrary"),
        name="fox_attention",
    )(q_aug, k_aug, proj, proj)


def _pad_cols(w, n):
    return jnp.zeros((w.shape[0], n), w.dtype).at[:, :w.shape[1]].set(w)


def kernel(x, c, mod_w, mod_b, norm_mix_gain, norm_ffn_gain, ab_w_in, ab_w_out, hgrn_lb_logits,
           hgrn_out_gain, fox_q_gain, fox_k_gain, fox_f_bias, gla_w_in, gla_w_gate_up, gla_b_gate,
           gla_out_gain, gla_w_out, ffn_w_in, ffn_w_out):
    batch, seq, d = x.shape
    depth = mod_w.shape[0]
    ab_main = 4 * A_HEADS * A_DK + 4 * B_HEADS * B_HEAD_DIM
    c_main = 2 * C_HEADS * C_DK + 2 * C_HEADS * C_DV

    lbs = lower_bounds(hgrn_lb_logits)
    mod = modulation(c, mod_w, mod_b)
    xs = x.reshape(batch * seq, d)
    ab_w_in_b, ab_w_out_b = ab_w_in.astype(BF16), ab_w_out.astype(BF16)
    gla_w_in_b, gla_w_out_b = gla_w_in.astype(BF16), gla_w_out.astype(BF16)
    ffn_w_in_b, ffn_w_out_b = ffn_w_in.astype(BF16), ffn_w_out.astype(BF16)

    for layer in range(depth):
        sh1, sc1, g1, sh2, sc2, g2 = [
            mod[layer, :, i * d:(i + 1) * d].reshape(batch, 1, d) for i in range(6)]
        if layer % 2 == 0:
            i = layer // 2
            proj, logits = norm_proj(xs, norm_mix_gain[layer], sh1, sc1, ab_w_in_b, i, ab_main,
                                     _pad_cols(ab_w_in[i, :, ab_main:], LANES).astype(BF16), seq)
            o_a = hgrn2_heads(proj, lbs[i], hgrn_out_gain[i], batch, seq)
            q_aug, k_aug = fox_prep(proj, logits, fox_f_bias[i], fox_q_gain[i], fox_k_gain[i],
                                    batch, seq)
            o_b = fox_attention(q_aug, k_aug, proj, batch, seq)
            xs = out_proj_residual([o_a, o_b], ab_w_out_b, i, xs, g1, seq)
        else:
            j = layer // 2
            proj, low = norm_proj(xs, norm_mix_gain[layer], sh1, sc1, gla_w_in_b, j, c_main,
                                  _pad_cols(gla_w_in[j, :, c_main:], LANES).astype(BF16), seq)
            w_up = jnp.zeros((LANES, gla_w_gate_up.shape[2]), BF16).at[:C_GATE_RANK].set(
                gla_w_gate_up[j].astype(BF16))
            o_c = gla_heads(proj, low, w_up, gla_b_gate[j], gla_out_gain[j], batch, seq)
            xs = out_proj_residual([o_c], gla_w_out_b, j, xs, g1, seq)
        xs = ffn_residual(xs, norm_ffn_gain[layer], sh2, sc2, g2, ffn_w_in_b, ffn_w_out_b, layer, seq)
    return xs.reshape(batch, seq, d)
```

```python
import functools

import jax
import jax.numpy as jnp
from jax import lax
from jax.experimental import pallas as pl
from jax.experimental.pallas import tpu as pltpu

F32 = jnp.float32
BF16 = jnp.bfloat16

EPS = 1e-6
MIN_FORGET = 1e-30
MASK_VALUE = -1e30
GLA_GATE_NORMALIZER = 16.0
LOG2E = 1.4426950408889634

LANES = 128
VMEM_LIMIT = 60 * 1024 * 1024

A_HEADS = 8
A_DK = 128
B_HEADS = 8
B_HEAD_DIM = 128
C_HEADS = 4
C_DK = 256
C_DV = 512
C_GATE_RANK = 16

PROJ_TM = 1024
PROJ_TN = 2048
PROJ_TC = 512
OUT_TM = 512
FFN_TM = 1024
FFN_TF = 512
FFN_TN = 512
NORM_ROWS = 16
NORM_UNROLL = 8
REC_TB = 512
REC_CHUNK = 64
HGRN_SUB = 16
GLA_CHUNK = 256
GLA_SUB = 64
ATT_TQ = 512
ATT_TK = 512
ATT_HEADS = 4
PREP_TB = 512


def _params(*sem):
    return pltpu.CompilerParams(dimension_semantics=sem, vmem_limit_bytes=VMEM_LIMIT)


def _sigmoid(x):
    return 1.0 / (1.0 + jnp.exp2(x * (-LOG2E)))


def _silu(x):
    return x * _sigmoid(x)


def _log_sigmoid(x):
    return jnp.minimum(x, 0.0) - jnp.log(1.0 + jnp.exp(-jnp.abs(x)))


def _split3(x):
    hi = x.astype(BF16)
    r = x - hi.astype(F32)
    mid = r.astype(BF16)
    lo = (r - mid.astype(F32)).astype(BF16)
    return hi, mid, lo


def _dot(a, b):
    return jnp.dot(a, b, preferred_element_type=F32)


def _dot_nt(a, b):
    return lax.dot_general(a, b, (((1,), (1,)), ((), ())), preferred_element_type=F32)


def _dot_tn(a, b):
    return lax.dot_general(a, b, (((0,), (0,)), ((), ())), preferred_element_type=F32)


def _lower_bound_kernel(logit_ref, o_ref):
    z = logit_ref[...]
    n = z.shape[0]
    e = jnp.exp(z - jnp.max(z, axis=0, keepdims=True))
    p = e / jnp.sum(e, axis=0, keepdims=True)
    run = jnp.zeros_like(p[0:1])
    rows = [run]
    for k in range(1, n):
        run = run + p[k:k + 1]
        rows.append(run)
    o_ref[...] = jnp.clip(jnp.concatenate(rows, axis=0), 0.0, 1.0 - 1e-6)


def lower_bounds(logits):
    return pl.pallas_call(
        _lower_bound_kernel,
        out_shape=jax.ShapeDtypeStruct(logits.shape, F32),
        name="hgrn_lower_bounds",
    )(logits.astype(F32))


def _mod_kernel(c_ref, w_ref, b_ref, o_ref):
    s = _silu(c_ref[...]).astype(BF16)
    o_ref[0] = _dot(s, w_ref[0].astype(BF16)) + b_ref[0]


def modulation(c, mod_w, mod_b, tn=2048):
    n_layers, d, n = mod_w.shape
    b = c.shape[0]
    rows = 8
    c_pad = jnp.zeros((rows, d), F32).at[:b].set(c)
    out = pl.pallas_call(
        _mod_kernel,
        grid=(n_layers, n // tn),
        in_specs=[
            pl.BlockSpec((rows, d), lambda l, j: (0, 0)),
            pl.BlockSpec((1, d, tn), lambda l, j: (l, 0, j)),
            pl.BlockSpec((1, 1, tn), lambda l, j: (l, 0, j)),
        ],
        out_specs=pl.BlockSpec((1, rows, tn), lambda l, j: (l, 0, j)),
        out_shape=jax.ShapeDtypeStruct((n_layers, rows, n), F32),
        compiler_params=_params("parallel", "parallel"),
        name="adaln_modulation",
    )(c_pad, mod_w, mod_b.reshape(n_layers, 1, n))
    return out[:, :b]


def _norm_rows(x_ref, gain_ref, sh_ref, sc_ref, h_ref, copy_ref=None):
    col_scale = gain_ref[...] * (1.0 + sc_ref[0])
    shift = sh_ref[0]

    def body(r, carry):
        rows = pl.ds(pl.multiple_of(r * NORM_ROWS, NORM_ROWS), NORM_ROWS)
        x = x_ref[rows, :]
        inv = lax.rsqrt(jnp.mean(x * x, axis=-1, keepdims=True) + EPS)
        h_ref[rows, :] = (x * inv * col_scale + shift).astype(h_ref.dtype)
        if copy_ref is not None:
            copy_ref[rows, :] = x
        return carry

    lax.fori_loop(0, x_ref.shape[0] // NORM_ROWS, body, 0, unroll=NORM_UNROLL)


def _norm_proj_kernel(x_ref, gain_ref, sh_ref, sc_ref, w_ref, waux_ref, o_ref, oaux_ref, h_ref):
    @pl.when(pl.program_id(1) == 0)
    def _():
        _norm_rows(x_ref, gain_ref, sh_ref, sc_ref, h_ref)
        oaux_ref[...] = _dot(h_ref[...], waux_ref[...])

    h = h_ref[...]
    for c0 in range(0, o_ref.shape[1], PROJ_TC):
        cols = slice(c0, c0 + PROJ_TC)
        o_ref[:, cols] = _dot(h, w_ref[0, :, cols]).astype(o_ref.dtype)


def norm_proj(x, gain, shift, scale, w_all, layer, n, w_aux, seq, tm=PROJ_TM, tn=PROJ_TN):
    m, d = x.shape
    tm = min(tm, seq)
    per_batch = seq // tm
    return pl.pallas_call(
        _norm_proj_kernel,
        grid=(m // tm, n // tn),
        in_specs=[
            pl.BlockSpec((tm, d), lambda i, j: (i, 0)),
            pl.BlockSpec((1, d), lambda i, j: (0, 0)),
            pl.BlockSpec((1, 1, d), lambda i, j: (i // per_batch, 0, 0)),
            pl.BlockSpec((1, 1, d), lambda i, j: (i // per_batch, 0, 0)),
            pl.BlockSpec((1, d, tn), lambda i, j: (layer, 0, j)),
            pl.BlockSpec((d, LANES), lambda i, j: (0, 0)),
        ],
        out_specs=[
            pl.BlockSpec((tm, tn), lambda i, j: (i, j)),
            pl.BlockSpec((tm, LANES), lambda i, j: (i, 0)),
        ],
        out_shape=[
            jax.ShapeDtypeStruct((m, n), BF16),
            jax.ShapeDtypeStruct((m, LANES), F32),
        ],
        scratch_shapes=[pltpu.VMEM((tm, d), BF16)],
        compiler_params=_params("parallel", "arbitrary"),
        name="norm_in_proj",
    )(x, gain.reshape(1, d), shift, scale, w_all, w_aux)


def _out_proj_kernel(*refs, n_lhs):
    lhs = refs[:n_lhs]
    w_ref, x_ref, gate_ref, o_ref = refs[n_lhs:]
    width = lhs[0].shape[1]
    y = _dot(lhs[0][...], w_ref[0, 0:width, :])
    for part, a_ref in enumerate(lhs[1:], 1):
        y = y + _dot(a_ref[...], w_ref[0, part * width:(part + 1) * width, :])
    o_ref[...] = x_ref[...] + gate_ref[0] * y


def out_proj_residual(lhs_list, w_all, layer, x, gate, seq, tm=OUT_TM):
    m, d = x.shape
    tm = min(tm, seq)
    per_batch = seq // tm
    n_lhs = len(lhs_list)
    width = lhs_list[0].shape[1]
    k = w_all.shape[1]
    assert all(a.shape[1] == width for a in lhs_list) and n_lhs * width == k
    in_specs = (
        [pl.BlockSpec((tm, width), lambda i: (i, 0)) for _ in lhs_list]
        + [pl.BlockSpec((1, k, d), lambda i: (layer, 0, 0), pipeline_mode=pl.Buffered(1)),
           pl.BlockSpec((tm, d), lambda i: (i, 0)),
           pl.BlockSpec((1, 1, d), lambda i: (i // per_batch, 0, 0))]
    )
    return pl.pallas_call(
        functools.partial(_out_proj_kernel, n_lhs=n_lhs),
        grid=(m // tm,),
        in_specs=in_specs,
        out_specs=pl.BlockSpec((tm, d), lambda i: (i, 0)),
        out_shape=jax.ShapeDtypeStruct((m, d), F32),
        compiler_params=_params("parallel"),
        name="out_proj_residual",
    )(*lhs_list, w_all, x, gate)


def _ffn_kernel(x_ref, gain_ref, sh_ref, sc_ref, gate_ref, wa_ref, wu_ref, wo_ref, o_ref, h_ref):
    @pl.when(pl.program_id(1) == 0)
    def _():
        _norm_rows(x_ref, gain_ref, sh_ref, sc_ref, h_ref, copy_ref=o_ref)

    h = h_ref[...]
    a = _dot(h, wa_ref[0])
    u = _dot(h, wu_ref[0])
    act = (_silu(a) * u).astype(BF16)
    d = o_ref.shape[1]
    for c0 in range(0, d, FFN_TN):
        cols = slice(c0, c0 + FFN_TN)
        o_ref[:, cols] += gate_ref[0][:, cols] * _dot(act, wo_ref[0, :, cols])


def ffn_residual(x, gain, shift, scale, gate, w_in, w_out, layer, seq, tm=FFN_TM, tf=FFN_TF):
    m, d = x.shape
    d_ff = w_out.shape[1]
    tm = min(tm, seq)
    per_batch = seq // tm
    nf = d_ff // tf
    vec = pl.BlockSpec((1, 1, d), lambda i, f: (i // per_batch, 0, 0))
    return pl.pallas_call(
        _ffn_kernel,
        grid=(m // tm, nf),
        in_specs=[
            pl.BlockSpec((tm, d), lambda i, f: (i, 0)),
            pl.BlockSpec((1, d), lambda i, f: (0, 0)),
            vec, vec, vec,
            pl.BlockSpec((1, d, tf), lambda i, f: (layer, 0, f)),
            pl.BlockSpec((1, d, tf), lambda i, f: (layer, 0, nf + f)),
            pl.BlockSpec((1, tf, d), lambda i, f: (layer, f, 0)),
        ],
        out_specs=pl.BlockSpec((tm, d), lambda i, f: (i, 0)),
        out_shape=jax.ShapeDtypeStruct((m, d), F32),
        scratch_shapes=[pltpu.VMEM((tm, d), BF16)],
        compiler_params=_params("parallel", "arbitrary"),
        name="swiglu_ffn_residual",
    )(x, gain.reshape(1, d), shift, scale, gate, w_in, w_in, w_out)


def _cumsum_rows(g, tril):
    n = g.shape[1]
    y = _dot(tril, jnp.concatenate(_split3(g), axis=1))
    return y[:, :n] + y[:, n:2 * n] + y[:, 2 * n:]


def _glr_heads(qs, ks, vs, cums, s_ref, sub):
    heads = range(len(qs))
    c, dk = qs[0].shape
    dv = vs[0].shape[1]
    vbs = [v.astype(BF16) for v in vs]
    lasts = [cum[c - 1:c, :] for cum in cums]

    inter = [_dot((qs[h] * jnp.exp2(cums[h])).astype(BF16), s_ref[h].astype(BF16)) for h in heads]
    updates = [_dot_tn((ks[h] * jnp.exp2(lasts[h] - cums[h])).astype(BF16), vbs[h]) for h in heads]
    for h in heads:
        decay_cols = jnp.transpose(jnp.broadcast_to(jnp.exp2(lasts[h]), (LANES, dk)))
        s_ref[h] = jnp.concatenate([decay_cols] * (dv // LANES), axis=1) * s_ref[h] + updates[h]

    spans = [(r0, r0 + sub) for r0 in range(0, c, sub)]
    scores = []
    for h in heads:
        q, k, cum = qs[h], ks[h], cums[h]
        per_head = []
        for r0, r1 in spans:
            hi = cum[r1 - 1:r1, :]
            mid = 0.5 * hi if r0 == 0 else 0.5 * (cum[r0 - 1:r0, :] + hi)
            qi = (q[r0:r1] * jnp.exp2(cum[r0:r1] - mid)).astype(BF16)
            ki = (k[:r1] * jnp.exp2(mid - cum[:r1])).astype(BF16)
            a = _dot_nt(qi, ki)
            row = r0 + lax.broadcasted_iota(jnp.int32, (sub, r1), 0)
            col = lax.broadcasted_iota(jnp.int32, (sub, r1), 1)
            per_head.append(jnp.where(col <= row, a, 0.0).astype(BF16))
        scores.append(per_head)
    outs = []
    for h in heads:
        blocks = [_dot(a, vbs[h][:r1]) for a, (_, r1) in zip(scores[h], spans)]
        intra = blocks[0] if len(blocks) == 1 else jnp.concatenate(blocks, axis=0)
        outs.append(inter[h] + intra)
    return outs


def _tril_bf16(c):
    row = lax.broadcasted_iota(jnp.int32, (c, c), 0)
    col = lax.broadcasted_iota(jnp.int32, (c, c), 1)
    return jnp.where(col <= row, 1.0, 0.0).astype(BF16)


def _hgrn_kernel(q_ref, f_ref, i_ref, g_ref, lb_ref, gain_ref, o_ref, s_ref, *, chunk, sub):
    @pl.when(pl.program_id(1) == 0)
    def _():
        s_ref[...] = jnp.zeros_like(s_ref)

    heads, dk, _ = s_ref.shape
    tril = _tril_bf16(chunk)

    def body(ci, carry):
        rows = pl.ds(pl.multiple_of(ci * chunk, chunk), chunk)
        lb = lb_ref[...]
        sig = _sigmoid(f_ref[rows, :].astype(F32))
        forget = lb + (1.0 - lb) * sig
        cum = _cumsum_rows(jnp.log2(jnp.maximum(forget, MIN_FORGET)), tril)
        key = (1.0 - lb) * (1.0 - sig)
        q = _silu(q_ref[rows, :].astype(F32))
        v = i_ref[rows, :]
        cols = [slice(hh * dk, (hh + 1) * dk) for hh in range(heads)]
        outs = _glr_heads([q[:, cs] for cs in cols], [key[:, cs] for cs in cols],
                          [v[:, cs] for cs in cols], [cum[:, cs] for cs in cols], s_ref, sub)
        for o, cs in zip(outs, cols):
            o = o * lax.rsqrt(jnp.mean(o * o, axis=-1, keepdims=True) + EPS) * gain_ref[:, cs]
            o_ref[rows, cs] = (o * _silu(g_ref[rows, cs].astype(F32))).astype(o_ref.dtype)
        return carry

    lax.fori_loop(0, q_ref.shape[0] // chunk, body, 0, unroll=4)


def hgrn2_heads(proj, lb, out_gain, batch, seq, tb=REC_TB, chunk=REC_CHUNK, sub=HGRN_SUB):
    m = proj.shape[0]
    tb = min(tb, seq)
    nt = seq // tb
    h, dk = A_HEADS, A_DK
    w = h * dk

    def col(group):
        return pl.BlockSpec((tb, w), lambda b, t: (b * nt + t, group))

    vec = pl.BlockSpec((1, w), lambda b, t: (0, 0))
    return pl.pallas_call(
        functools.partial(_hgrn_kernel, chunk=chunk, sub=sub),
        grid=(batch, nt),
        in_specs=[col(0), col(1), col(2), col(3), vec, vec],
        out_specs=pl.BlockSpec((tb, w), lambda b, t: (b * nt + t, 0)),
        out_shape=jax.ShapeDtypeStruct((m, w), BF16),
        scratch_shapes=[pltpu.VMEM((h, dk, dk), F32)],
        compiler_params=_params("parallel", "arbitrary"),
        name="hgrn2_recurrence",
    )(proj, proj, proj, proj, lb.reshape(1, w), out_gain.reshape(1, w))


def _gla_kernel(q_ref, k_ref, v_ref, g_ref, low_ref, wup_ref, bias_ref, gain_ref, o_ref, s_ref,
                *, chunk, sub):
    @pl.when(pl.program_id(1) == 0)
    def _():
        s_ref[...] = jnp.zeros_like(s_ref)

    heads, dk, dv = s_ref.shape
    tril = _tril_bf16(chunk)
    q_scale = dk ** -0.5

    def body(ci, carry):
        rows = pl.ds(pl.multiple_of(ci * chunk, chunk), chunk)
        z = _dot(low_ref[rows, :].astype(BF16), wup_ref[...]) + bias_ref[...]
        cum = _cumsum_rows(_log_sigmoid(z) * (LOG2E / GLA_GATE_NORMALIZER), tril)
        kcs = [slice(hh * dk, (hh + 1) * dk) for hh in range(heads)]
        vcs = [slice(hh * dv, (hh + 1) * dv) for hh in range(heads)]
        outs = _glr_heads([q_ref[rows, kc].astype(F32) * q_scale for kc in kcs],
                          [k_ref[rows, kc].astype(F32) for kc in kcs],
                          [v_ref[rows, vc] for vc in vcs], [cum[:, kc] for kc in kcs], s_ref, sub)
        for o, vc in zip(outs, vcs):
            o = o * lax.rsqrt(jnp.mean(o * o, axis=-1, keepdims=True) + EPS) * gain_ref[...]
            o_ref[rows, vc] = (o * _silu(g_ref[rows, vc].astype(F32))).astype(o_ref.dtype)
        return carry

    lax.fori_loop(0, q_ref.shape[0] // chunk, body, 0)


def gla_heads(proj, low, w_up, b_gate, out_gain, batch, seq, tb=REC_TB, chunk=GLA_CHUNK,
              sub=GLA_SUB):
    m = proj.shape[0]
    tb = min(tb, seq)
    nt = seq // tb
    h, dk, dv = C_HEADS, C_DK, C_DV
    kw, vw = h * dk, h * dv
    v0 = 2 * kw // vw
    return pl.pallas_call(
        functools.partial(_gla_kernel, chunk=chunk, sub=sub),
        grid=(batch, nt),
        in_specs=[
            pl.BlockSpec((tb, kw), lambda b, t: (b * nt + t, 0)),
            pl.BlockSpec((tb, kw), lambda b, t: (b * nt + t, 1)),
            pl.BlockSpec((tb, vw), lambda b, t: (b * nt + t, v0)),
            pl.BlockSpec((tb, vw), lambda b, t: (b * nt + t, v0 + 1)),
            pl.BlockSpec((tb, LANES), lambda b, t: (b * nt + t, 0)),
            pl.BlockSpec((LANES, kw), lambda b, t: (0, 0)),
            pl.BlockSpec((1, kw), lambda b, t: (0, 0)),
            pl.BlockSpec((1, dv), lambda b, t: (0, 0)),
        ],
        out_specs=pl.BlockSpec((tb, vw), lambda b, t: (b * nt + t, 0)),
        out_shape=jax.ShapeDtypeStruct((m, vw), BF16),
        scratch_shapes=[pltpu.VMEM((h, dk, dv), F32)],
        compiler_params=_params("parallel", "arbitrary"),
        name="gla_recurrence",
    )(proj, proj, proj, proj, low, w_up, b_gate.reshape(1, kw), out_gain.reshape(1, dv))


F_PIECES = 3


def _fox_prep_kernel(q_ref, k_ref, logit_ref, bias_ref, qg_ref, kg_ref,
                     qo_ref, ko_ref, carry_ref, *, heads, head_dim):
    @pl.when(pl.program_id(1) == 0)
    def _():
        carry_ref[...] = jnp.zeros_like(carry_ref)

    tb = q_ref.shape[0]
    log_f = _log_sigmoid(logit_ref[...] + bias_ref[...])
    cum = _cumsum_rows(log_f, _tril_bf16(tb)) + carry_ref[0:1, :]
    carry_ref[...] = jnp.broadcast_to(cum[tb - 1:tb, :], carry_ref.shape)
    pieces = jnp.concatenate(_split3(-LOG2E * cum), axis=1)
    src = lax.broadcasted_iota(jnp.int32, (F_PIECES * LANES, heads * head_dim), 0)
    dst = lax.broadcasted_iota(jnp.int32, (F_PIECES * LANES, heads * head_dim), 1)
    select = jnp.where((dst // head_dim == src % LANES) & (dst % head_dim == src // LANES), 1.0, 0.0)
    bias_all = _dot(pieces, select.astype(BF16))
    lane = lax.broadcasted_iota(jnp.int32, (tb, head_dim), 1)
    one_cols = jnp.where(lane < F_PIECES, 1.0, 0.0).astype(qo_ref.dtype)

    qg = qg_ref[...] * (head_dim ** -0.5 * LOG2E)
    kg = kg_ref[...]
    for hh in range(heads):
        cols = slice(hh * head_dim, (hh + 1) * head_dim)
        lo = 2 * hh * head_dim
        main = slice(lo, lo + head_dim)
        extra = slice(lo + head_dim, lo + 2 * head_dim)
        q = q_ref[:, cols].astype(F32)
        qo_ref[:, main] = (q * lax.rsqrt(jnp.mean(q * q, axis=-1, keepdims=True) + EPS) * qg
                           ).astype(qo_ref.dtype)
        qo_ref[:, extra] = one_cols
        k = k_ref[:, cols].astype(F32)
        ko_ref[:, main] = (k * lax.rsqrt(jnp.mean(k * k, axis=-1, keepdims=True) + EPS) * kg
                           ).astype(ko_ref.dtype)
        ko_ref[:, extra] = bias_all[:, cols].astype(ko_ref.dtype)


def fox_prep(proj, logits, f_bias, q_gain, k_gain, batch, seq, tb=PREP_TB):
    m = proj.shape[0]
    h, hd = B_HEADS, B_HEAD_DIM
    w = h * hd
    tb = min(tb, seq)
    nt = seq // tb
    base = (4 * A_HEADS * A_DK) // w
    bias = jnp.zeros((1, LANES), F32).at[0, :h].set(f_bias)

    def col(group):
        return pl.BlockSpec((tb, w), lambda b, t: (b * nt + t, base + group))

    wide_tok = pl.BlockSpec((tb, 2 * w), lambda b, t: (b * nt + t, 0))
    return pl.pallas_call(
        functools.partial(_fox_prep_kernel, heads=h, head_dim=hd),
        grid=(batch, nt),
        in_specs=[
            col(0), col(1),
            pl.BlockSpec((tb, LANES), lambda b, t: (b * nt + t, 0)),
            pl.BlockSpec((1, LANES), lambda b, t: (0, 0)),
            pl.BlockSpec((1, hd), lambda b, t: (0, 0)),
            pl.BlockSpec((1, hd), lambda b, t: (0, 0)),
        ],
        out_specs=[wide_tok, wide_tok],
        out_shape=[jax.ShapeDtypeStruct((m, 2 * w), BF16)] * 2,
        scratch_shapes=[pltpu.VMEM((8, LANES), F32)],
        compiler_params=_params("parallel", "arbitrary"),
        name="fox_prep",
    )(proj, proj, logits, bias, q_gain.reshape(1, hd), k_gain.reshape(1, hd))


def _fox_attn_kernel(q_ref, k_ref, v_ref, g_ref, o_ref, m_ref, acc_ref, *, tq, tk, heads, head_dim):
    qi = pl.program_id(2)
    q_start = pl.multiple_of(qi * tq, tq)
    wide = 2 * head_dim

    def tile(start, mask, width=tk, first=False):
        rows = pl.ds(start, width)
        ones = jnp.ones((width, head_dim), BF16)
        scores = [_dot_nt(q_ref[:, hh * wide:(hh + 1) * wide], k_ref[rows, hh * wide:(hh + 1) * wide])
                  for hh in range(heads)]
        for hh, s in enumerate(scores):
            if mask is not None:
                s = jnp.where(mask, s, MASK_VALUE)
            row_max = jnp.max(s, axis=-1, keepdims=True)
            if first:
                m_new = jnp.broadcast_to(row_max, (tq, LANES))
            else:
                m_prev = m_ref[hh]
                m_new = jnp.maximum(m_prev, row_max)
            p = jnp.exp2(s - jnp.concatenate([m_new] * (width // LANES), axis=1))
            m_ref[hh] = m_new
            v_aug = jnp.concatenate([v_ref[rows, hh * head_dim:(hh + 1) * head_dim], ones], axis=1)
            pv = _dot(p.astype(BF16), v_aug)
            if first:
                acc_ref[hh] = pv
            else:
                alpha = jnp.exp2(m_prev - m_new)
                acc_ref[hh] = jnp.concatenate([alpha] * (wide // LANES), axis=1) * acc_ref[hh] + pv

    for d in range(tq // tk):
        row = lax.broadcasted_iota(jnp.int32, (tq, tk), 0)
        col = lax.broadcasted_iota(jnp.int32, (tq, tk), 1) + d * tk
        tile(pl.multiple_of(q_start + d * tk, tk), col <= row, first=(d == 0))

    n_full = qi * (tq // tk)

    def quad(gi, carry):
        tile(pl.multiple_of(4 * gi * tk, 4 * tk), None, width=4 * tk)
        return carry

    lax.fori_loop(0, n_full // 4, quad, 0)

    @pl.when(n_full % 4 >= 2)
    def _():
        tile(pl.multiple_of((n_full // 4) * 4 * tk, 2 * tk), None, width=2 * tk)

    @pl.when(n_full % 2 == 1)
    def _():
        tile(pl.multiple_of((n_full - 1) * tk, tk), None)

    for hh in range(heads):
        acc = acc_ref[hh]
        cols = slice(hh * head_dim, (hh + 1) * head_dim)
        o_ref[:, cols] = (acc[:, :head_dim] / acc[:, head_dim:]
                          * _sigmoid(g_ref[:, cols].astype(F32))).astype(o_ref.dtype)


def fox_attention(q_aug, k_aug, proj, batch, seq, tq=ATT_TQ, tk=ATT_TK, group=ATT_HEADS):
    m = q_aug.shape[0]
    h, hd = B_HEADS, B_HEAD_DIM
    tq = min(tq, seq)
    tk = min(tk, tq)
    nq = seq // tq
    v_block = (4 * A_HEADS * A_DK + 2 * h * hd) // (group * hd)
    gate_block = v_block + h // group
    return pl.pallas_call(
        functools.partial(_fox_attn_kernel, tq=tq, tk=tk, heads=group, head_dim=hd),
        grid=(batch, h // group, nq),
        in_specs=[
            pl.BlockSpec((tq, 2 * group * hd), lambda b, hg, i: (b * nq + i, hg)),
            pl.BlockSpec((seq, 2 * group * hd), lambda b, hg, i: (b, hg),
                         pipeline_mode=pl.Buffered(1)),
            pl.BlockSpec((seq, group * hd), lambda b, hg, i: (b, v_block + hg),
                         pipeline_mode=pl.Buffered(1)),
            pl.BlockSpec((tq, group * hd), lambda b, hg, i: (b * nq + i, gate_block + hg)),
        ],
        out_specs=pl.BlockSpec((tq, group * hd), lambda b, hg, i: (b * nq + i, hg)),
        out_shape=jax.ShapeDtypeStruct((m, h * hd), BF16),
        scratch_shapes=[pltpu.VMEM((group, tq, LANES), F32), pltpu.VMEM((group, tq, 2 * hd), F32)],
        compiler_params=_params("parallel", "parallel", "arbitrary"),
        name="fox_attention",
    )(q_aug, k_aug, proj, proj)


def _pad_cols(w, n):
    return jnp.zeros((w.shape[0], n), w.dtype).at[:, :w.shape[1]].set(w)


def kernel(x, c, mod_w, mod_b, norm_mix_gain, norm_ffn_gain, ab_w_in, ab_w_out, hgrn_lb_logits,
           hgrn_out_gain, fox_q_gain, fox_k_gain, fox_f_bias, gla_w_in, gla_w_gate_up, gla_b_gate,
           gla_out_gain, gla_w_out, ffn_w_in, ffn_w_out):
    batch, seq, d = x.shape
    depth = mod_w.shape[0]
    ab_main = 4 * A_HEADS * A_DK + 4 * B_HEADS * B_HEAD_DIM
    c_main = 2 * C_HEADS * C_DK + 2 * C_HEADS * C_DV

    lbs = lower_bounds(hgrn_lb_logits)
    mod = modulation(c, mod_w, mod_b)
    xs = x.reshape(batch * seq, d)
    ab_w_in_b, ab_w_out_b = ab_w_in.astype(BF16), ab_w_out.astype(BF16)
    gla_w_in_b, gla_w_out_b = gla_w_in.astype(BF16), gla_w_out.astype(BF16)
    ffn_w_in_b, ffn_w_out_b = ffn_w_in.astype(BF16), ffn_w_out.astype(BF16)

    for layer in range(depth):
        sh1, sc1, g1, sh2, sc2, g2 = [
            mod[layer, :, i * d:(i + 1) * d].reshape(batch, 1, d) for i in range(6)]
        if layer % 2 == 0:
            i = layer // 2
            proj, logits = norm_proj(xs, norm_mix_gain[layer], sh1, sc1, ab_w_in_b, i, ab_main,
                                     _pad_cols(ab_w_in[i, :, ab_main:], LANES).astype(BF16), seq)
            o_a = hgrn2_heads(proj, lbs[i], hgrn_out_gain[i], batch, seq)
            q_aug, k_aug = fox_prep(proj, logits, fox_f_bias[i], fox_q_gain[i], fox_k_gain[i],
                                    batch, seq)
            o_b = fox_attention(q_aug, k_aug, proj, batch, seq)
            xs = out_proj_residual([o_a, o_b], ab_w_out_b, i, xs, g1, seq)
        else:
            j = layer // 2
            proj, low = norm_proj(xs, norm_mix_gain[layer], sh1, sc1, gla_w_in_b, j, c_main,
                                  _pad_cols(gla_w_in[j, :, c_main:], LANES).astype(BF16), seq)
            w_up = jnp.zeros((LANES, gla_w_gate_up.shape[2]), BF16).at[:C_GATE_RANK].set(
                gla_w_gate_up[j].astype(BF16))
            o_c = gla_heads(proj, low, w_up, gla_b_gate[j], gla_out_gain[j], batch, seq)
            xs = out_proj_residual([o_c], gla_w_out_b, j, xs, g1, seq)
        xs = ffn_residual(xs, norm_ffn_gain[layer], sh2, sc2, g2, ffn_w_in_b, ffn_w_out_b, layer, seq)
    return xs.reshape(batch, seq, d)
```

```python
import functools

import jax
import jax.numpy as jnp
from jax import lax
from jax.experimental import pallas as pl
from jax.experimental.pallas import tpu as pltpu

F32 = jnp.float32
BF16 = jnp.bfloat16

EPS = 1e-6
MIN_FORGET = 1e-30
MASK_VALUE = -1e30
GLA_GATE_NORMALIZER = 16.0
LOG2E = 1.4426950408889634

LANES = 128
VMEM_LIMIT = 60 * 1024 * 1024

A_HEADS = 8
A_DK = 128
B_HEADS = 8
B_HEAD_DIM = 128
C_HEADS = 4
C_DK = 256
C_DV = 512
C_GATE_RANK = 16

PROJ_TM = 1024
PROJ_TN = 2048
PROJ_TC = 512
OUT_TM = 512
FFN_TM = 1024
FFN_TF = 512
FFN_TN = 512
NORM_ROWS = 16
NORM_UNROLL = 8
REC_TB = 512
REC_CHUNK = 64
HGRN_SUB = 16
GLA_CHUNK = 256
GLA_SUB = 64
ATT_TQ = 512
ATT_TK = 512
ATT_HEADS = 4
PREP_TB = 512


def _params(*sem):
    return pltpu.CompilerParams(dimension_semantics=sem, vmem_limit_bytes=VMEM_LIMIT)


def _sigmoid(x):
    return 1.0 / (1.0 + jnp.exp2(x * (-LOG2E)))


def _silu(x):
    return x * _sigmoid(x)


def _log_sigmoid(x):
    return jnp.minimum(x, 0.0) - jnp.log(1.0 + jnp.exp(-jnp.abs(x)))


def _split3(x):
    hi = x.astype(BF16)
    r = x - hi.astype(F32)
    mid = r.astype(BF16)
    lo = (r - mid.astype(F32)).astype(BF16)
    return hi, mid, lo


def _dot(a, b):
    return jnp.dot(a, b, preferred_element_type=F32)


def _dot_nt(a, b):
    return lax.dot_general(a, b, (((1,), (1,)), ((), ())), preferred_element_type=F32)


def _dot_tn(a, b):
    return lax.dot_general(a, b, (((0,), (0,)), ((), ())), preferred_element_type=F32)


def _lower_bound_kernel(logit_ref, o_ref):
    z = logit_ref[...]
    n = z.shape[0]
    e = jnp.exp(z - jnp.max(z, axis=0, keepdims=True))
    p = e / jnp.sum(e, axis=0, keepdims=True)
    run = jnp.zeros_like(p[0:1])
    rows = [run]
    for k in range(1, n):
        run = run + p[k:k + 1]
        rows.append(run)
    o_ref[...] = jnp.clip(jnp.concatenate(rows, axis=0), 0.0, 1.0 - 1e-6)


def lower_bounds(logits):
    return pl.pallas_call(
        _lower_bound_kernel,
        out_shape=jax.ShapeDtypeStruct(logits.shape, F32),
        name="hgrn_lower_bounds",
    )(logits.astype(F32))


def _mod_kernel(c_ref, w_ref, b_ref, o_ref):
    s = _silu(c_ref[...]).astype(BF16)
    o_ref[0] = _dot(s, w_ref[0].astype(BF16)) + b_ref[0]


def modulation(c, mod_w, mod_b, tn=2048):
    n_layers, d, n = mod_w.shape
    b = c.shape[0]
    rows = 8
    c_pad = jnp.zeros((rows, d), F32).at[:b].set(c)
    out = pl.pallas_call(
        _mod_kernel,
        grid=(n_layers, n // tn),
        in_specs=[
            pl.BlockSpec((rows, d), lambda l, j: (0, 0)),
            pl.BlockSpec((1, d, tn), lambda l, j: (l, 0, j)),
            pl.BlockSpec((1, 1, tn), lambda l, j: (l, 0, j)),
        ],
        out_specs=pl.BlockSpec((1, rows, tn), lambda l, j: (l, 0, j)),
        out_shape=jax.ShapeDtypeStruct((n_layers, rows, n), F32),
        compiler_params=_params("parallel", "parallel"),
        name="adaln_modulation",
    )(c_pad, mod_w, mod_b.reshape(n_layers, 1, n))
    return out[:, :b]


def _norm_rows(x_ref, gain_ref, sh_ref, sc_ref, h_ref, copy_ref=None):
    col_scale = gain_ref[...] * (1.0 + sc_ref[0])
    shift = sh_ref[0]

    def body(r, carry):
        rows = pl.ds(pl.multiple_of(r * NORM_ROWS, NORM_ROWS), NORM_ROWS)
        x = x_ref[rows, :]
        inv = lax.rsqrt(jnp.mean(x * x, axis=-1, keepdims=True) + EPS)
        h_ref[rows, :] = (x * inv * col_scale + shift).astype(h_ref.dtype)
        if copy_ref is not None:
            copy_ref[rows, :] = x
        return carry

    lax.fori_loop(0, x_ref.shape[0] // NORM_ROWS, body, 0, unroll=NORM_UNROLL)


def _norm_proj_kernel(x_ref, gain_ref, sh_ref, sc_ref, w_ref, waux_ref, o_ref, oaux_ref, h_ref):
    @pl.when(pl.program_id(1) == 0)
    def _():
        _norm_rows(x_ref, gain_ref, sh_ref, sc_ref, h_ref)
        oaux_ref[...] = _dot(h_ref[...], waux_ref[...])

    h = h_ref[...]
    for c0 in range(0, o_ref.shape[1], PROJ_TC):
        cols = slice(c0, c0 + PROJ_TC)
        o_ref[:, cols] = _dot(h, w_ref[0, :, cols]).astype(o_ref.dtype)


def norm_proj(x, gain, shift, scale, w_all, layer, n, w_aux, seq, tm=PROJ_TM, tn=PROJ_TN):
    m, d = x.shape
    tm = min(tm, seq)
    per_batch = seq // tm
    return pl.pallas_call(
        _norm_proj_kernel,
        grid=(m // tm, n // tn),
        in_specs=[
            pl.BlockSpec((tm, d), lambda i, j: (i, 0)),
            pl.BlockSpec((1, d), lambda i, j: (0, 0)),
            pl.BlockSpec((1, 1, d), lambda i, j: (i // per_batch, 0, 0)),
            pl.BlockSpec((1, 1, d), lambda i, j: (i // per_batch, 0, 0)),
            pl.BlockSpec((1, d, tn), lambda i, j: (layer, 0, j)),
            pl.BlockSpec((d, LANES), lambda i, j: (0, 0)),
        ],
        out_specs=[
            pl.BlockSpec((tm, tn), lambda i, j: (i, j)),
            pl.BlockSpec((tm, LANES), lambda i, j: (i, 0)),
        ],
        out_shape=[
            jax.ShapeDtypeStruct((m, n), BF16),
            jax.ShapeDtypeStruct((m, LANES), F32),
        ],
        scratch_shapes=[pltpu.VMEM((tm, d), BF16)],
        compiler_params=_params("parallel", "arbitrary"),
        name="norm_in_proj",
    )(x, gain.reshape(1, d), shift, scale, w_all, w_aux)


def _out_proj_kernel(*refs, n_lhs):
    lhs = refs[:n_lhs]
    w_ref, x_ref, gate_ref, o_ref = refs[n_lhs:]
    width = lhs[0].shape[1]
    y = _dot(lhs[0][...], w_ref[0, 0:width, :])
    for part, a_ref in enumerate(lhs[1:], 1):
        y = y + _dot(a_ref[...], w_ref[0, part * width:(part + 1) * width, :])
    o_ref[...] = x_ref[...] + gate_ref[0] * y


def out_proj_residual(lhs_list, w_all, layer, x, gate, seq, tm=OUT_TM):
    m, d = x.shape
    tm = min(tm, seq)
    per_batch = seq // tm
    n_lhs = len(lhs_list)
    width = lhs_list[0].shape[1]
    k = w_all.shape[1]
    assert all(a.shape[1] == width for a in lhs_list) and n_lhs * width == k
    in_specs = (
        [pl.BlockSpec((tm, width), lambda i: (i, 0)) for _ in lhs_list]
        + [pl.BlockSpec((1, k, d), lambda i: (layer, 0, 0), pipeline_mode=pl.Buffered(1)),
           pl.BlockSpec((tm, d), lambda i: (i, 0)),
           pl.BlockSpec((1, 1, d), lambda i: (i // per_batch, 0, 0))]
    )
    return pl.pallas_call(
        functools.partial(_out_proj_kernel, n_lhs=n_lhs),
        grid=(m // tm,),
        in_specs=in_specs,
        out_specs=pl.BlockSpec((tm, d), lambda i: (i, 0)),
        out_shape=jax.ShapeDtypeStruct((m, d), F32),
        compiler_params=_params("parallel"),
        name="out_proj_residual",
    )(*lhs_list, w_all, x, gate)


def _ffn_kernel(*refs, cast_next):
    if cast_next:
        (x_ref, gain_ref, sh_ref, sc_ref, gate_ref, wa_ref, wu_ref, wo_ref, nin_ref, nout_ref,
         o_ref, nin_o_ref, nout_o_ref, h_ref) = refs
    else:
        x_ref, gain_ref, sh_ref, sc_ref, gate_ref, wa_ref, wu_ref, wo_ref, o_ref, h_ref = refs

    @pl.when(pl.program_id(1) == 0)
    def _():
        _norm_rows(x_ref, gain_ref, sh_ref, sc_ref, h_ref, copy_ref=o_ref)

    h = h_ref[...]
    a = _dot(h, wa_ref[...])
    u = _dot(h, wu_ref[...])
    act = (_silu(a) * u).astype(BF16)
    d = o_ref.shape[1]
    for c0 in range(0, d, FFN_TN):
        cols = slice(c0, c0 + FFN_TN)
        o_ref[:, cols] += gate_ref[0][:, cols] * _dot(act, wo_ref[:, cols])
    if cast_next:
        nin_o_ref[...] = nin_ref[0].astype(nin_o_ref.dtype)
        nout_o_ref[...] = nout_ref[0].astype(nout_o_ref.dtype)


def ffn_residual(x, gain, shift, scale, gate, w_in, w_out, seq, next_weights=None,
                 tm=FFN_TM, tf=FFN_TF):
    m, d = x.shape
    d_ff = w_out.shape[0]
    tm = min(tm, seq)
    per_batch = seq // tm
    ni, nf = m // tm, d_ff // tf
    vec = pl.BlockSpec((1, 1, d), lambda i, f: (i // per_batch, 0, 0))
    in_specs = [
        pl.BlockSpec((tm, d), lambda i, f: (i, 0)),
        pl.BlockSpec((1, d), lambda i, f: (0, 0)),
        vec, vec, vec,
        pl.BlockSpec((d, tf), lambda i, f: (0, f)),
        pl.BlockSpec((d, tf), lambda i, f: (0, nf + f)),
        pl.BlockSpec((tf, d), lambda i, f: (f, 0)),
    ]
    out_specs = [pl.BlockSpec((tm, d), lambda i, f: (i, 0))]
    out_shape = [jax.ShapeDtypeStruct((m, d), F32)]
    args = [x, gain.reshape(1, d), shift, scale, gate, w_in, w_in, w_out]
    if next_weights is not None:
        nin, nout, nl = next_weights
        in_rows, in_cols = d // ni, 2 * d_ff // nf
        out_rows, out_cols = d_ff // nf, d // ni
        in_specs += [pl.BlockSpec((1, in_rows, in_cols), lambda i, f: (nl, i, f)),
                     pl.BlockSpec((1, out_rows, out_cols), lambda i, f: (nl, f, i))]
        out_specs += [pl.BlockSpec((in_rows, in_cols), lambda i, f: (i, f)),
                      pl.BlockSpec((out_rows, out_cols), lambda i, f: (f, i))]
        out_shape += [jax.ShapeDtypeStruct((d, 2 * d_ff), BF16), jax.ShapeDtypeStruct((d_ff, d), BF16)]
        args += [nin, nout]
    outs = pl.pallas_call(
        functools.partial(_ffn_kernel, cast_next=next_weights is not None),
        grid=(ni, nf),
        in_specs=in_specs,
        out_specs=out_specs,
        out_shape=out_shape,
        scratch_shapes=[pltpu.VMEM((tm, d), BF16)],
        compiler_params=_params("parallel", "arbitrary"),
        name="swiglu_ffn_residual",
    )(*args)
    return outs if next_weights is not None else outs[0]


def _cumsum_rows(g, tril):
    n = g.shape[1]
    y = _dot(tril, jnp.concatenate(_split3(g), axis=1))
    return y[:, :n] + y[:, n:2 * n] + y[:, 2 * n:]


def _glr_heads(qs, ks, vs, cums, s_ref, sub):
    heads = range(len(qs))
    c, dk = qs[0].shape
    dv = vs[0].shape[1]
    vbs = [v.astype(BF16) for v in vs]
    lasts = [cum[c - 1:c, :] for cum in cums]

    inter = [_dot((qs[h] * jnp.exp2(cums[h])).astype(BF16), s_ref[h].astype(BF16)) for h in heads]
    updates = [_dot_tn((ks[h] * jnp.exp2(lasts[h] - cums[h])).astype(BF16), vbs[h]) for h in heads]
    for h in heads:
        decay_cols = jnp.transpose(jnp.broadcast_to(jnp.exp2(lasts[h]), (LANES, dk)))
        s_ref[h] = jnp.concatenate([decay_cols] * (dv // LANES), axis=1) * s_ref[h] + updates[h]

    spans = [(r0, r0 + sub) for r0 in range(0, c, sub)]
    scores = []
    for h in heads:
        q, k, cum = qs[h], ks[h], cums[h]
        per_head = []
        for r0, r1 in spans:
            hi = cum[r1 - 1:r1, :]
            mid = 0.5 * hi if r0 == 0 else 0.5 * (cum[r0 - 1:r0, :] + hi)
            qi = (q[r0:r1] * jnp.exp2(cum[r0:r1] - mid)).astype(BF16)
            ki = (k[:r1] * jnp.exp2(mid - cum[:r1])).astype(BF16)
            a = _dot_nt(qi, ki)
            row = r0 + lax.broadcasted_iota(jnp.int32, (sub, r1), 0)
            col = lax.broadcasted_iota(jnp.int32, (sub, r1), 1)
            per_head.append(jnp.where(col <= row, a, 0.0).astype(BF16))
        scores.append(per_head)
    outs = []
    for h in heads:
        blocks = [_dot(a, vbs[h][:r1]) for a, (_, r1) in zip(scores[h], spans)]
        intra = blocks[0] if len(blocks) == 1 else jnp.concatenate(blocks, axis=0)
        outs.append(inter[h] + intra)
    return outs


def _tril_bf16(c):
    row = lax.broadcasted_iota(jnp.int32, (c, c), 0)
    col = lax.broadcasted_iota(jnp.int32, (c, c), 1)
    return jnp.where(col <= row, 1.0, 0.0).astype(BF16)


def _hgrn_kernel(q_ref, f_ref, i_ref, g_ref, lb_ref, gain_ref, o_ref, s_ref, *, chunk, sub):
    @pl.when(pl.program_id(1) == 0)
    def _():
        s_ref[...] = jnp.zeros_like(s_ref)

    heads, dk, _ = s_ref.shape
    tril = _tril_bf16(chunk)

    def body(ci, carry):
        rows = pl.ds(pl.multiple_of(ci * chunk, chunk), chunk)
        lb = lb_ref[...]
        sig = _sigmoid(f_ref[rows, :].astype(F32))
        forget = lb + (1.0 - lb) * sig
        cum = _cumsum_rows(jnp.log2(jnp.maximum(forget, MIN_FORGET)), tril)
        key = (1.0 - lb) * (1.0 - sig)
        q = _silu(q_ref[rows, :].astype(F32))
        v = i_ref[rows, :]
        cols = [slice(hh * dk, (hh + 1) * dk) for hh in range(heads)]
        outs = _glr_heads([q[:, cs] for cs in cols], [key[:, cs] for cs in cols],
                          [v[:, cs] for cs in cols], [cum[:, cs] for cs in cols], s_ref, sub)
        for o, cs in zip(outs, cols):
            o = o * lax.rsqrt(jnp.mean(o * o, axis=-1, keepdims=True) + EPS) * gain_ref[:, cs]
            o_ref[rows, cs] = (o * _silu(g_ref[rows, cs].astype(F32))).astype(o_ref.dtype)
        return carry

    lax.fori_loop(0, q_ref.shape[0] // chunk, body, 0, unroll=4)


def hgrn2_heads(proj, lb, out_gain, batch, seq, tb=REC_TB, chunk=REC_CHUNK, sub=HGRN_SUB):
    m = proj.shape[0]
    tb = min(tb, seq)
    nt = seq // tb
    h, dk = A_HEADS, A_DK
    w = h * dk

    def col(group):
        return pl.BlockSpec((tb, w), lambda b, t: (b * nt + t, group))

    vec = pl.BlockSpec((1, w), lambda b, t: (0, 0))
    return pl.pallas_call(
        functools.partial(_hgrn_kernel, chunk=chunk, sub=sub),
        grid=(batch, nt),
        in_specs=[col(0), col(1), col(2), col(3), vec, vec],
        out_specs=pl.BlockSpec((tb, w), lambda b, t: (b * nt + t, 0)),
        out_shape=jax.ShapeDtypeStruct((m, w), BF16),
        scratch_shapes=[pltpu.VMEM((h, dk, dk), F32)],
        compiler_params=_params("parallel", "arbitrary"),
        name="hgrn2_recurrence",
    )(proj, proj, proj, proj, lb.reshape(1, w), out_gain.reshape(1, w))


def _gla_kernel(q_ref, k_ref, v_ref, g_ref, low_ref, wup_ref, bias_ref, gain_ref, o_ref, s_ref,
                *, chunk, sub):
    @pl.when(pl.program_id(1) == 0)
    def _():
        s_ref[...] = jnp.zeros_like(s_ref)

    heads, dk, dv = s_ref.shape
    tril = _tril_bf16(chunk)
    q_scale = dk ** -0.5

    def body(ci, carry):
        rows = pl.ds(pl.multiple_of(ci * chunk, chunk), chunk)
        z = _dot(low_ref[rows, :].astype(BF16), wup_ref[...]) + bias_ref[...]
        cum = _cumsum_rows(_log_sigmoid(z) * (LOG2E / GLA_GATE_NORMALIZER), tril)
        kcs = [slice(hh * dk, (hh + 1) * dk) for hh in range(heads)]
        vcs = [slice(hh * dv, (hh + 1) * dv) for hh in range(heads)]
        outs = _glr_heads([q_ref[rows, kc].astype(F32) * q_scale for kc in kcs],
                          [k_ref[rows, kc].astype(F32) for kc in kcs],
                          [v_ref[rows, vc] for vc in vcs], [cum[:, kc] for kc in kcs], s_ref, sub)
        for o, vc in zip(outs, vcs):
            o = o * lax.rsqrt(jnp.mean(o * o, axis=-1, keepdims=True) + EPS) * gain_ref[...]
            o_ref[rows, vc] = (o * _silu(g_ref[rows, vc].astype(F32))).astype(o_ref.dtype)
        return carry

    lax.fori_loop(0, q_ref.shape[0] // chunk, body, 0)


def gla_heads(proj, low, w_up, b_gate, out_gain, batch, seq, tb=REC_TB, chunk=GLA_CHUNK,
              sub=GLA_SUB):
    m = proj.shape[0]
    tb = min(tb, seq)
    nt = seq // tb
    h, dk, dv = C_HEADS, C_DK, C_DV
    kw, vw = h * dk, h * dv
    v0 = 2 * kw // vw
    return pl.pallas_call(
        functools.partial(_gla_kernel, chunk=chunk, sub=sub),
        grid=(batch, nt),
        in_specs=[
            pl.BlockSpec((tb, kw), lambda b, t: (b * nt + t, 0)),
            pl.BlockSpec((tb, kw), lambda b, t: (b * nt + t, 1)),
            pl.BlockSpec((tb, vw), lambda b, t: (b * nt + t, v0)),
            pl.BlockSpec((tb, vw), lambda b, t: (b * nt + t, v0 + 1)),
            pl.BlockSpec((tb, LANES), lambda b, t: (b * nt + t, 0)),
            pl.BlockSpec((LANES, kw), lambda b, t: (0, 0)),
            pl.BlockSpec((1, kw), lambda b, t: (0, 0)),
            pl.BlockSpec((1, dv), lambda b, t: (0, 0)),
        ],
        out_specs=pl.BlockSpec((tb, vw), lambda b, t: (b * nt + t, 0)),
        out_shape=jax.ShapeDtypeStruct((m, vw), BF16),
        scratch_shapes=[pltpu.VMEM((h, dk, dv), F32)],
        compiler_params=_params("parallel", "arbitrary"),
        name="gla_recurrence",
    )(proj, proj, proj, proj, low, w_up, b_gate.reshape(1, kw), out_gain.reshape(1, dv))


F_PIECES = 3


def _fox_prep_kernel(q_ref, k_ref, logit_ref, bias_ref, qg_ref, kg_ref,
                     qo_ref, ko_ref, carry_ref, *, heads, head_dim):
    @pl.when(pl.program_id(1) == 0)
    def _():
        carry_ref[...] = jnp.zeros_like(carry_ref)

    tb = q_ref.shape[0]
    log_f = _log_sigmoid(logit_ref[...] + bias_ref[...])
    cum = _cumsum_rows(log_f, _tril_bf16(tb)) + carry_ref[0:1, :]
    carry_ref[...] = jnp.broadcast_to(cum[tb - 1:tb, :], carry_ref.shape)
    pieces = jnp.concatenate(_split3(-LOG2E * cum), axis=1)
    src = lax.broadcasted_iota(jnp.int32, (F_PIECES * LANES, heads * head_dim), 0)
    dst = lax.broadcasted_iota(jnp.int32, (F_PIECES * LANES, heads * head_dim), 1)
    select = jnp.where((dst // head_dim == src % LANES) & (dst % head_dim == src // LANES), 1.0, 0.0)
    bias_all = _dot(pieces, select.astype(BF16))
    lane = lax.broadcasted_iota(jnp.int32, (tb, head_dim), 1)
    one_cols = jnp.where(lane < F_PIECES, 1.0, 0.0).astype(qo_ref.dtype)

    qg = qg_ref[...] * (head_dim ** -0.5 * LOG2E)
    kg = kg_ref[...]
    for hh in range(heads):
        cols = slice(hh * head_dim, (hh + 1) * head_dim)
        lo = 2 * hh * head_dim
        main = slice(lo, lo + head_dim)
        extra = slice(lo + head_dim, lo + 2 * head_dim)
        q = q_ref[:, cols].astype(F32)
        qo_ref[:, main] = (q * lax.rsqrt(jnp.mean(q * q, axis=-1, keepdims=True) + EPS) * qg
                           ).astype(qo_ref.dtype)
        qo_ref[:, extra] = one_cols
        k = k_ref[:, cols].astype(F32)
        ko_ref[:, main] = (k * lax.rsqrt(jnp.mean(k * k, axis=-1, keepdims=True) + EPS) * kg
                           ).astype(ko_ref.dtype)
        ko_ref[:, extra] = bias_all[:, cols].astype(ko_ref.dtype)


def fox_prep(proj, logits, f_bias, q_gain, k_gain, batch, seq, tb=PREP_TB):
    m = proj.shape[0]
    h, hd = B_HEADS, B_HEAD_DIM
    w = h * hd
    tb = min(tb, seq)
    nt = seq // tb
    base = (4 * A_HEADS * A_DK) // w
    bias = jnp.zeros((1, LANES), F32).at[0, :h].set(f_bias)

    def col(group):
        return pl.BlockSpec((tb, w), lambda b, t: (b * nt + t, base + group))

    wide_tok = pl.BlockSpec((tb, 2 * w), lambda b, t: (b * nt + t, 0))
    return pl.pallas_call(
        functools.partial(_fox_prep_kernel, heads=h, head_dim=hd),
        grid=(batch, nt),
        in_specs=[
            col(0), col(1),
            pl.BlockSpec((tb, LANES), lambda b, t: (b * nt + t, 0)),
            pl.BlockSpec((1, LANES), lambda b, t: (0, 0)),
            pl.BlockSpec((1, hd), lambda b, t: (0, 0)),
            pl.BlockSpec((1, hd), lambda b, t: (0, 0)),
        ],
        out_specs=[wide_tok, wide_tok],
        out_shape=[jax.ShapeDtypeStruct((m, 2 * w), BF16)] * 2,
        scratch_shapes=[pltpu.VMEM((8, LANES), F32)],
        compiler_params=_params("parallel", "arbitrary"),
        name="fox_prep",
    )(proj, proj, logits, bias, q_gain.reshape(1, hd), k_gain.reshape(1, hd))


def _fox_attn_kernel(q_ref, k_ref, v_ref, g_ref, o_ref, m_ref, acc_ref, *, tq, tk, heads, head_dim):
    qi = pl.program_id(2)
    q_start = pl.multiple_of(qi * tq, tq)
    wide = 2 * head_dim

    def tile(start, mask, width=tk, first=False):
        rows = pl.ds(start, width)
        ones = jnp.ones((width, head_dim), BF16)
        scores = [_dot_nt(q_ref[:, hh * wide:(hh + 1) * wide], k_ref[rows, hh * wide:(hh + 1) * wide])
                  for hh in range(heads)]
        for hh, s in enumerate(scores):
            if mask is not None:
                s = jnp.where(mask, s, MASK_VALUE)
            row_max = jnp.max(s, axis=-1, keepdims=True)
            if first:
                m_new = jnp.broadcast_to(row_max, (tq, LANES))
            else:
                m_prev = m_ref[hh]
                m_new = jnp.maximum(m_prev, row_max)
            p = jnp.exp2(s - jnp.concatenate([m_new] * (width // LANES), axis=1))
            m_ref[hh] = m_new
            v_aug = jnp.concatenate([v_ref[rows, hh * head_dim:(hh + 1) * head_dim], ones], axis=1)
            pv = _dot(p.astype(BF16), v_aug)
            if first:
                acc_ref[hh] = pv
            else:
                alpha = jnp.exp2(m_prev - m_new)
                acc_ref[hh] = jnp.concatenate([alpha] * (wide // LANES), axis=1) * acc_ref[hh] + pv

    for d in range(tq // tk):
        row = lax.broadcasted_iota(jnp.int32, (tq, tk), 0)
        col = lax.broadcasted_iota(jnp.int32, (tq, tk), 1) + d * tk
        tile(pl.multiple_of(q_start + d * tk, tk), col <= row, first=(d == 0))

    n_full = qi * (tq // tk)

    def quad(gi, carry):
        tile(pl.multiple_of(4 * gi * tk, 4 * tk), None, width=4 * tk)
        return carry

    lax.fori_loop(0, n_full // 4, quad, 0)

    @pl.when(n_full % 4 >= 2)
    def _():
        tile(pl.multiple_of((n_full // 4) * 4 * tk, 2 * tk), None, width=2 * tk)

    @pl.when(n_full % 2 == 1)
    def _():
        tile(pl.multiple_of((n_full - 1) * tk, tk), None)

    for hh in range(heads):
        acc = acc_ref[hh]
        cols = slice(hh * head_dim, (hh + 1) * head_dim)
        o_ref[:, cols] = (acc[:, :head_dim] / acc[:, head_dim:]
                          * _sigmoid(g_ref[:, cols].astype(F32))).astype(o_ref.dtype)


def fox_attention(q_aug, k_aug, proj, batch, seq, tq=ATT_TQ, tk=ATT_TK, group=ATT_HEADS):
    m = q_aug.shape[0]
    h, hd = B_HEADS, B_HEAD_DIM
    tq = min(tq, seq)
    tk = min(tk, tq)
    nq = seq // tq
    v_block = (4 * A_HEADS * A_DK + 2 * h * hd) // (group * hd)
    gate_block = v_block + h // group
    return pl.pallas_call(
        functools.partial(_fox_attn_kernel, tq=tq, tk=tk, heads=group, head_dim=hd),
        grid=(batch, h // group, nq),
        in_specs=[
            pl.BlockSpec((tq, 2 * group * hd), lambda b, hg, i: (b * nq + i, hg)),
            pl.BlockSpec((seq, 2 * group * hd), lambda b, hg, i: (b, hg),
                         pipeline_mode=pl.Buffered(1)),
            pl.BlockSpec((seq, group * hd), lambda b, hg, i: (b, v_block + hg),
                         pipeline_mode=pl.Buffered(1)),
            pl.BlockSpec((tq, group * hd), lambda b, hg, i: (b * nq + i, gate_block + hg)),
        ],
        out_specs=pl.BlockSpec((tq, group * hd), lambda b, hg, i: (b * nq + i, hg)),
        out_shape=jax.ShapeDtypeStruct((m, h * hd), BF16),
        scratch_shapes=[pltpu.VMEM((group, tq, LANES), F32), pltpu.VMEM((group, tq, 2 * hd), F32)],
        compiler_params=_params("parallel", "parallel", "arbitrary"),
        name="fox_attention",
    )(q_aug, k_aug, proj, proj)


def _pad_cols(w, n):
    return jnp.zeros((w.shape[0], n), w.dtype).at[:, :w.shape[1]].set(w)


def kernel(x, c, mod_w, mod_b, norm_mix_gain, norm_ffn_gain, ab_w_in, ab_w_out, hgrn_lb_logits,
           hgrn_out_gain, fox_q_gain, fox_k_gain, fox_f_bias, gla_w_in, gla_w_gate_up, gla_b_gate,
           gla_out_gain, gla_w_out, ffn_w_in, ffn_w_out):
    batch, seq, d = x.shape
    depth = mod_w.shape[0]
    ab_main = 4 * A_HEADS * A_DK + 4 * B_HEADS * B_HEAD_DIM
    c_main = 2 * C_HEADS * C_DK + 2 * C_HEADS * C_DV

    lbs = lower_bounds(hgrn_lb_logits)
    mod = modulation(c, mod_w, mod_b)
    xs = x.reshape(batch * seq, d)
    ab_w_in_b, ab_w_out_b = ab_w_in.astype(BF16), ab_w_out.astype(BF16)
    gla_w_in_b, gla_w_out_b = gla_w_in.astype(BF16), gla_w_out.astype(BF16)
    ffn_w_in_b, ffn_w_out_b = ffn_w_in[0].astype(BF16), ffn_w_out[0].astype(BF16)

    for layer in range(depth):
        sh1, sc1, g1, sh2, sc2, g2 = [
            mod[layer, :, i * d:(i + 1) * d].reshape(batch, 1, d) for i in range(6)]
        if layer % 2 == 0:
            i = layer // 2
            proj, logits = norm_proj(xs, norm_mix_gain[layer], sh1, sc1, ab_w_in_b, i, ab_main,
                                     _pad_cols(ab_w_in[i, :, ab_main:], LANES).astype(BF16), seq)
            o_a = hgrn2_heads(proj, lbs[i], hgrn_out_gain[i], batch, seq)
            q_aug, k_aug = fox_prep(proj, logits, fox_f_bias[i], fox_q_gain[i], fox_k_gain[i],
                                    batch, seq)
            o_b = fox_attention(q_aug, k_aug, proj, batch, seq)
            xs = out_proj_residual([o_a, o_b], ab_w_out_b, i, xs, g1, seq)
        else:
            j = layer // 2
            proj, low = norm_proj(xs, norm_mix_gain[layer], sh1, sc1, gla_w_in_b, j, c_main,
                                  _pad_cols(gla_w_in[j, :, c_main:], LANES).astype(BF16), seq)
            w_up = jnp.zeros((LANES, gla_w_gate_up.shape[2]), BF16).at[:C_GATE_RANK].set(
                gla_w_gate_up[j].astype(BF16))
            o_c = gla_heads(proj, low, w_up, gla_b_gate[j], gla_out_gain[j], batch, seq)
            xs = out_proj_residual([o_c], gla_w_out_b, j, xs, g1, seq)
        if layer + 1 < depth:
            xs, ffn_w_in_b, ffn_w_out_b = ffn_residual(
                xs, norm_ffn_gain[layer], sh2, sc2, g2, ffn_w_in_b, ffn_w_out_b, seq,
                next_weights=(ffn_w_in, ffn_w_out, layer + 1))
        else:
            xs = ffn_residual(xs, norm_ffn_gain[layer], sh2, sc2, g2, ffn_w_in_b, ffn_w_out_b, seq)
    return xs.reshape(batch, seq, d)
```

```python
import functools

import jax
import jax.numpy as jnp
from jax import lax
from jax.experimental import pallas as pl
from jax.experimental.pallas import tpu as pltpu

F32 = jnp.float32
BF16 = jnp.bfloat16

EPS = 1e-6
MIN_FORGET = 1e-30
MASK_VALUE = -1e30
GLA_GATE_NORMALIZER = 16.0
LOG2E = 1.4426950408889634

LANES = 128
VMEM_LIMIT = 60 * 1024 * 1024

A_HEADS = 8
A_DK = 128
B_HEADS = 8
B_HEAD_DIM = 128
C_HEADS = 4
C_DK = 256
C_DV = 512
C_GATE_RANK = 16

PROJ_TM = 1024
PROJ_TN = 2048
PROJ_TC = 512
OUT_TM = 512
FFN_TM = 1024
FFN_TF = 512
FFN_TN = 512
NORM_ROWS = 16
NORM_UNROLL = 8
REC_TB = 512
REC_CHUNK = 64
HGRN_SUB = 16
GLA_CHUNK = 256
GLA_SUB = 64
ATT_TQ = 512
ATT_TK = 512
ATT_HEADS = 4
PREP_TB = 512


def _params(*sem):
    return pltpu.CompilerParams(dimension_semantics=sem, vmem_limit_bytes=VMEM_LIMIT)


def _sigmoid(x):
    return 1.0 / (1.0 + jnp.exp2(x * (-LOG2E)))


def _silu(x):
    return x * _sigmoid(x)


def _log_sigmoid(x):
    return jnp.minimum(x, 0.0) - jnp.log(1.0 + jnp.exp(-jnp.abs(x)))


def _split3(x):
    hi = x.astype(BF16)
    r = x - hi.astype(F32)
    mid = r.astype(BF16)
    lo = (r - mid.astype(F32)).astype(BF16)
    return hi, mid, lo


def _dot(a, b):
    return jnp.dot(a, b, preferred_element_type=F32)


def _dot_nt(a, b):
    return lax.dot_general(a, b, (((1,), (1,)), ((), ())), preferred_element_type=F32)


def _dot_tn(a, b):
    return lax.dot_general(a, b, (((0,), (0,)), ((), ())), preferred_element_type=F32)


def _lower_bound_kernel(logit_ref, o_ref):
    z = logit_ref[...]
    n = z.shape[0]
    e = jnp.exp(z - jnp.max(z, axis=0, keepdims=True))
    p = e / jnp.sum(e, axis=0, keepdims=True)
    run = jnp.zeros_like(p[0:1])
    rows = [run]
    for k in range(1, n):
        run = run + p[k:k + 1]
        rows.append(run)
    o_ref[...] = jnp.clip(jnp.concatenate(rows, axis=0), 0.0, 1.0 - 1e-6)


def lower_bounds(logits):
    return pl.pallas_call(
        _lower_bound_kernel,
        out_shape=jax.ShapeDtypeStruct(logits.shape, F32),
        name="hgrn_lower_bounds",
    )(logits.astype(F32))


def _mod_kernel(c_ref, w_ref, b_ref, o_ref):
    s = _silu(c_ref[...]).astype(BF16)
    o_ref[0] = _dot(s, w_ref[0].astype(BF16)) + b_ref[0]


def modulation(c, mod_w, mod_b, tn=2048):
    n_layers, d, n = mod_w.shape
    b = c.shape[0]
    rows = 8
    c_pad = jnp.zeros((rows, d), F32).at[:b].set(c)
    out = pl.pallas_call(
        _mod_kernel,
        grid=(n_layers, n // tn),
        in_specs=[
            pl.BlockSpec((rows, d), lambda l, j: (0, 0)),
            pl.BlockSpec((1, d, tn), lambda l, j: (l, 0, j)),
            pl.BlockSpec((1, 1, tn), lambda l, j: (l, 0, j)),
        ],
        out_specs=pl.BlockSpec((1, rows, tn), lambda l, j: (l, 0, j)),
        out_shape=jax.ShapeDtypeStruct((n_layers, rows, n), F32),
        compiler_params=_params("parallel", "parallel"),
        name="adaln_modulation",
    )(c_pad, mod_w, mod_b.reshape(n_layers, 1, n))
    return out[:, :b]


def _norm_rows(x_ref, gain_ref, sh_ref, sc_ref, h_ref, copy_ref=None):
    col_scale = gain_ref[...] * (1.0 + sc_ref[0])
    shift = sh_ref[0]

    def body(r, carry):
        rows = pl.ds(pl.multiple_of(r * NORM_ROWS, NORM_ROWS), NORM_ROWS)
        x = x_ref[rows, :]
        inv = lax.rsqrt(jnp.mean(x * x, axis=-1, keepdims=True) + EPS)
        h_ref[rows, :] = (x * inv * col_scale + shift).astype(h_ref.dtype)
        if copy_ref is not None:
            copy_ref[rows, :] = x
        return carry

    lax.fori_loop(0, x_ref.shape[0] // NORM_ROWS, body, 0, unroll=NORM_UNROLL)


def _norm_proj_kernel(*refs, cast_side):
    if cast_side:
        (x_ref, gain_ref, sh_ref, sc_ref, w_ref, waux_ref, ca_ref, cb_ref,
         o_ref, oaux_ref, ca_o_ref, cb_o_ref, h_ref) = refs
    else:
        x_ref, gain_ref, sh_ref, sc_ref, w_ref, waux_ref, o_ref, oaux_ref, h_ref = refs

    @pl.when(pl.program_id(1) == 0)
    def _():
        _norm_rows(x_ref, gain_ref, sh_ref, sc_ref, h_ref)
        oaux_ref[...] = _dot(h_ref[...], waux_ref[...])

    h = h_ref[...]
    for c0 in range(0, o_ref.shape[1], PROJ_TC):
        cols = slice(c0, c0 + PROJ_TC)
        o_ref[:, cols] = _dot(h, w_ref[0, :, cols]).astype(o_ref.dtype)
    if cast_side:
        ca_o_ref[...] = ca_ref[0].astype(ca_o_ref.dtype)
        cb_o_ref[...] = cb_ref[0].astype(cb_o_ref.dtype)


def _side_cast_specs(mats, layer, n_i, n_j):
    a, b = mats
    ra, ca = a.shape[1] // n_i, a.shape[2] // n_j
    rb, cb = b.shape[1] // n_j, b.shape[2] // n_i
    in_specs = [pl.BlockSpec((1, ra, ca), lambda i, j: (layer, i, j)),
                pl.BlockSpec((1, rb, cb), lambda i, j: (layer, j, i))]
    out_specs = [pl.BlockSpec((ra, ca), lambda i, j: (i, j)),
                 pl.BlockSpec((rb, cb), lambda i, j: (j, i))]
    out_shape = [jax.ShapeDtypeStruct(a.shape[1:], BF16), jax.ShapeDtypeStruct(b.shape[1:], BF16)]
    return in_specs, out_specs, out_shape


def norm_proj(x, gain, shift, scale, w_all, layer, n, w_aux, seq, side_cast=None,
              tm=PROJ_TM, tn=PROJ_TN):
    m, d = x.shape
    tm = min(tm, seq)
    per_batch = seq // tm
    grid = (m // tm, n // tn)
    in_specs = [
        pl.BlockSpec((tm, d), lambda i, j: (i, 0)),
        pl.BlockSpec((1, d), lambda i, j: (0, 0)),
        pl.BlockSpec((1, 1, d), lambda i, j: (i // per_batch, 0, 0)),
        pl.BlockSpec((1, 1, d), lambda i, j: (i // per_batch, 0, 0)),
        pl.BlockSpec((1, d, tn), lambda i, j: (layer, 0, j)),
        pl.BlockSpec((d, LANES), lambda i, j: (0, 0)),
    ]
    out_specs = [
        pl.BlockSpec((tm, tn), lambda i, j: (i, j)),
        pl.BlockSpec((tm, LANES), lambda i, j: (i, 0)),
    ]
    out_shape = [
        jax.ShapeDtypeStruct((m, n), BF16),
        jax.ShapeDtypeStruct((m, LANES), F32),
    ]
    args = [x, gain.reshape(1, d), shift, scale, w_all, w_aux]
    if side_cast is not None:
        mats, cast_layer = side_cast
        extra_in, extra_out, extra_shape = _side_cast_specs(mats, cast_layer, *grid)
        in_specs += extra_in
        out_specs += extra_out
        out_shape += extra_shape
        args += list(mats)
    return pl.pallas_call(
        functools.partial(_norm_proj_kernel, cast_side=side_cast is not None),
        grid=grid,
        in_specs=in_specs,
        out_specs=out_specs,
        out_shape=out_shape,
        scratch_shapes=[pltpu.VMEM((tm, d), BF16)],
        compiler_params=_params("parallel", "arbitrary"),
        name="norm_in_proj",
    )(*args)


def _out_proj_kernel(*refs, n_lhs):
    lhs = refs[:n_lhs]
    w_ref, x_ref, gate_ref, o_ref = refs[n_lhs:]
    width = lhs[0].shape[1]
    y = _dot(lhs[0][...], w_ref[0, 0:width, :])
    for part, a_ref in enumerate(lhs[1:], 1):
        y = y + _dot(a_ref[...], w_ref[0, part * width:(part + 1) * width, :])
    o_ref[...] = x_ref[...] + gate_ref[0] * y


def out_proj_residual(lhs_list, w_all, layer, x, gate, seq, tm=OUT_TM):
    m, d = x.shape
    tm = min(tm, seq)
    per_batch = seq // tm
    n_lhs = len(lhs_list)
    width = lhs_list[0].shape[1]
    k = w_all.shape[1]
    assert all(a.shape[1] == width for a in lhs_list) and n_lhs * width == k
    in_specs = (
        [pl.BlockSpec((tm, width), lambda i: (i, 0)) for _ in lhs_list]
        + [pl.BlockSpec((1, k, d), lambda i: (layer, 0, 0), pipeline_mode=pl.Buffered(1)),
           pl.BlockSpec((tm, d), lambda i: (i, 0)),
           pl.BlockSpec((1, 1, d), lambda i: (i // per_batch, 0, 0))]
    )
    return pl.pallas_call(
        functools.partial(_out_proj_kernel, n_lhs=n_lhs),
        grid=(m // tm,),
        in_specs=in_specs,
        out_specs=pl.BlockSpec((tm, d), lambda i: (i, 0)),
        out_shape=jax.ShapeDtypeStruct((m, d), F32),
        compiler_params=_params("parallel"),
        name="out_proj_residual",
    )(*lhs_list, w_all, x, gate)


def _ffn_kernel(*refs, cast_next):
    if cast_next:
        (x_ref, gain_ref, sh_ref, sc_ref, gate_ref, wa_ref, wu_ref, wo_ref, nin_ref, nout_ref,
         o_ref, nin_o_ref, nout_o_ref, h_ref) = refs
    else:
        x_ref, gain_ref, sh_ref, sc_ref, gate_ref, wa_ref, wu_ref, wo_ref, o_ref, h_ref = refs

    @pl.when(pl.program_id(1) == 0)
    def _():
        _norm_rows(x_ref, gain_ref, sh_ref, sc_ref, h_ref, copy_ref=o_ref)

    h = h_ref[...]
    a = _dot(h, wa_ref[...])
    u = _dot(h, wu_ref[...])
    act = (_silu(a) * u).astype(BF16)
    d = o_ref.shape[1]
    for c0 in range(0, d, FFN_TN):
        cols = slice(c0, c0 + FFN_TN)
        o_ref[:, cols] += gate_ref[0][:, cols] * _dot(act, wo_ref[:, cols])
    if cast_next:
        nin_o_ref[...] = nin_ref[0].astype(nin_o_ref.dtype)
        nout_o_ref[...] = nout_ref[0].astype(nout_o_ref.dtype)


def ffn_residual(x, gain, shift, scale, gate, w_in, w_out, seq, next_weights=None,
                 tm=FFN_TM, tf=FFN_TF):
    m, d = x.shape
    d_ff = w_out.shape[0]
    tm = min(tm, seq)
    per_batch = seq // tm
    ni, nf = m // tm, d_ff // tf
    vec = pl.BlockSpec((1, 1, d), lambda i, f: (i // per_batch, 0, 0))
    in_specs = [
        pl.BlockSpec((tm, d), lambda i, f: (i, 0)),
        pl.BlockSpec((1, d), lambda i, f: (0, 0)),
        vec, vec, vec,
        pl.BlockSpec((d, tf), lambda i, f: (0, f)),
        pl.BlockSpec((d, tf), lambda i, f: (0, nf + f)),
        pl.BlockSpec((tf, d), lambda i, f: (f, 0)),
    ]
    out_specs = [pl.BlockSpec((tm, d), lambda i, f: (i, 0))]
    out_shape = [jax.ShapeDtypeStruct((m, d), F32)]
    args = [x, gain.reshape(1, d), shift, scale, gate, w_in, w_in, w_out]
    if next_weights is not None:
        nin, nout, nl = next_weights
        extra_in, extra_out, extra_shape = _side_cast_specs((nin, nout), nl, ni, nf)
        in_specs += extra_in
        out_specs += extra_out
        out_shape += extra_shape
        args += [nin, nout]
    outs = pl.pallas_call(
        functools.partial(_ffn_kernel, cast_next=next_weights is not None),
        grid=(ni, nf),
        in_specs=in_specs,
        out_specs=out_specs,
        out_shape=out_shape,
        scratch_shapes=[pltpu.VMEM((tm, d), BF16)],
        compiler_params=_params("parallel", "arbitrary"),
        name="swiglu_ffn_residual",
    )(*args)
    return outs if next_weights is not None else outs[0]


def _cumsum_rows(g, tril):
    n = g.shape[1]
    y = _dot(tril, jnp.concatenate(_split3(g), axis=1))
    return y[:, :n] + y[:, n:2 * n] + y[:, 2 * n:]


def _glr_heads(qs, ks, vs, cums, s_ref, sub):
    heads = range(len(qs))
    c, dk = qs[0].shape
    dv = vs[0].shape[1]
    vbs = [v.astype(BF16) for v in vs]
    lasts = [cum[c - 1:c, :] for cum in cums]

    inter = [_dot((qs[h] * jnp.exp2(cums[h])).astype(BF16), s_ref[h].astype(BF16)) for h in heads]
    updates = [_dot_tn((ks[h] * jnp.exp2(lasts[h] - cums[h])).astype(BF16), vbs[h]) for h in heads]
    for h in heads:
        decay_cols = jnp.transpose(jnp.broadcast_to(jnp.exp2(lasts[h]), (LANES, dk)))
        s_ref[h] = jnp.concatenate([decay_cols] * (dv // LANES), axis=1) * s_ref[h] + updates[h]

    spans = [(r0, r0 + sub) for r0 in range(0, c, sub)]
    scores = []
    for h in heads:
        q, k, cum = qs[h], ks[h], cums[h]
        per_head = []
        for r0, r1 in spans:
            hi = cum[r1 - 1:r1, :]
            mid = 0.5 * hi if r0 == 0 else 0.5 * (cum[r0 - 1:r0, :] + hi)
            qi = (q[r0:r1] * jnp.exp2(cum[r0:r1] - mid)).astype(BF16)
            ki = (k[:r1] * jnp.exp2(mid - cum[:r1])).astype(BF16)
            a = _dot_nt(qi, ki)
            row = r0 + lax.broadcasted_iota(jnp.int32, (sub, r1), 0)
            col = lax.broadcasted_iota(jnp.int32, (sub, r1), 1)
            per_head.append(jnp.where(col <= row, a, 0.0).astype(BF16))
        scores.append(per_head)
    outs = []
    for h in heads:
        blocks = [_dot(a, vbs[h][:r1]) for a, (_, r1) in zip(scores[h], spans)]
        intra = blocks[0] if len(blocks) == 1 else jnp.concatenate(blocks, axis=0)
        outs.append(inter[h] + intra)
    return outs


def _tril_bf16(c):
    row = lax.broadcasted_iota(jnp.int32, (c, c), 0)
    col = lax.broadcasted_iota(jnp.int32, (c, c), 1)
    return jnp.where(col <= row, 1.0, 0.0).astype(BF16)


def _hgrn_kernel(q_ref, f_ref, i_ref, g_ref, lb_ref, gain_ref, o_ref, s_ref, *, chunk, sub):
    @pl.when(pl.program_id(1) == 0)
    def _():
        s_ref[...] = jnp.zeros_like(s_ref)

    heads, dk, _ = s_ref.shape
    tril = _tril_bf16(chunk)

    def body(ci, carry):
        rows = pl.ds(pl.multiple_of(ci * chunk, chunk), chunk)
        lb = lb_ref[...]
        sig = _sigmoid(f_ref[rows, :].astype(F32))
        forget = lb + (1.0 - lb) * sig
        cum = _cumsum_rows(jnp.log2(jnp.maximum(forget, MIN_FORGET)), tril)
        key = (1.0 - lb) * (1.0 - sig)
        q = _silu(q_ref[rows, :].astype(F32))
        v = i_ref[rows, :]
        cols = [slice(hh * dk, (hh + 1) * dk) for hh in range(heads)]
        outs = _glr_heads([q[:, cs] for cs in cols], [key[:, cs] for cs in cols],
                          [v[:, cs] for cs in cols], [cum[:, cs] for cs in cols], s_ref, sub)
        for o, cs in zip(outs, cols):
            o = o * lax.rsqrt(jnp.mean(o * o, axis=-1, keepdims=True) + EPS) * gain_ref[:, cs]
            o_ref[rows, cs] = (o * _silu(g_ref[rows, cs].astype(F32))).astype(o_ref.dtype)
        return carry

    lax.fori_loop(0, q_ref.shape[0] // chunk, body, 0, unroll=4)


def hgrn2_heads(proj, lb, out_gain, batch, seq, tb=REC_TB, chunk=REC_CHUNK, sub=HGRN_SUB):
    m = proj.shape[0]
    tb = min(tb, seq)
    nt = seq // tb
    h, dk = A_HEADS, A_DK
    w = h * dk

    def col(group):
        return pl.BlockSpec((tb, w), lambda b, t: (b * nt + t, group))

    vec = pl.BlockSpec((1, w), lambda b, t: (0, 0))
    return pl.pallas_call(
        functools.partial(_hgrn_kernel, chunk=chunk, sub=sub),
        grid=(batch, nt),
        in_specs=[col(0), col(1), col(2), col(3), vec, vec],
        out_specs=pl.BlockSpec((tb, w), lambda b, t: (b * nt + t, 0)),
        out_shape=jax.ShapeDtypeStruct((m, w), BF16),
        scratch_shapes=[pltpu.VMEM((h, dk, dk), F32)],
        compiler_params=_params("parallel", "arbitrary"),
        name="hgrn2_recurrence",
    )(proj, proj, proj, proj, lb.reshape(1, w), out_gain.reshape(1, w))


def _gla_kernel(q_ref, k_ref, v_ref, g_ref, low_ref, wup_ref, bias_ref, gain_ref, o_ref, s_ref,
                *, chunk, sub):
    @pl.when(pl.program_id(1) == 0)
    def _():
        s_ref[...] = jnp.zeros_like(s_ref)

    heads, dk, dv = s_ref.shape
    tril = _tril_bf16(chunk)
    q_scale = dk ** -0.5

    def body(ci, carry):
        rows = pl.ds(pl.multiple_of(ci * chunk, chunk), chunk)
        z = _dot(low_ref[rows, :].astype(BF16), wup_ref[...]) + bias_ref[...]
        cum = _cumsum_rows(_log_sigmoid(z) * (LOG2E / GLA_GATE_NORMALIZER), tril)
        kcs = [slice(hh * dk, (hh + 1) * dk) for hh in range(heads)]
        vcs = [slice(hh * dv, (hh + 1) * dv) for hh in range(heads)]
        outs = _glr_heads([q_ref[rows, kc].astype(F32) * q_scale for kc in kcs],
                          [k_ref[rows, kc].astype(F32) for kc in kcs],
                          [v_ref[rows, vc] for vc in vcs], [cum[:, kc] for kc in kcs], s_ref, sub)
        for o, vc in zip(outs, vcs):
            o = o * lax.rsqrt(jnp.mean(o * o, axis=-1, keepdims=True) + EPS) * gain_ref[...]
            o_ref[rows, vc] = (o * _silu(g_ref[rows, vc].astype(F32))).astype(o_ref.dtype)
        return carry

    lax.fori_loop(0, q_ref.shape[0] // chunk, body, 0)


def gla_heads(proj, low, w_up, b_gate, out_gain, batch, seq, tb=REC_TB, chunk=GLA_CHUNK,
              sub=GLA_SUB):
    m = proj.shape[0]
    tb = min(tb, seq)
    nt = seq // tb
    h, dk, dv = C_HEADS, C_DK, C_DV
    kw, vw = h * dk, h * dv
    v0 = 2 * kw // vw
    return pl.pallas_call(
        functools.partial(_gla_kernel, chunk=chunk, sub=sub),
        grid=(batch, nt),
        in_specs=[
            pl.BlockSpec((tb, kw), lambda b, t: (b * nt + t, 0)),
            pl.BlockSpec((tb, kw), lambda b, t: (b * nt + t, 1)),
            pl.BlockSpec((tb, vw), lambda b, t: (b * nt + t, v0)),
            pl.BlockSpec((tb, vw), lambda b, t: (b * nt + t, v0 + 1)),
            pl.BlockSpec((tb, LANES), lambda b, t: (b * nt + t, 0)),
            pl.BlockSpec((LANES, kw), lambda b, t: (0, 0)),
            pl.BlockSpec((1, kw), lambda b, t: (0, 0)),
            pl.BlockSpec((1, dv), lambda b, t: (0, 0)),
        ],
        out_specs=pl.BlockSpec((tb, vw), lambda b, t: (b * nt + t, 0)),
        out_shape=jax.ShapeDtypeStruct((m, vw), BF16),
        scratch_shapes=[pltpu.VMEM((h, dk, dv), F32)],
        compiler_params=_params("parallel", "arbitrary"),
        name="gla_recurrence",
    )(proj, proj, proj, proj, low, w_up, b_gate.reshape(1, kw), out_gain.reshape(1, dv))


F_PIECES = 3


def _fox_prep_kernel(q_ref, k_ref, logit_ref, bias_ref, qg_ref, kg_ref,
                     qo_ref, ko_ref, carry_ref, *, heads, head_dim):
    @pl.when(pl.program_id(1) == 0)
    def _():
        carry_ref[...] = jnp.zeros_like(carry_ref)

    tb = q_ref.shape[0]
    log_f = _log_sigmoid(logit_ref[...] + bias_ref[...])
    cum = _cumsum_rows(log_f, _tril_bf16(tb)) + carry_ref[0:1, :]
    carry_ref[...] = jnp.broadcast_to(cum[tb - 1:tb, :], carry_ref.shape)
    pieces = jnp.concatenate(_split3(-LOG2E * cum), axis=1)
    src = lax.broadcasted_iota(jnp.int32, (F_PIECES * LANES, heads * head_dim), 0)
    dst = lax.broadcasted_iota(jnp.int32, (F_PIECES * LANES, heads * head_dim), 1)
    select = jnp.where((dst // head_dim == src % LANES) & (dst % head_dim == src // LANES), 1.0, 0.0)
    bias_all = _dot(pieces, select.astype(BF16))
    lane = lax.broadcasted_iota(jnp.int32, (tb, head_dim), 1)
    one_cols = jnp.where(lane < F_PIECES, 1.0, 0.0).astype(qo_ref.dtype)

    qg = qg_ref[...] * (head_dim ** -0.5 * LOG2E)
    kg = kg_ref[...]
    for hh in range(heads):
        cols = slice(hh * head_dim, (hh + 1) * head_dim)
        lo = 2 * hh * head_dim
        main = slice(lo, lo + head_dim)
        extra = slice(lo + head_dim, lo + 2 * head_dim)
        q = q_ref[:, cols].astype(F32)
        qo_ref[:, main] = (q * lax.rsqrt(jnp.mean(q * q, axis=-1, keepdims=True) + EPS) * qg
                           ).astype(qo_ref.dtype)
        qo_ref[:, extra] = one_cols
        k = k_ref[:, cols].astype(F32)
        ko_ref[:, main] = (k * lax.rsqrt(jnp.mean(k * k, axis=-1, keepdims=True) + EPS) * kg
                           ).astype(ko_ref.dtype)
        ko_ref[:, extra] = bias_all[:, cols].astype(ko_ref.dtype)


def fox_prep(proj, logits, f_bias, q_gain, k_gain, batch, seq, tb=PREP_TB):
    m = proj.shape[0]
    h, hd = B_HEADS, B_HEAD_DIM
    w = h * hd
    tb = min(tb, seq)
    nt = seq // tb
    base = (4 * A_HEADS * A_DK) // w
    bias = jnp.zeros((1, LANES), F32).at[0, :h].set(f_bias)

    def col(group):
        return pl.BlockSpec((tb, w), lambda b, t: (b * nt + t, base + group))

    wide_tok = pl.BlockSpec((tb, 2 * w), lambda b, t: (b * nt + t, 0))
    return pl.pallas_call(
        functools.partial(_fox_prep_kernel, heads=h, head_dim=hd),
        grid=(batch, nt),
        in_specs=[
            col(0), col(1),
            pl.BlockSpec((tb, LANES), lambda b, t: (b * nt + t, 0)),
            pl.BlockSpec((1, LANES), lambda b, t: (0, 0)),
            pl.BlockSpec((1, hd), lambda b, t: (0, 0)),
            pl.BlockSpec((1, hd), lambda b, t: (0, 0)),
        ],
        out_specs=[wide_tok, wide_tok],
        out_shape=[jax.ShapeDtypeStruct((m, 2 * w), BF16)] * 2,
        scratch_shapes=[pltpu.VMEM((8, LANES), F32)],
        compiler_params=_params("parallel", "arbitrary"),
        name="fox_prep",
    )(proj, proj, logits, bias, q_gain.reshape(1, hd), k_gain.reshape(1, hd))


def _fox_attn_kernel(q_ref, k_ref, v_ref, g_ref, o_ref, m_ref, acc_ref, *, tq, tk, heads, head_dim):
    qi = pl.program_id(2)
    q_start = pl.multiple_of(qi * tq, tq)
    wide = 2 * head_dim

    def tile(start, mask, width=tk, first=False):
        rows = pl.ds(start, width)
        ones = jnp.ones((width, head_dim), BF16)
        scores = [_dot_nt(q_ref[:, hh * wide:(hh + 1) * wide], k_ref[rows, hh * wide:(hh + 1) * wide])
                  for hh in range(heads)]
        for hh, s in enumerate(scores):
            if mask is not None:
                s = jnp.where(mask, s, MASK_VALUE)
            row_max = jnp.max(s, axis=-1, keepdims=True)
            if first:
                m_new = jnp.broadcast_to(row_max, (tq, LANES))
            else:
                m_prev = m_ref[hh]
                m_new = jnp.maximum(m_prev, row_max)
            p = jnp.exp2(s - jnp.concatenate([m_new] * (width // LANES), axis=1))
            m_ref[hh] = m_new
            v_aug = jnp.concatenate([v_ref[rows, hh * head_dim:(hh + 1) * head_dim], ones], axis=1)
            pv = _dot(p.astype(BF16), v_aug)
            if first:
                acc_ref[hh] = pv
            else:
                alpha = jnp.exp2(m_prev - m_new)
                acc_ref[hh] = jnp.concatenate([alpha] * (wide // LANES), axis=1) * acc_ref[hh] + pv

    for d in range(tq // tk):
        row = lax.broadcasted_iota(jnp.int32, (tq, tk), 0)
        col = lax.broadcasted_iota(jnp.int32, (tq, tk), 1) + d * tk
        tile(pl.multiple_of(q_start + d * tk, tk), col <= row, first=(d == 0))

    n_full = qi * (tq // tk)

    def quad(gi, carry):
        tile(pl.multiple_of(4 * gi * tk, 4 * tk), None, width=4 * tk)
        return carry

    lax.fori_loop(0, n_full // 4, quad, 0)

    @pl.when(n_full % 4 >= 2)
    def _():
        tile(pl.multiple_of((n_full // 4) * 4 * tk, 2 * tk), None, width=2 * tk)

    @pl.when(n_full % 2 == 1)
    def _():
        tile(pl.multiple_of((n_full - 1) * tk, tk), None)

    for hh in range(heads):
        acc = acc_ref[hh]
        cols = slice(hh * head_dim, (hh + 1) * head_dim)
        o_ref[:, cols] = (acc[:, :head_dim] / acc[:, head_dim:]
                          * _sigmoid(g_ref[:, cols].astype(F32))).astype(o_ref.dtype)


def fox_attention(q_aug, k_aug, proj, batch, seq, tq=ATT_TQ, tk=ATT_TK, group=ATT_HEADS):
    m = q_aug.shape[0]
    h, hd = B_HEADS, B_HEAD_DIM
    tq = min(tq, seq)
    tk = min(tk, tq)
    nq = seq // tq
    v_block = (4 * A_HEADS * A_DK + 2 * h * hd) // (group * hd)
    gate_block = v_block + h // group
    return pl.pallas_call(
        functools.partial(_fox_attn_kernel, tq=tq, tk=tk, heads=group, head_dim=hd),
        grid=(batch, h // group, nq),
        in_specs=[
            pl.BlockSpec((tq, 2 * group * hd), lambda b, hg, i: (b * nq + i, hg)),
            pl.BlockSpec((seq, 2 * group * hd), lambda b, hg, i: (b, hg),
                         pipeline_mode=pl.Buffered(1)),
            pl.BlockSpec((seq, group * hd), lambda b, hg, i: (b, v_block + hg),
                         pipeline_mode=pl.Buffered(1)),
            pl.BlockSpec((tq, group * hd), lambda b, hg, i: (b * nq + i, gate_block + hg)),
        ],
        out_specs=pl.BlockSpec((tq, group * hd), lambda b, hg, i: (b * nq + i, hg)),
        out_shape=jax.ShapeDtypeStruct((m, h * hd), BF16),
        scratch_shapes=[pltpu.VMEM((group, tq, LANES), F32), pltpu.VMEM((group, tq, 2 * hd), F32)],
        compiler_params=_params("parallel", "parallel", "arbitrary"),
        name="fox_attention",
    )(q_aug, k_aug, proj, proj)


def _pad_cols(w, n):
    return jnp.zeros((w.shape[0], n), w.dtype).at[:, :w.shape[1]].set(w)


def kernel(x, c, mod_w, mod_b, norm_mix_gain, norm_ffn_gain, ab_w_in, ab_w_out, hgrn_lb_logits,
           hgrn_out_gain, fox_q_gain, fox_k_gain, fox_f_bias, gla_w_in, gla_w_gate_up, gla_b_gate,
           gla_out_gain, gla_w_out, ffn_w_in, ffn_w_out):
    batch, seq, d = x.shape
    depth = mod_w.shape[0]
    ab_main = 4 * A_HEADS * A_DK + 4 * B_HEADS * B_HEAD_DIM
    c_main = 2 * C_HEADS * C_DK + 2 * C_HEADS * C_DV

    lbs = lower_bounds(hgrn_lb_logits)
    mod = modulation(c, mod_w, mod_b)
    xs = x.reshape(batch * seq, d)
    ab_w_in_b, ab_w_out_b = ab_w_in.astype(BF16), ab_w_out.astype(BF16)
    gla_w_in_b, gla_w_out_b = gla_w_in.astype(BF16), gla_w_out.astype(BF16)
    ffn_w_in_b = ffn_w_out_b = None

    for layer in range(depth):
        sh1, sc1, g1, sh2, sc2, g2 = [
            mod[layer, :, i * d:(i + 1) * d].reshape(batch, 1, d) for i in range(6)]
        if layer % 2 == 0:
            i = layer // 2
            w_aux = _pad_cols(ab_w_in[i, :, ab_main:], LANES).astype(BF16)
            if layer == 0:
                proj, logits, ffn_w_in_b, ffn_w_out_b = norm_proj(
                    xs, norm_mix_gain[layer], sh1, sc1, ab_w_in_b, i, ab_main, w_aux, seq,
                    side_cast=((ffn_w_in, ffn_w_out), 0))
            else:
                proj, logits = norm_proj(xs, norm_mix_gain[layer], sh1, sc1, ab_w_in_b, i, ab_main,
                                         w_aux, seq)
            o_a = hgrn2_heads(proj, lbs[i], hgrn_out_gain[i], batch, seq)
            q_aug, k_aug = fox_prep(proj, logits, fox_f_bias[i], fox_q_gain[i], fox_k_gain[i],
                                    batch, seq)
            o_b = fox_attention(q_aug, k_aug, proj, batch, seq)
            xs = out_proj_residual([o_a, o_b], ab_w_out_b, i, xs, g1, seq)
        else:
            j = layer // 2
            proj, low = norm_proj(xs, norm_mix_gain[layer], sh1, sc1, gla_w_in_b, j, c_main,
                                  _pad_cols(gla_w_in[j, :, c_main:], LANES).astype(BF16), seq)
            w_up = jnp.zeros((LANES, gla_w_gate_up.shape[2]), BF16).at[:C_GATE_RANK].set(
                gla_w_gate_up[j].astype(BF16))
            o_c = gla_heads(proj, low, w_up, gla_b_gate[j], gla_out_gain[j], batch, seq)
            xs = out_proj_residual([o_c], gla_w_out_b, j, xs, g1, seq)
        if layer + 1 < depth:
            xs, ffn_w_in_b, ffn_w_out_b = ffn_residual(
                xs, norm_ffn_gain[layer], sh2, sc2, g2, ffn_w_in_b, ffn_w_out_b, seq,
                next_weights=(ffn_w_in, ffn_w_out, layer + 1))
        else:
            xs = ffn_residual(xs, norm_ffn_gain[layer], sh2, sc2, g2, ffn_w_in_b, ffn_w_out_b, seq)
    return xs.reshape(batch, seq, d)
```

```python
import functools

import jax
import jax.numpy as jnp
from jax import lax
from jax.experimental import pallas as pl
from jax.experimental.pallas import tpu as pltpu

F32 = jnp.float32
BF16 = jnp.bfloat16

EPS = 1e-6
MIN_FORGET = 1e-30
MASK_VALUE = -1e30
GLA_GATE_NORMALIZER = 16.0
LOG2E = 1.4426950408889634

LANES = 128
VMEM_LIMIT = 60 * 1024 * 1024

A_HEADS = 8
A_DK = 128
B_HEADS = 8
B_HEAD_DIM = 128
C_HEADS = 4
C_DK = 256
C_DV = 512
C_GATE_RANK = 16

PROJ_TM = 1024
PROJ_TN = 2048
PROJ_TC = 512
OUT_TM = 512
FFN_TM = 1024
FFN_TF = 512
FFN_TN = 512
NORM_ROWS = 16
NORM_UNROLL = 8
REC_TB = 512
REC_CHUNK = 64
HGRN_SUB = 16
GLA_CHUNK = 256
GLA_SUB = 64
ATT_TQ = 512
ATT_TK = 512
ATT_HEADS = 4
PREP_TB = 512


def _params(*sem):
    return pltpu.CompilerParams(dimension_semantics=sem, vmem_limit_bytes=VMEM_LIMIT)


def _sigmoid(x):
    return 1.0 / (1.0 + jnp.exp2(x * (-LOG2E)))


def _silu(x):
    return x * _sigmoid(x)


def _log_sigmoid(x):
    return jnp.minimum(x, 0.0) - jnp.log(1.0 + jnp.exp(-jnp.abs(x)))


def _split3(x):
    hi = x.astype(BF16)
    r = x - hi.astype(F32)
    mid = r.astype(BF16)
    lo = (r - mid.astype(F32)).astype(BF16)
    return hi, mid, lo


def _dot(a, b):
    return jnp.dot(a, b, preferred_element_type=F32)


def _dot_nt(a, b):
    return lax.dot_general(a, b, (((1,), (1,)), ((), ())), preferred_element_type=F32)


def _dot_tn(a, b):
    return lax.dot_general(a, b, (((0,), (0,)), ((), ())), preferred_element_type=F32)


def _lower_bound_kernel(logit_ref, o_ref):
    z = logit_ref[...]
    n = z.shape[0]
    e = jnp.exp(z - jnp.max(z, axis=0, keepdims=True))
    p = e / jnp.sum(e, axis=0, keepdims=True)
    run = jnp.zeros_like(p[0:1])
    rows = [run]
    for k in range(1, n):
        run = run + p[k:k + 1]
        rows.append(run)
    o_ref[...] = jnp.clip(jnp.concatenate(rows, axis=0), 0.0, 1.0 - 1e-6)


def lower_bounds(logits):
    return pl.pallas_call(
        _lower_bound_kernel,
        out_shape=jax.ShapeDtypeStruct(logits.shape, F32),
        name="hgrn_lower_bounds",
    )(logits.astype(F32))


def _mod_kernel(c_ref, w_ref, b_ref, o_ref):
    s = _silu(c_ref[...]).astype(BF16)
    o_ref[0] = _dot(s, w_ref[0].astype(BF16)) + b_ref[0]


def modulation(c_pad, mod_w, mod_b, stride=1, tn=2048):
    n_layers, d, n = mod_w.shape
    rows = c_pad.shape[0]
    n_sel = -(-n_layers // stride)
    return pl.pallas_call(
        _mod_kernel,
        grid=(n_sel, n // tn),
        in_specs=[
            pl.BlockSpec((rows, d), lambda l, j: (0, 0)),
            pl.BlockSpec((1, d, tn), lambda l, j: (l * stride, 0, j)),
            pl.BlockSpec((1, 1, tn), lambda l, j: (l * stride, 0, j)),
        ],
        out_specs=pl.BlockSpec((1, rows, tn), lambda l, j: (l, 0, j)),
        out_shape=jax.ShapeDtypeStruct((n_sel, rows, n), F32),
        compiler_params=_params("parallel", "parallel"),
        name="adaln_modulation",
    )(c_pad, mod_w, mod_b.reshape(n_layers, 1, n))


def _norm_rows(x_ref, gain_ref, sh_ref, sc_ref, h_ref, copy_ref=None):
    col_scale = gain_ref[...] * (1.0 + sc_ref[0])
    shift = sh_ref[0]

    def body(r, carry):
        rows = pl.ds(pl.multiple_of(r * NORM_ROWS, NORM_ROWS), NORM_ROWS)
        x = x_ref[rows, :]
        inv = lax.rsqrt(jnp.mean(x * x, axis=-1, keepdims=True) + EPS)
        h_ref[rows, :] = (x * inv * col_scale + shift).astype(h_ref.dtype)
        if copy_ref is not None:
            copy_ref[rows, :] = x
        return carry

    lax.fori_loop(0, x_ref.shape[0] // NORM_ROWS, body, 0, unroll=NORM_UNROLL)


def _norm_proj_kernel(*refs, cast_side):
    if cast_side:
        (x_ref, gain_ref, sh_ref, sc_ref, w_ref, waux_ref, ca_ref, cb_ref,
         o_ref, oaux_ref, ca_o_ref, cb_o_ref, h_ref) = refs
    else:
        x_ref, gain_ref, sh_ref, sc_ref, w_ref, waux_ref, o_ref, oaux_ref, h_ref = refs

    @pl.when(pl.program_id(1) == 0)
    def _():
        _norm_rows(x_ref, gain_ref, sh_ref, sc_ref, h_ref)
        oaux_ref[...] = _dot(h_ref[...], waux_ref[...])

    h = h_ref[...]
    for c0 in range(0, o_ref.shape[1], PROJ_TC):
        cols = slice(c0, c0 + PROJ_TC)
        o_ref[:, cols] = _dot(h, w_ref[0, :, cols]).astype(o_ref.dtype)
    if cast_side:
        ca_o_ref[...] = ca_ref[0].astype(ca_o_ref.dtype)
        cb_o_ref[...] = cb_ref[0].astype(cb_o_ref.dtype)


def _side_cast_specs(mats, layer, n_i, n_j):
    a, b = mats
    ra, ca = a.shape[1] // n_i, a.shape[2] // n_j
    rb, cb = b.shape[1] // n_j, b.shape[2] // n_i
    in_specs = [pl.BlockSpec((1, ra, ca), lambda i, j: (layer, i, j)),
                pl.BlockSpec((1, rb, cb), lambda i, j: (layer, j, i))]
    out_specs = [pl.BlockSpec((ra, ca), lambda i, j: (i, j)),
                 pl.BlockSpec((rb, cb), lambda i, j: (j, i))]
    out_shape = [jax.ShapeDtypeStruct(a.shape[1:], BF16), jax.ShapeDtypeStruct(b.shape[1:], BF16)]
    return in_specs, out_specs, out_shape


def norm_proj(x, gain, shift, scale, w_all, layer, n, w_aux, seq, side_cast=None,
              tm=PROJ_TM, tn=PROJ_TN):
    m, d = x.shape
    tm = min(tm, seq)
    per_batch = seq // tm
    grid = (m // tm, n // tn)
    in_specs = [
        pl.BlockSpec((tm, d), lambda i, j: (i, 0)),
        pl.BlockSpec((1, d), lambda i, j: (0, 0)),
        pl.BlockSpec((1, 1, d), lambda i, j: (i // per_batch, 0, 0)),
        pl.BlockSpec((1, 1, d), lambda i, j: (i // per_batch, 0, 0)),
        pl.BlockSpec((1, d, tn), lambda i, j: (layer, 0, j)),
        pl.BlockSpec((d, LANES), lambda i, j: (0, 0)),
    ]
    out_specs = [
        pl.BlockSpec((tm, tn), lambda i, j: (i, j)),
        pl.BlockSpec((tm, LANES), lambda i, j: (i, 0)),
    ]
    out_shape = [
        jax.ShapeDtypeStruct((m, n), BF16),
        jax.ShapeDtypeStruct((m, LANES), F32),
    ]
    args = [x, gain.reshape(1, d), shift, scale, w_all, w_aux]
    if side_cast is not None:
        mats, cast_layer = side_cast
        extra_in, extra_out, extra_shape = _side_cast_specs(mats, cast_layer, *grid)
        in_specs += extra_in
        out_specs += extra_out
        out_shape += extra_shape
        args += list(mats)
    return pl.pallas_call(
        functools.partial(_norm_proj_kernel, cast_side=side_cast is not None),
        grid=grid,
        in_specs=in_specs,
        out_specs=out_specs,
        out_shape=out_shape,
        scratch_shapes=[pltpu.VMEM((tm, d), BF16)],
        compiler_params=_params("parallel", "arbitrary"),
        name="norm_in_proj",
    )(*args)


def _out_proj_kernel(*refs, n_lhs):
    lhs = refs[:n_lhs]
    w_ref, x_ref, gate_ref, o_ref = refs[n_lhs:]
    width = lhs[0].shape[1]
    y = _dot(lhs[0][...], w_ref[0, 0:width, :])
    for part, a_ref in enumerate(lhs[1:], 1):
        y = y + _dot(a_ref[...], w_ref[0, part * width:(part + 1) * width, :])
    o_ref[...] = x_ref[...] + gate_ref[0] * y


def out_proj_residual(lhs_list, w_all, layer, x, gate, seq, tm=OUT_TM):
    m, d = x.shape
    tm = min(tm, seq)
    per_batch = seq // tm
    n_lhs = len(lhs_list)
    width = lhs_list[0].shape[1]
    k = w_all.shape[1]
    assert all(a.shape[1] == width for a in lhs_list) and n_lhs * width == k
    in_specs = (
        [pl.BlockSpec((tm, width), lambda i: (i, 0)) for _ in lhs_list]
        + [pl.BlockSpec((1, k, d), lambda i: (layer, 0, 0), pipeline_mode=pl.Buffered(1)),
           pl.BlockSpec((tm, d), lambda i: (i, 0)),
           pl.BlockSpec((1, 1, d), lambda i: (i // per_batch, 0, 0))]
    )
    return pl.pallas_call(
        functools.partial(_out_proj_kernel, n_lhs=n_lhs),
        grid=(m // tm,),
        in_specs=in_specs,
        out_specs=pl.BlockSpec((tm, d), lambda i: (i, 0)),
        out_shape=jax.ShapeDtypeStruct((m, d), F32),
        compiler_params=_params("parallel"),
        name="out_proj_residual",
    )(*lhs_list, w_all, x, gate)


def _ffn_kernel(*refs, cast_next):
    if cast_next:
        (x_ref, gain_ref, sh_ref, sc_ref, gate_ref, wa_ref, wu_ref, wo_ref, nin_ref, nout_ref,
         o_ref, nin_o_ref, nout_o_ref, h_ref) = refs
    else:
        x_ref, gain_ref, sh_ref, sc_ref, gate_ref, wa_ref, wu_ref, wo_ref, o_ref, h_ref = refs

    @pl.when(pl.program_id(1) == 0)
    def _():
        _norm_rows(x_ref, gain_ref, sh_ref, sc_ref, h_ref, copy_ref=o_ref)

    h = h_ref[...]
    a = _dot(h, wa_ref[...])
    u = _dot(h, wu_ref[...])
    act = (_silu(a) * u).astype(BF16)
    d = o_ref.shape[1]
    for c0 in range(0, d, FFN_TN):
        cols = slice(c0, c0 + FFN_TN)
        o_ref[:, cols] += gate_ref[0][:, cols] * _dot(act, wo_ref[:, cols])
    if cast_next:
        nin_o_ref[...] = nin_ref[0].astype(nin_o_ref.dtype)
        nout_o_ref[...] = nout_ref[0].astype(nout_o_ref.dtype)


def ffn_residual(x, gain, shift, scale, gate, w_in, w_out, seq, next_weights=None,
                 tm=FFN_TM, tf=FFN_TF):
    m, d = x.shape
    d_ff = w_out.shape[0]
    tm = min(tm, seq)
    per_batch = seq // tm
    ni, nf = m // tm, d_ff // tf
    vec = pl.BlockSpec((1, 1, d), lambda i, f: (i // per_batch, 0, 0))
    in_specs = [
        pl.BlockSpec((tm, d), lambda i, f: (i, 0)),
        pl.BlockSpec((1, d), lambda i, f: (0, 0)),
        vec, vec, vec,
        pl.BlockSpec((d, tf), lambda i, f: (0, f)),
        pl.BlockSpec((d, tf), lambda i, f: (0, nf + f)),
        pl.BlockSpec((tf, d), lambda i, f: (f, 0)),
    ]
    out_specs = [pl.BlockSpec((tm, d), lambda i, f: (i, 0))]
    out_shape = [jax.ShapeDtypeStruct((m, d), F32)]
    args = [x, gain.reshape(1, d), shift, scale, gate, w_in, w_in, w_out]
    if next_weights is not None:
        nin, nout, nl = next_weights
        extra_in, extra_out, extra_shape = _side_cast_specs((nin, nout), nl, ni, nf)
        in_specs += extra_in
        out_specs += extra_out
        out_shape += extra_shape
        args += [nin, nout]
    outs = pl.pallas_call(
        functools.partial(_ffn_kernel, cast_next=next_weights is not None),
        grid=(ni, nf),
        in_specs=in_specs,
        out_specs=out_specs,
        out_shape=out_shape,
        scratch_shapes=[pltpu.VMEM((tm, d), BF16)],
        compiler_params=_params("parallel", "arbitrary"),
        name="swiglu_ffn_residual",
    )(*args)
    return outs if next_weights is not None else outs[0]


def _cumsum_rows(g, tril):
    n = g.shape[1]
    y = _dot(tril, jnp.concatenate(_split3(g), axis=1))
    return y[:, :n] + y[:, n:2 * n] + y[:, 2 * n:]


def _glr_heads(qs, ks, vs, cums, s_ref, sub):
    heads = range(len(qs))
    c, dk = qs[0].shape
    dv = vs[0].shape[1]
    vbs = [v.astype(BF16) for v in vs]
    lasts = [cum[c - 1:c, :] for cum in cums]

    inter = [_dot((qs[h] * jnp.exp2(cums[h])).astype(BF16), s_ref[h].astype(BF16)) for h in heads]
    updates = [_dot_tn((ks[h] * jnp.exp2(lasts[h] - cums[h])).astype(BF16), vbs[h]) for h in heads]
    for h in heads:
        decay_cols = jnp.transpose(jnp.broadcast_to(jnp.exp2(lasts[h]), (LANES, dk)))
        s_ref[h] = jnp.concatenate([decay_cols] * (dv // LANES), axis=1) * s_ref[h] + updates[h]

    spans = [(r0, r0 + sub) for r0 in range(0, c, sub)]
    scores = []
    for h in heads:
        q, k, cum = qs[h], ks[h], cums[h]
        per_head = []
        for r0, r1 in spans:
            hi = cum[r1 - 1:r1, :]
            mid = 0.5 * hi if r0 == 0 else 0.5 * (cum[r0 - 1:r0, :] + hi)
            qi = (q[r0:r1] * jnp.exp2(cum[r0:r1] - mid)).astype(BF16)
            ki = (k[:r1] * jnp.exp2(mid - cum[:r1])).astype(BF16)
            a = _dot_nt(qi, ki)
            row = r0 + lax.broadcasted_iota(jnp.int32, (sub, r1), 0)
            col = lax.broadcasted_iota(jnp.int32, (sub, r1), 1)
            per_head.append(jnp.where(col <= row, a, 0.0).astype(BF16))
        scores.append(per_head)
    outs = []
    for h in heads:
        blocks = [_dot(a, vbs[h][:r1]) for a, (_, r1) in zip(scores[h], spans)]
        intra = blocks[0] if len(blocks) == 1 else jnp.concatenate(blocks, axis=0)
        outs.append(inter[h] + intra)
    return outs


def _tril_bf16(c):
    row = lax.broadcasted_iota(jnp.int32, (c, c), 0)
    col = lax.broadcasted_iota(jnp.int32, (c, c), 1)
    return jnp.where(col <= row, 1.0, 0.0).astype(BF16)


def _hgrn_kernel(*refs, chunk, sub, mod_splits):
    if mod_splits:
        (q_ref, f_ref, i_ref, g_ref, lb_ref, gain_ref, c_ref, mw_ref, mb_ref,
         o_ref, mo_ref, s_ref) = refs
    else:
        q_ref, f_ref, i_ref, g_ref, lb_ref, gain_ref, o_ref, s_ref = refs

    @pl.when(pl.program_id(1) == 0)
    def _():
        s_ref[...] = jnp.zeros_like(s_ref)

    if mod_splits:
        step = pl.program_id(0) * pl.num_programs(1) + pl.program_id(1)
        row_block = step // mod_splits
        width = mo_ref.shape[1] // mod_splits
        cols = pl.ds(pl.multiple_of((step % mod_splits) * width, LANES), width)
        part = _dot(_silu(c_ref[...]).astype(BF16), mw_ref[0].astype(BF16))

        @pl.when(row_block == 0)
        def _():
            mo_ref[:, cols] = mb_ref[0, :, cols] + part

        @pl.when(row_block > 0)
        def _():
            mo_ref[:, cols] += part

    heads, dk, _ = s_ref.shape
    tril = _tril_bf16(chunk)

    def body(ci, carry):
        rows = pl.ds(pl.multiple_of(ci * chunk, chunk), chunk)
        lb = lb_ref[...]
        sig = _sigmoid(f_ref[rows, :].astype(F32))
        forget = lb + (1.0 - lb) * sig
        cum = _cumsum_rows(jnp.log2(jnp.maximum(forget, MIN_FORGET)), tril)
        key = (1.0 - lb) * (1.0 - sig)
        q = _silu(q_ref[rows, :].astype(F32))
        v = i_ref[rows, :]
        cols = [slice(hh * dk, (hh + 1) * dk) for hh in range(heads)]
        outs = _glr_heads([q[:, cs] for cs in cols], [key[:, cs] for cs in cols],
                          [v[:, cs] for cs in cols], [cum[:, cs] for cs in cols], s_ref, sub)
        for o, cs in zip(outs, cols):
            o = o * lax.rsqrt(jnp.mean(o * o, axis=-1, keepdims=True) + EPS) * gain_ref[:, cs]
            o_ref[rows, cs] = (o * _silu(g_ref[rows, cs].astype(F32))).astype(o_ref.dtype)
        return carry

    lax.fori_loop(0, q_ref.shape[0] // chunk, body, 0, unroll=4)


def hgrn2_heads(proj, lb, out_gain, batch, seq, mod_next=None, tb=REC_TB, chunk=REC_CHUNK,
                sub=HGRN_SUB):
    m = proj.shape[0]
    tb = min(tb, seq)
    nt = seq // tb
    h, dk = A_HEADS, A_DK
    w = h * dk

    def col(group):
        return pl.BlockSpec((tb, w), lambda b, t: (b * nt + t, group))

    vec = pl.BlockSpec((1, w), lambda b, t: (0, 0))
    in_specs = [col(0), col(1), col(2), col(3), vec, vec]
    out_specs = [pl.BlockSpec((tb, w), lambda b, t: (b * nt + t, 0))]
    out_shape = [jax.ShapeDtypeStruct((m, w), BF16)]
    args = [proj, proj, proj, proj, lb.reshape(1, w), out_gain.reshape(1, w)]
    splits = 0
    if mod_next is not None:
        c_pad, mod_w, mod_b, nl = mod_next
        d, n = mod_w.shape[1], mod_w.shape[2]
        steps = batch * nt
        row_blocks = min(steps, d // LANES)
        splits = steps // row_blocks
        block_rows = d // row_blocks
        assert steps == row_blocks * splits and d == block_rows * row_blocks
        assert n % (splits * LANES) == 0
        in_specs += [
            pl.BlockSpec((c_pad.shape[0], block_rows), lambda b, t: (0, (b * nt + t) // splits)),
            pl.BlockSpec((1, block_rows, n // splits),
                         lambda b, t: (nl, (b * nt + t) // splits, (b * nt + t) % splits)),
            pl.BlockSpec((1, 1, n), lambda b, t: (nl, 0, 0)),
        ]
        out_specs += [pl.BlockSpec((c_pad.shape[0], n), lambda b, t: (0, 0))]
        out_shape += [jax.ShapeDtypeStruct((c_pad.shape[0], n), F32)]
        args += [c_pad, mod_w, mod_b]
    outs = pl.pallas_call(
        functools.partial(_hgrn_kernel, chunk=chunk, sub=sub, mod_splits=splits),
        grid=(batch, nt),
        in_specs=in_specs,
        out_specs=out_specs,
        out_shape=out_shape,
        scratch_shapes=[pltpu.VMEM((h, dk, dk), F32)],
        compiler_params=_params("arbitrary" if splits else "parallel", "arbitrary"),
        name="hgrn2_recurrence",
    )(*args)
    return outs if splits else outs[0]


def _gla_kernel(q_ref, k_ref, v_ref, g_ref, low_ref, wup_ref, bias_ref, gain_ref, o_ref, s_ref,
                *, chunk, sub):
    @pl.when(pl.program_id(1) == 0)
    def _():
        s_ref[...] = jnp.zeros_like(s_ref)

    heads, dk, dv = s_ref.shape
    tril = _tril_bf16(chunk)
    q_scale = dk ** -0.5

    def body(ci, carry):
        rows = pl.ds(pl.multiple_of(ci * chunk, chunk), chunk)
        z = _dot(low_ref[rows, :].astype(BF16), wup_ref[...]) + bias_ref[...]
        cum = _cumsum_rows(_log_sigmoid(z) * (LOG2E / GLA_GATE_NORMALIZER), tril)
        kcs = [slice(hh * dk, (hh + 1) * dk) for hh in range(heads)]
        vcs = [slice(hh * dv, (hh + 1) * dv) for hh in range(heads)]
        outs = _glr_heads([q_ref[rows, kc].astype(F32) * q_scale for kc in kcs],
                          [k_ref[rows, kc].astype(F32) for kc in kcs],
                          [v_ref[rows, vc] for vc in vcs], [cum[:, kc] for kc in kcs], s_ref, sub)
        for o, vc in zip(outs, vcs):
            o = o * lax.rsqrt(jnp.mean(o * o, axis=-1, keepdims=True) + EPS) * gain_ref[...]
            o_ref[rows, vc] = (o * _silu(g_ref[rows, vc].astype(F32))).astype(o_ref.dtype)
        return carry

    lax.fori_loop(0, q_ref.shape[0] // chunk, body, 0)


def gla_heads(proj, low, w_up, b_gate, out_gain, batch, seq, tb=REC_TB, chunk=GLA_CHUNK,
              sub=GLA_SUB):
    m = proj.shape[0]
    tb = min(tb, seq)
    nt = seq // tb
    h, dk, dv = C_HEADS, C_DK, C_DV
    kw, vw = h * dk, h * dv
    v0 = 2 * kw // vw
    return pl.pallas_call(
        functools.partial(_gla_kernel, chunk=chunk, sub=sub),
        grid=(batch, nt),
        in_specs=[
            pl.BlockSpec((tb, kw), lambda b, t: (b * nt + t, 0)),
            pl.BlockSpec((tb, kw), lambda b, t: (b * nt + t, 1)),
            pl.BlockSpec((tb, vw), lambda b, t: (b * nt + t, v0)),
            pl.BlockSpec((tb, vw), lambda b, t: (b * nt + t, v0 + 1)),
            pl.BlockSpec((tb, LANES), lambda b, t: (b * nt + t, 0)),
            pl.BlockSpec((LANES, kw), lambda b, t: (0, 0)),
            pl.BlockSpec((1, kw), lambda b, t: (0, 0)),
            pl.BlockSpec((1, dv), lambda b, t: (0, 0)),
        ],
        out_specs=pl.BlockSpec((tb, vw), lambda b, t: (b * nt + t, 0)),
        out_shape=jax.ShapeDtypeStruct((m, vw), BF16),
        scratch_shapes=[pltpu.VMEM((h, dk, dv), F32)],
        compiler_params=_params("parallel", "arbitrary"),
        name="gla_recurrence",
    )(proj, proj, proj, proj, low, w_up, b_gate.reshape(1, kw), out_gain.reshape(1, dv))


F_PIECES = 3


def _fox_prep_kernel(q_ref, k_ref, logit_ref, bias_ref, qg_ref, kg_ref,
                     qo_ref, ko_ref, carry_ref, *, heads, head_dim):
    @pl.when(pl.program_id(1) == 0)
    def _():
        carry_ref[...] = jnp.zeros_like(carry_ref)

    tb = q_ref.shape[0]
    log_f = _log_sigmoid(logit_ref[...] + bias_ref[...])
    cum = _cumsum_rows(log_f, _tril_bf16(tb)) + carry_ref[0:1, :]
    carry_ref[...] = jnp.broadcast_to(cum[tb - 1:tb, :], carry_ref.shape)
    pieces = jnp.concatenate(_split3(-LOG2E * cum), axis=1)
    src = lax.broadcasted_iota(jnp.int32, (F_PIECES * LANES, heads * head_dim), 0)
    dst = lax.broadcasted_iota(jnp.int32, (F_PIECES * LANES, heads * head_dim), 1)
    select = jnp.where((dst // head_dim == src % LANES) & (dst % head_dim == src // LANES), 1.0, 0.0)
    bias_all = _dot(pieces, select.astype(BF16))
    lane = lax.broadcasted_iota(jnp.int32, (tb, head_dim), 1)
    one_cols = jnp.where(lane < F_PIECES, 1.0, 0.0).astype(qo_ref.dtype)

    qg = qg_ref[...] * (head_dim ** -0.5 * LOG2E)
    kg = kg_ref[...]
    for hh in range(heads):
        cols = slice(hh * head_dim, (hh + 1) * head_dim)
        lo = 2 * hh * head_dim
        main = slice(lo, lo + head_dim)
        extra = slice(lo + head_dim, lo + 2 * head_dim)
        q = q_ref[:, cols].astype(F32)
        qo_ref[:, main] = (q * lax.rsqrt(jnp.mean(q * q, axis=-1, keepdims=True) + EPS) * qg
                           ).astype(qo_ref.dtype)
        qo_ref[:, extra] = one_cols
        k = k_ref[:, cols].astype(F32)
        ko_ref[:, main] = (k * lax.rsqrt(jnp.mean(k * k, axis=-1, keepdims=True) + EPS) * kg
                           ).astype(ko_ref.dtype)
        ko_ref[:, extra] = bias_all[:, cols].astype(ko_ref.dtype)


def fox_prep(proj, logits, f_bias, q_gain, k_gain, batch, seq, tb=PREP_TB):
    m = proj.shape[0]
    h, hd = B_HEADS, B_HEAD_DIM
    w = h * hd
    tb = min(tb, seq)
    nt = seq // tb
    base = (4 * A_HEADS * A_DK) // w
    bias = jnp.zeros((1, LANES), F32).at[0, :h].set(f_bias)

    def col(group):
        return pl.BlockSpec((tb, w), lambda b, t: (b * nt + t, base + group))

    wide_tok = pl.BlockSpec((tb, 2 * w), lambda b, t: (b * nt + t, 0))
    return pl.pallas_call(
        functools.partial(_fox_prep_kernel, heads=h, head_dim=hd),
        grid=(batch, nt),
        in_specs=[
            col(0), col(1),
            pl.BlockSpec((tb, LANES), lambda b, t: (b * nt + t, 0)),
            pl.BlockSpec((1, LANES), lambda b, t: (0, 0)),
            pl.BlockSpec((1, hd), lambda b, t: (0, 0)),
            pl.BlockSpec((1, hd), lambda b, t: (0, 0)),
        ],
        out_specs=[wide_tok, wide_tok],
        out_shape=[jax.ShapeDtypeStruct((m, 2 * w), BF16)] * 2,
        scratch_shapes=[pltpu.VMEM((8, LANES), F32)],
        compiler_params=_params("parallel", "arbitrary"),
        name="fox_prep",
    )(proj, proj, logits, bias, q_gain.reshape(1, hd), k_gain.reshape(1, hd))


def _fox_attn_kernel(q_ref, k_ref, v_ref, g_ref, o_ref, m_ref, acc_ref, *, tq, tk, heads, head_dim):
    qi = pl.program_id(2)
    q_start = pl.multiple_of(qi * tq, tq)
    wide = 2 * head_dim

    def tile(start, mask, width=tk, first=False):
        rows = pl.ds(start, width)
        ones = jnp.ones((width, head_dim), BF16)
        scores = [_dot_nt(q_ref[:, hh * wide:(hh + 1) * wide], k_ref[rows, hh * wide:(hh + 1) * wide])
                  for hh in range(heads)]
        for hh, s in enumerate(scores):
            if mask is not None:
                s = jnp.where(mask, s, MASK_VALUE)
            row_max = jnp.max(s, axis=-1, keepdims=True)
            if first:
                m_new = jnp.broadcast_to(row_max, (tq, LANES))
            else:
                m_prev = m_ref[hh]
                m_new = jnp.maximum(m_prev, row_max)
            p = jnp.exp2(s - jnp.concatenate([m_new] * (width // LANES), axis=1))
            m_ref[hh] = m_new
            v_aug = jnp.concatenate([v_ref[rows, hh * head_dim:(hh + 1) * head_dim], ones], axis=1)
            pv = _dot(p.astype(BF16), v_aug)
            if first:
                acc_ref[hh] = pv
            else:
                alpha = jnp.exp2(m_prev - m_new)
                acc_ref[hh] = jnp.concatenate([alpha] * (wide // LANES), axis=1) * acc_ref[hh] + pv

    for d in range(tq // tk):
        row = lax.broadcasted_iota(jnp.int32, (tq, tk), 0)
        col = lax.broadcasted_iota(jnp.int32, (tq, tk), 1) + d * tk
        tile(pl.multiple_of(q_start + d * tk, tk), col <= row, first=(d == 0))

    n_full = qi * (tq // tk)

    def quad(gi, carry):
        tile(pl.multiple_of(4 * gi * tk, 4 * tk), None, width=4 * tk)
        return carry

    lax.fori_loop(0, n_full // 4, quad, 0)

    @pl.when(n_full % 4 >= 2)
    def _():
        tile(pl.multiple_of((n_full // 4) * 4 * tk, 2 * tk), None, width=2 * tk)

    @pl.when(n_full % 2 == 1)
    def _():
        tile(pl.multiple_of((n_full - 1) * tk, tk), None)

    for hh in range(heads):
        acc = acc_ref[hh]
        cols = slice(hh * head_dim, (hh + 1) * head_dim)
        o_ref[:, cols] = (acc[:, :head_dim] / acc[:, head_dim:]
                          * _sigmoid(g_ref[:, cols].astype(F32))).astype(o_ref.dtype)


def fox_attention(q_aug, k_aug, proj, batch, seq, tq=ATT_TQ, tk=ATT_TK, group=ATT_HEADS):
    m = q_aug.shape[0]
    h, hd = B_HEADS, B_HEAD_DIM
    tq = min(tq, seq)
    tk = min(tk, tq)
    nq = seq // tq
    v_block = (4 * A_HEADS * A_DK + 2 * h * hd) // (group * hd)
    gate_block = v_block + h // group
    return pl.pallas_call(
        functools.partial(_fox_attn_kernel, tq=tq, tk=tk, heads=group, head_dim=hd),
        grid=(batch, h // group, nq),
        in_specs=[
            pl.BlockSpec((tq, 2 * group * hd), lambda b, hg, i: (b * nq + i, hg)),
            pl.BlockSpec((seq, 2 * group * hd), lambda b, hg, i: (b, hg),
                         pipeline_mode=pl.Buffered(1)),
            pl.BlockSpec((seq, group * hd), lambda b, hg, i: (b, v_block + hg),
                         pipeline_mode=pl.Buffered(1)),
            pl.BlockSpec((tq, group * hd), lambda b, hg, i: (b * nq + i, gate_block + hg)),
        ],
        out_specs=pl.BlockSpec((tq, group * hd), lambda b, hg, i: (b * nq + i, hg)),
        out_shape=jax.ShapeDtypeStruct((m, h * hd), BF16),
        scratch_shapes=[pltpu.VMEM((group, tq, LANES), F32), pltpu.VMEM((group, tq, 2 * hd), F32)],
        compiler_params=_params("parallel", "parallel", "arbitrary"),
        name="fox_attention",
    )(q_aug, k_aug, proj, proj)


def _pad_cols(w, n):
    return jnp.zeros((w.shape[0], n), w.dtype).at[:, :w.shape[1]].set(w)


def kernel(x, c, mod_w, mod_b, norm_mix_gain, norm_ffn_gain, ab_w_in, ab_w_out, hgrn_lb_logits,
           hgrn_out_gain, fox_q_gain, fox_k_gain, fox_f_bias, gla_w_in, gla_w_gate_up, gla_b_gate,
           gla_out_gain, gla_w_out, ffn_w_in, ffn_w_out):
    batch, seq, d = x.shape
    depth = mod_w.shape[0]
    ab_main = 4 * A_HEADS * A_DK + 4 * B_HEADS * B_HEAD_DIM
    c_main = 2 * C_HEADS * C_DK + 2 * C_HEADS * C_DV

    lbs = lower_bounds(hgrn_lb_logits)
    c_pad = jnp.zeros((8, d), F32).at[:batch].set(c)
    mod_b3 = mod_b.reshape(depth, 1, mod_b.shape[1])
    mod_even = modulation(c_pad, mod_w, mod_b, stride=2)
    mod_odd = None
    xs = x.reshape(batch * seq, d)
    ab_w_in_b, ab_w_out_b = ab_w_in.astype(BF16), ab_w_out.astype(BF16)
    gla_w_in_b, gla_w_out_b = gla_w_in.astype(BF16), gla_w_out.astype(BF16)
    ffn_w_in_b = ffn_w_out_b = None

    for layer in range(depth):
        mod = mod_even[layer // 2] if layer % 2 == 0 else mod_odd
        sh1, sc1, g1, sh2, sc2, g2 = [
            mod[:batch, i * d:(i + 1) * d].reshape(batch, 1, d) for i in range(6)]
        if layer % 2 == 0:
            i = layer // 2
            w_aux = _pad_cols(ab_w_in[i, :, ab_main:], LANES).astype(BF16)
            if layer == 0:
                proj, logits, ffn_w_in_b, ffn_w_out_b = norm_proj(
                    xs, norm_mix_gain[layer], sh1, sc1, ab_w_in_b, i, ab_main, w_aux, seq,
                    side_cast=((ffn_w_in, ffn_w_out), 0))
            else:
                proj, logits = norm_proj(xs, norm_mix_gain[layer], sh1, sc1, ab_w_in_b, i, ab_main,
                                         w_aux, seq)
            if layer + 1 < depth:
                o_a, mod_odd = hgrn2_heads(proj, lbs[i], hgrn_out_gain[i], batch, seq,
                                           mod_next=(c_pad, mod_w, mod_b3, layer + 1))
            else:
                o_a = hgrn2_heads(proj, lbs[i], hgrn_out_gain[i], batch, seq)
            q_aug, k_aug = fox_prep(proj, logits, fox_f_bias[i], fox_q_gain[i], fox_k_gain[i],
                                    batch, seq)
            o_b = fox_attention(q_aug, k_aug, proj, batch, seq)
            xs = out_proj_residual([o_a, o_b], ab_w_out_b, i, xs, g1, seq)
        else:
            j = layer // 2
            proj, low = norm_proj(xs, norm_mix_gain[layer], sh1, sc1, gla_w_in_b, j, c_main,
                                  _pad_cols(gla_w_in[j, :, c_main:], LANES).astype(BF16), seq)
            w_up = jnp.zeros((LANES, gla_w_gate_up.shape[2]), BF16).at[:C_GATE_RANK].set(
                gla_w_gate_up[j].astype(BF16))
            o_c = gla_heads(proj, low, w_up, gla_b_gate[j], gla_out_gain[j], batch, seq)
            xs = out_proj_residual([o_c], gla_w_out_b, j, xs, g1, seq)
        if layer + 1 < depth:
            xs, ffn_w_in_b, ffn_w_out_b = ffn_residual(
                xs, norm_ffn_gain[layer], sh2, sc2, g2, ffn_w_in_b, ffn_w_out_b, seq,
                next_weights=(ffn_w_in, ffn_w_out, layer + 1))
        else:
            xs = ffn_residual(xs, norm_ffn_gain[layer], sh2, sc2, g2, ffn_w_in_b, ffn_w_out_b, seq)
    return xs.reshape(batch, seq, d)
```

```python
import functools

import jax
import jax.numpy as jnp
from jax import lax
from jax.experimental import pallas as pl
from jax.experimental.pallas import tpu as pltpu

F32 = jnp.float32
BF16 = jnp.bfloat16

EPS = 1e-6
MIN_FORGET = 1e-30
MASK_VALUE = -1e30
GLA_GATE_NORMALIZER = 16.0
LOG2E = 1.4426950408889634

LANES = 128
VMEM_LIMIT = 60 * 1024 * 1024

A_HEADS = 8
A_DK = 128
B_HEADS = 8
B_HEAD_DIM = 128
C_HEADS = 4
C_DK = 256
C_DV = 512
C_GATE_RANK = 16

PROJ_TM = 1024
PROJ_TN = 2048
PROJ_TC = 512
OUT_TM = 512
FFN_TM = 1024
FFN_TF = 512
FFN_TN = 512
NORM_ROWS = 16
NORM_UNROLL = 8
REC_TB = 512
REC_CHUNK = 64
HGRN_SUB = 16
GLA_CHUNK = 256
GLA_SUB = 64
ATT_TQ = 512
ATT_TK = 512
ATT_HEADS = 4
PREP_TB = 512


def _params(*sem):
    return pltpu.CompilerParams(dimension_semantics=sem, vmem_limit_bytes=VMEM_LIMIT)


def _sigmoid(x):
    return 1.0 / (1.0 + jnp.exp2(x * (-LOG2E)))


def _silu(x):
    return x * _sigmoid(x)


def _log_sigmoid(x):
    return jnp.minimum(x, 0.0) - jnp.log(1.0 + jnp.exp(-jnp.abs(x)))


def _split3(x):
    hi = x.astype(BF16)
    r = x - hi.astype(F32)
    mid = r.astype(BF16)
    lo = (r - mid.astype(F32)).astype(BF16)
    return hi, mid, lo


def _dot(a, b):
    return jnp.dot(a, b, preferred_element_type=F32)


def _dot_nt(a, b):
    return lax.dot_general(a, b, (((1,), (1,)), ((), ())), preferred_element_type=F32)


def _dot_tn(a, b):
    return lax.dot_general(a, b, (((0,), (0,)), ((), ())), preferred_element_type=F32)


def _lower_bound_kernel(logit_ref, o_ref):
    z = logit_ref[...]
    n = z.shape[0]
    e = jnp.exp(z - jnp.max(z, axis=0, keepdims=True))
    p = e / jnp.sum(e, axis=0, keepdims=True)
    run = jnp.zeros_like(p[0:1])
    rows = [run]
    for k in range(1, n):
        run = run + p[k:k + 1]
        rows.append(run)
    o_ref[...] = jnp.clip(jnp.concatenate(rows, axis=0), 0.0, 1.0 - 1e-6)


def lower_bounds(logits):
    return pl.pallas_call(
        _lower_bound_kernel,
        out_shape=jax.ShapeDtypeStruct(logits.shape, F32),
        name="hgrn_lower_bounds",
    )(logits.astype(F32))


def _mod_kernel(c_ref, w_ref, b_ref, o_ref):
    s = _silu(c_ref[...]).astype(BF16)
    o_ref[0] = _dot(s, w_ref[0].astype(BF16)) + b_ref[0]


def modulation(c_pad, mod_w, mod_b, stride=1, tn=2048):
    n_layers, d, n = mod_w.shape
    rows = c_pad.shape[0]
    n_sel = -(-n_layers // stride)
    return pl.pallas_call(
        _mod_kernel,
        grid=(n_sel, n // tn),
        in_specs=[
            pl.BlockSpec((rows, d), lambda l, j: (0, 0)),
            pl.BlockSpec((1, d, tn), lambda l, j: (l * stride, 0, j)),
            pl.BlockSpec((1, 1, tn), lambda l, j: (l * stride, 0, j)),
        ],
        out_specs=pl.BlockSpec((1, rows, tn), lambda l, j: (l, 0, j)),
        out_shape=jax.ShapeDtypeStruct((n_sel, rows, n), F32),
        compiler_params=_params("parallel", "parallel"),
        name="adaln_modulation",
    )(c_pad, mod_w, mod_b.reshape(n_layers, 1, n))


def _norm_rows(x_ref, gain_ref, sh_ref, sc_ref, h_ref, copy_ref=None):
    col_scale = gain_ref[...] * (1.0 + sc_ref[0])
    shift = sh_ref[0]

    def body(r, carry):
        rows = pl.ds(pl.multiple_of(r * NORM_ROWS, NORM_ROWS), NORM_ROWS)
        x = x_ref[rows, :]
        inv = lax.rsqrt(jnp.mean(x * x, axis=-1, keepdims=True) + EPS)
        h_ref[rows, :] = (x * inv * col_scale + shift).astype(h_ref.dtype)
        if copy_ref is not None:
            copy_ref[rows, :] = x
        return carry

    lax.fori_loop(0, x_ref.shape[0] // NORM_ROWS, body, 0, unroll=NORM_UNROLL)


def _norm_proj_kernel(*refs, cast_side):
    if cast_side:
        (x_ref, gain_ref, sh_ref, sc_ref, w_ref, waux_ref, ca_ref, cb_ref,
         o_ref, oaux_ref, ca_o_ref, cb_o_ref, h_ref) = refs
    else:
        x_ref, gain_ref, sh_ref, sc_ref, w_ref, waux_ref, o_ref, oaux_ref, h_ref = refs

    @pl.when(pl.program_id(1) == 0)
    def _():
        _norm_rows(x_ref, gain_ref, sh_ref, sc_ref, h_ref)
        oaux_ref[...] = _dot(h_ref[...], waux_ref[...])

    h = h_ref[...]
    for c0 in range(0, o_ref.shape[1], PROJ_TC):
        cols = slice(c0, c0 + PROJ_TC)
        o_ref[:, cols] = _dot(h, w_ref[0, :, cols]).astype(o_ref.dtype)
    if cast_side:
        ca_o_ref[...] = ca_ref[0].astype(ca_o_ref.dtype)
        cb_o_ref[...] = cb_ref[0].astype(cb_o_ref.dtype)


def _side_cast_specs(mats, layer, n_i, n_j):
    a, b = mats
    ra, ca = a.shape[1] // n_i, a.shape[2] // n_j
    rb, cb = b.shape[1] // n_j, b.shape[2] // n_i
    in_specs = [pl.BlockSpec((1, ra, ca), lambda i, j: (layer, i, j)),
                pl.BlockSpec((1, rb, cb), lambda i, j: (layer, j, i))]
    out_specs = [pl.BlockSpec((ra, ca), lambda i, j: (i, j)),
                 pl.BlockSpec((rb, cb), lambda i, j: (j, i))]
    out_shape = [jax.ShapeDtypeStruct(a.shape[1:], BF16), jax.ShapeDtypeStruct(b.shape[1:], BF16)]
    return in_specs, out_specs, out_shape


def norm_proj(x, gain, shift, scale, w_all, layer, n, w_aux, seq, side_cast=None,
              tm=PROJ_TM, tn=PROJ_TN):
    m, d = x.shape
    tm = min(tm, seq)
    per_batch = seq // tm
    grid = (m // tm, n // tn)
    in_specs = [
        pl.BlockSpec((tm, d), lambda i, j: (i, 0)),
        pl.BlockSpec((1, d), lambda i, j: (0, 0)),
        pl.BlockSpec((1, 1, d), lambda i, j: (i // per_batch, 0, 0)),
        pl.BlockSpec((1, 1, d), lambda i, j: (i // per_batch, 0, 0)),
        pl.BlockSpec((1, d, tn), lambda i, j: (layer, 0, j)),
        pl.BlockSpec((d, LANES), lambda i, j: (0, 0)),
    ]
    out_specs = [
        pl.BlockSpec((tm, tn), lambda i, j: (i, j)),
        pl.BlockSpec((tm, LANES), lambda i, j: (i, 0)),
    ]
    out_shape = [
        jax.ShapeDtypeStruct((m, n), BF16),
        jax.ShapeDtypeStruct((m, LANES), F32),
    ]
    args = [x, gain.reshape(1, d), shift, scale, w_all, w_aux]
    if side_cast is not None:
        mats, cast_layer = side_cast
        extra_in, extra_out, extra_shape = _side_cast_specs(mats, cast_layer, *grid)
        in_specs += extra_in
        out_specs += extra_out
        out_shape += extra_shape
        args += list(mats)
    return pl.pallas_call(
        functools.partial(_norm_proj_kernel, cast_side=side_cast is not None),
        grid=grid,
        in_specs=in_specs,
        out_specs=out_specs,
        out_shape=out_shape,
        scratch_shapes=[pltpu.VMEM((tm, d), BF16)],
        compiler_params=_params("parallel", "arbitrary"),
        name="norm_in_proj",
    )(*args)


def _out_proj_kernel(*refs, n_lhs):
    lhs = refs[:n_lhs]
    w_ref, x_ref, gate_ref, o_ref = refs[n_lhs:]
    width = lhs[0].shape[1]
    y = _dot(lhs[0][...], w_ref[0, 0:width, :])
    for part, a_ref in enumerate(lhs[1:], 1):
        y = y + _dot(a_ref[...], w_ref[0, part * width:(part + 1) * width, :])
    o_ref[...] = x_ref[...] + gate_ref[0] * y


def out_proj_residual(lhs_list, w_all, layer, x, gate, seq, tm=OUT_TM):
    m, d = x.shape
    tm = min(tm, seq)
    per_batch = seq // tm
    n_lhs = len(lhs_list)
    width = lhs_list[0].shape[1]
    k = w_all.shape[1]
    assert all(a.shape[1] == width for a in lhs_list) and n_lhs * width == k
    in_specs = (
        [pl.BlockSpec((tm, width), lambda i: (i, 0)) for _ in lhs_list]
        + [pl.BlockSpec((1, k, d), lambda i: (layer, 0, 0), pipeline_mode=pl.Buffered(1)),
           pl.BlockSpec((tm, d), lambda i: (i, 0)),
           pl.BlockSpec((1, 1, d), lambda i: (i // per_batch, 0, 0))]
    )
    return pl.pallas_call(
        functools.partial(_out_proj_kernel, n_lhs=n_lhs),
        grid=(m // tm,),
        in_specs=in_specs,
        out_specs=pl.BlockSpec((tm, d), lambda i: (i, 0)),
        out_shape=jax.ShapeDtypeStruct((m, d), F32),
        compiler_params=_params("parallel"),
        name="out_proj_residual",
    )(*lhs_list, w_all, x, gate)


def _ffn_kernel(*refs, cast_next):
    if cast_next:
        (x_ref, gain_ref, sh_ref, sc_ref, gate_ref, wa_ref, wu_ref, wo_ref, nin_ref, nout_ref,
         o_ref, nin_o_ref, nout_o_ref, h_ref) = refs
    else:
        x_ref, gain_ref, sh_ref, sc_ref, gate_ref, wa_ref, wu_ref, wo_ref, o_ref, h_ref = refs

    @pl.when(pl.program_id(1) == 0)
    def _():
        _norm_rows(x_ref, gain_ref, sh_ref, sc_ref, h_ref, copy_ref=o_ref)

    h = h_ref[...]
    a = _dot(h, wa_ref[...])
    u = _dot(h, wu_ref[...])
    act = (_silu(a) * u).astype(BF16)
    d = o_ref.shape[1]
    for c0 in range(0, d, FFN_TN):
        cols = slice(c0, c0 + FFN_TN)
        o_ref[:, cols] += gate_ref[0][:, cols] * _dot(act, wo_ref[:, cols])
    if cast_next:
        nin_o_ref[...] = nin_ref[0].astype(nin_o_ref.dtype)
        nout_o_ref[...] = nout_ref[0].astype(nout_o_ref.dtype)


def ffn_residual(x, gain, shift, scale, gate, w_in, w_out, seq, next_weights=None,
                 tm=FFN_TM, tf=FFN_TF):
    m, d = x.shape
    d_ff = w_out.shape[0]
    tm = min(tm, seq)
    per_batch = seq // tm
    ni, nf = m // tm, d_ff // tf
    vec = pl.BlockSpec((1, 1, d), lambda i, f: (i // per_batch, 0, 0))
    in_specs = [
        pl.BlockSpec((tm, d), lambda i, f: (i, 0)),
        pl.BlockSpec((1, d), lambda i, f: (0, 0)),
        vec, vec, vec,
        pl.BlockSpec((d, tf), lambda i, f: (0, f)),
        pl.BlockSpec((d, tf), lambda i, f: (0, nf + f)),
        pl.BlockSpec((tf, d), lambda i, f: (f, 0)),
    ]
    out_specs = [pl.BlockSpec((tm, d), lambda i, f: (i, 0))]
    out_shape = [jax.ShapeDtypeStruct((m, d), F32)]
    args = [x, gain.reshape(1, d), shift, scale, gate, w_in, w_in, w_out]
    if next_weights is not None:
        nin, nout, nl = next_weights
        extra_in, extra_out, extra_shape = _side_cast_specs((nin, nout), nl, ni, nf)
        in_specs += extra_in
        out_specs += extra_out
        out_shape += extra_shape
        args += [nin, nout]
    outs = pl.pallas_call(
        functools.partial(_ffn_kernel, cast_next=next_weights is not None),
        grid=(ni, nf),
        in_specs=in_specs,
        out_specs=out_specs,
        out_shape=out_shape,
        scratch_shapes=[pltpu.VMEM((tm, d), BF16)],
        compiler_params=_params("parallel", "arbitrary"),
        name="swiglu_ffn_residual",
    )(*args)
    return outs if next_weights is not None else outs[0]


def _cumsum_rows(g, tril):
    n = g.shape[1]
    y = _dot(tril, jnp.concatenate(_split3(g), axis=1))
    return y[:, :n] + y[:, n:2 * n] + y[:, 2 * n:]


def _glr_heads(qs, ks, vs, cums, s_ref, sub):
    heads = range(len(qs))
    c, dk = qs[0].shape
    dv = vs[0].shape[1]
    vbs = [v.astype(BF16) for v in vs]
    lasts = [cum[c - 1:c, :] for cum in cums]

    inter = [_dot((qs[h] * jnp.exp2(cums[h])).astype(BF16), s_ref[h].astype(BF16)) for h in heads]
    updates = [_dot_tn((ks[h] * jnp.exp2(lasts[h] - cums[h])).astype(BF16), vbs[h]) for h in heads]
    for h in heads:
        decay_cols = jnp.transpose(jnp.broadcast_to(jnp.exp2(lasts[h]), (LANES, dk)))
        s_ref[h] = jnp.concatenate([decay_cols] * (dv // LANES), axis=1) * s_ref[h] + updates[h]

    spans = [(r0, r0 + sub) for r0 in range(0, c, sub)]
    scores = []
    for h in heads:
        q, k, cum = qs[h], ks[h], cums[h]
        per_head = []
        for r0, r1 in spans:
            hi = cum[r1 - 1:r1, :]
            mid = 0.5 * hi if r0 == 0 else 0.5 * (cum[r0 - 1:r0, :] + hi)
            qi = (q[r0:r1] * jnp.exp2(cum[r0:r1] - mid)).astype(BF16)
            ki = (k[:r1] * jnp.exp2(mid - cum[:r1])).astype(BF16)
            a = _dot_nt(qi, ki)
            row = r0 + lax.broadcasted_iota(jnp.int32, (sub, r1), 0)
            col = lax.broadcasted_iota(jnp.int32, (sub, r1), 1)
            per_head.append(jnp.where(col <= row, a, 0.0).astype(BF16))
        scores.append(per_head)
    outs = []
    for h in heads:
        blocks = [_dot(a, vbs[h][:r1]) for a, (_, r1) in zip(scores[h], spans)]
        intra = blocks[0] if len(blocks) == 1 else jnp.concatenate(blocks, axis=0)
        outs.append(inter[h] + intra)
    return outs


def _tril_bf16(c):
    row = lax.broadcasted_iota(jnp.int32, (c, c), 0)
    col = lax.broadcasted_iota(jnp.int32, (c, c), 1)
    return jnp.where(col <= row, 1.0, 0.0).astype(BF16)


def _hgrn_kernel(*refs, chunk, sub, mod_splits, prep):
    n_in = 6 + (3 if mod_splits else 0) + (6 if prep else 0)
    ins, rest = refs[:n_in], list(refs[n_in:])
    q_ref, f_ref, i_ref, g_ref, lb_ref, gain_ref = ins[:6]
    o_ref = rest.pop(0)
    if mod_splits:
        c_ref, mw_ref, mb_ref = ins[6:9]
        mo_ref = rest.pop(0)
    if prep:
        qo_ref, ko_ref = rest.pop(0), rest.pop(0)
        s_ref, carry_ref = rest
        _fox_prep_kernel(*ins[n_in - 6:], qo_ref, ko_ref, carry_ref, heads=prep[0], head_dim=prep[1])
    else:
        (s_ref,) = rest

    @pl.when(pl.program_id(1) == 0)
    def _():
        s_ref[...] = jnp.zeros_like(s_ref)

    if mod_splits:
        step = pl.program_id(0) * pl.num_programs(1) + pl.program_id(1)
        row_block = step // mod_splits
        width = mo_ref.shape[1] // mod_splits
        cols = pl.ds(pl.multiple_of((step % mod_splits) * width, LANES), width)
        part = _dot(_silu(c_ref[...]).astype(BF16), mw_ref[0].astype(BF16))

        @pl.when(row_block == 0)
        def _():
            mo_ref[:, cols] = mb_ref[0, :, cols] + part

        @pl.when(row_block > 0)
        def _():
            mo_ref[:, cols] += part

    heads, dk, _ = s_ref.shape
    tril = _tril_bf16(chunk)

    def body(ci, carry):
        rows = pl.ds(pl.multiple_of(ci * chunk, chunk), chunk)
        lb = lb_ref[...]
        sig = _sigmoid(f_ref[rows, :].astype(F32))
        forget = lb + (1.0 - lb) * sig
        cum = _cumsum_rows(jnp.log2(jnp.maximum(forget, MIN_FORGET)), tril)
        key = (1.0 - lb) * (1.0 - sig)
        q = _silu(q_ref[rows, :].astype(F32))
        v = i_ref[rows, :]
        cols = [slice(hh * dk, (hh + 1) * dk) for hh in range(heads)]
        outs = _glr_heads([q[:, cs] for cs in cols], [key[:, cs] for cs in cols],
                          [v[:, cs] for cs in cols], [cum[:, cs] for cs in cols], s_ref, sub)
        for o, cs in zip(outs, cols):
            o = o * lax.rsqrt(jnp.mean(o * o, axis=-1, keepdims=True) + EPS) * gain_ref[:, cs]
            o_ref[rows, cs] = (o * _silu(g_ref[rows, cs].astype(F32))).astype(o_ref.dtype)
        return carry

    lax.fori_loop(0, q_ref.shape[0] // chunk, body, 0, unroll=4)


def hgrn2_heads(proj, lb, out_gain, batch, seq, mod_next=None, fox=None, tb=REC_TB,
                chunk=REC_CHUNK, sub=HGRN_SUB):
    m = proj.shape[0]
    tb = min(tb, seq)
    nt = seq // tb
    h, dk = A_HEADS, A_DK
    w = h * dk

    def col(group):
        return pl.BlockSpec((tb, w), lambda b, t: (b * nt + t, group))

    vec = pl.BlockSpec((1, w), lambda b, t: (0, 0))
    in_specs = [col(0), col(1), col(2), col(3), vec, vec]
    out_specs = [pl.BlockSpec((tb, w), lambda b, t: (b * nt + t, 0))]
    out_shape = [jax.ShapeDtypeStruct((m, w), BF16)]
    args = [proj, proj, proj, proj, lb.reshape(1, w), out_gain.reshape(1, w)]
    splits = 0
    if mod_next is not None:
        c_pad, mod_w, mod_b, nl = mod_next
        d, n = mod_w.shape[1], mod_w.shape[2]
        steps = batch * nt
        row_blocks = min(steps, d // LANES)
        splits = steps // row_blocks
        block_rows = d // row_blocks
        assert steps == row_blocks * splits and d == block_rows * row_blocks
        assert n % (splits * LANES) == 0
        in_specs += [
            pl.BlockSpec((c_pad.shape[0], block_rows), lambda b, t: (0, (b * nt + t) // splits)),
            pl.BlockSpec((1, block_rows, n // splits),
                         lambda b, t: (nl, (b * nt + t) // splits, (b * nt + t) % splits)),
            pl.BlockSpec((1, 1, n), lambda b, t: (nl, 0, 0)),
        ]
        out_specs += [pl.BlockSpec((c_pad.shape[0], n), lambda b, t: (0, 0))]
        out_shape += [jax.ShapeDtypeStruct((c_pad.shape[0], n), F32)]
        args += [c_pad, mod_w, mod_b]
    scratch = [pltpu.VMEM((h, dk, dk), F32)]
    prep = None
    if fox is not None:
        logits, f_bias, q_gain, k_gain = fox
        fh, hd = B_HEADS, B_HEAD_DIM
        fw = fh * hd
        base = (4 * w) // fw
        bias = jnp.zeros((1, LANES), F32).at[0, :fh].set(f_bias)
        in_specs += [
            pl.BlockSpec((tb, fw), lambda b, t: (b * nt + t, base)),
            pl.BlockSpec((tb, fw), lambda b, t: (b * nt + t, base + 1)),
            pl.BlockSpec((tb, LANES), lambda b, t: (b * nt + t, 0)),
            pl.BlockSpec((1, LANES), lambda b, t: (0, 0)),
            pl.BlockSpec((1, hd), lambda b, t: (0, 0)),
            pl.BlockSpec((1, hd), lambda b, t: (0, 0)),
        ]
        out_specs += [pl.BlockSpec((tb, 2 * fw), lambda b, t: (b * nt + t, 0))] * 2
        out_shape += [jax.ShapeDtypeStruct((m, 2 * fw), BF16)] * 2
        args += [proj, proj, logits, bias, q_gain.reshape(1, hd), k_gain.reshape(1, hd)]
        scratch += [pltpu.VMEM((8, LANES), F32)]
        prep = (fh, hd)
    outs = pl.pallas_call(
        functools.partial(_hgrn_kernel, chunk=chunk, sub=sub, mod_splits=splits, prep=prep),
        grid=(batch, nt),
        in_specs=in_specs,
        out_specs=out_specs,
        out_shape=out_shape,
        scratch_shapes=scratch,
        compiler_params=_params("arbitrary" if splits else "parallel", "arbitrary"),
        name="hgrn2_recurrence",
    )(*args)
    return outs if len(outs) > 1 else outs[0]


def _gla_kernel(q_ref, k_ref, v_ref, g_ref, low_ref, wup_ref, bias_ref, gain_ref, o_ref, s_ref,
                *, chunk, sub):
    @pl.when(pl.program_id(1) == 0)
    def _():
        s_ref[...] = jnp.zeros_like(s_ref)

    heads, dk, dv = s_ref.shape
    tril = _tril_bf16(chunk)
    q_scale = dk ** -0.5

    def body(ci, carry):
        rows = pl.ds(pl.multiple_of(ci * chunk, chunk), chunk)
        z = _dot(low_ref[rows, :].astype(BF16), wup_ref[...]) + bias_ref[...]
        cum = _cumsum_rows(_log_sigmoid(z) * (LOG2E / GLA_GATE_NORMALIZER), tril)
        kcs = [slice(hh * dk, (hh + 1) * dk) for hh in range(heads)]
        vcs = [slice(hh * dv, (hh + 1) * dv) for hh in range(heads)]
        outs = _glr_heads([q_ref[rows, kc].astype(F32) * q_scale for kc in kcs],
                          [k_ref[rows, kc].astype(F32) for kc in kcs],
                          [v_ref[rows, vc] for vc in vcs], [cum[:, kc] for kc in kcs], s_ref, sub)
        for o, vc in zip(outs, vcs):
            o = o * lax.rsqrt(jnp.mean(o * o, axis=-1, keepdims=True) + EPS) * gain_ref[...]
            o_ref[rows, vc] = (o * _silu(g_ref[rows, vc].astype(F32))).astype(o_ref.dtype)
        return carry

    lax.fori_loop(0, q_ref.shape[0] // chunk, body, 0)


def gla_heads(proj, low, w_up, b_gate, out_gain, batch, seq, tb=REC_TB, chunk=GLA_CHUNK,
              sub=GLA_SUB):
    m = proj.shape[0]
    tb = min(tb, seq)
    nt = seq // tb
    h, dk, dv = C_HEADS, C_DK, C_DV
    kw, vw = h * dk, h * dv
    v0 = 2 * kw // vw
    return pl.pallas_call(
        functools.partial(_gla_kernel, chunk=chunk, sub=sub),
        grid=(batch, nt),
        in_specs=[
            pl.BlockSpec((tb, kw), lambda b, t: (b * nt + t, 0)),
            pl.BlockSpec((tb, kw), lambda b, t: (b * nt + t, 1)),
            pl.BlockSpec((tb, vw), lambda b, t: (b * nt + t, v0)),
            pl.BlockSpec((tb, vw), lambda b, t: (b * nt + t, v0 + 1)),
            pl.BlockSpec((tb, LANES), lambda b, t: (b * nt + t, 0)),
            pl.BlockSpec((LANES, kw), lambda b, t: (0, 0)),
            pl.BlockSpec((1, kw), lambda b, t: (0, 0)),
            pl.BlockSpec((1, dv), lambda b, t: (0, 0)),
        ],
        out_specs=pl.BlockSpec((tb, vw), lambda b, t: (b * nt + t, 0)),
        out_shape=jax.ShapeDtypeStruct((m, vw), BF16),
        scratch_shapes=[pltpu.VMEM((h, dk, dv), F32)],
        compiler_params=_params("parallel", "arbitrary"),
        name="gla_recurrence",
    )(proj, proj, proj, proj, low, w_up, b_gate.reshape(1, kw), out_gain.reshape(1, dv))


F_PIECES = 3


def _fox_prep_kernel(q_ref, k_ref, logit_ref, bias_ref, qg_ref, kg_ref,
                     qo_ref, ko_ref, carry_ref, *, heads, head_dim):
    @pl.when(pl.program_id(1) == 0)
    def _():
        carry_ref[...] = jnp.zeros_like(carry_ref)

    tb = q_ref.shape[0]
    log_f = _log_sigmoid(logit_ref[...] + bias_ref[...])
    cum = _cumsum_rows(log_f, _tril_bf16(tb)) + carry_ref[0:1, :]
    carry_ref[...] = jnp.broadcast_to(cum[tb - 1:tb, :], carry_ref.shape)
    pieces = jnp.concatenate(_split3(-LOG2E * cum), axis=1)
    src = lax.broadcasted_iota(jnp.int32, (F_PIECES * LANES, heads * head_dim), 0)
    dst = lax.broadcasted_iota(jnp.int32, (F_PIECES * LANES, heads * head_dim), 1)
    select = jnp.where((dst // head_dim == src % LANES) & (dst % head_dim == src // LANES), 1.0, 0.0)
    bias_all = _dot(pieces, select.astype(BF16))
    lane = lax.broadcasted_iota(jnp.int32, (tb, head_dim), 1)
    one_cols = jnp.where(lane < F_PIECES, 1.0, 0.0).astype(qo_ref.dtype)

    qg = qg_ref[...] * (head_dim ** -0.5 * LOG2E)
    kg = kg_ref[...]
    for hh in range(heads):
        cols = slice(hh * head_dim, (hh + 1) * head_dim)
        lo = 2 * hh * head_dim
        main = slice(lo, lo + head_dim)
        extra = slice(lo + head_dim, lo + 2 * head_dim)
        q = q_ref[:, cols].astype(F32)
        qo_ref[:, main] = (q * lax.rsqrt(jnp.mean(q * q, axis=-1, keepdims=True) + EPS) * qg
                           ).astype(qo_ref.dtype)
        qo_ref[:, extra] = one_cols
        k = k_ref[:, cols].astype(F32)
        ko_ref[:, main] = (k * lax.rsqrt(jnp.mean(k * k, axis=-1, keepdims=True) + EPS) * kg
                           ).astype(ko_ref.dtype)
        ko_ref[:, extra] = bias_all[:, cols].astype(ko_ref.dtype)


def fox_prep(proj, logits, f_bias, q_gain, k_gain, batch, seq, tb=PREP_TB):
    m = proj.shape[0]
    h, hd = B_HEADS, B_HEAD_DIM
    w = h * hd
    tb = min(tb, seq)
    nt = seq // tb
    base = (4 * A_HEADS * A_DK) // w
    bias = jnp.zeros((1, LANES), F32).at[0, :h].set(f_bias)

    def col(group):
        return pl.BlockSpec((tb, w), lambda b, t: (b * nt + t, base + group))

    wide_tok = pl.BlockSpec((tb, 2 * w), lambda b, t: (b * nt + t, 0))
    return pl.pallas_call(
        functools.partial(_fox_prep_kernel, heads=h, head_dim=hd),
        grid=(batch, nt),
        in_specs=[
            col(0), col(1),
            pl.BlockSpec((tb, LANES), lambda b, t: (b * nt + t, 0)),
            pl.BlockSpec((1, LANES), lambda b, t: (0, 0)),
            pl.BlockSpec((1, hd), lambda b, t: (0, 0)),
            pl.BlockSpec((1, hd), lambda b, t: (0, 0)),
        ],
        out_specs=[wide_tok, wide_tok],
        out_shape=[jax.ShapeDtypeStruct((m, 2 * w), BF16)] * 2,
        scratch_shapes=[pltpu.VMEM((8, LANES), F32)],
        compiler_params=_params("parallel", "arbitrary"),
        name="fox_prep",
    )(proj, proj, logits, bias, q_gain.reshape(1, hd), k_gain.reshape(1, hd))


def _fox_attn_kernel(q_ref, k_ref, v_ref, g_ref, o_ref, m_ref, acc_ref, *, tq, tk, heads, head_dim):
    qi = pl.program_id(2)
    q_start = pl.multiple_of(qi * tq, tq)
    wide = 2 * head_dim

    def tile(start, mask, width=tk, first=False):
        rows = pl.ds(start, width)
        ones = jnp.ones((width, head_dim), BF16)
        scores = [_dot_nt(q_ref[:, hh * wide:(hh + 1) * wide], k_ref[rows, hh * wide:(hh + 1) * wide])
                  for hh in range(heads)]
        for hh, s in enumerate(scores):
            if mask is not None:
                s = jnp.where(mask, s, MASK_VALUE)
            row_max = jnp.max(s, axis=-1, keepdims=True)
            if first:
                m_new = jnp.broadcast_to(row_max, (tq, LANES))
            else:
                m_prev = m_ref[hh]
                m_new = jnp.maximum(m_prev, row_max)
            p = jnp.exp2(s - jnp.concatenate([m_new] * (width // LANES), axis=1))
            m_ref[hh] = m_new
            v_aug = jnp.concatenate([v_ref[rows, hh * head_dim:(hh + 1) * head_dim], ones], axis=1)
            pv = _dot(p.astype(BF16), v_aug)
            if first:
                acc_ref[hh] = pv
            else:
                alpha = jnp.exp2(m_prev - m_new)
                acc_ref[hh] = jnp.concatenate([alpha] * (wide // LANES), axis=1) * acc_ref[hh] + pv

    for d in range(tq // tk):
        row = lax.broadcasted_iota(jnp.int32, (tq, tk), 0)
        col = lax.broadcasted_iota(jnp.int32, (tq, tk), 1) + d * tk
        tile(pl.multiple_of(q_start + d * tk, tk), col <= row, first=(d == 0))

    n_full = qi * (tq // tk)

    def quad(gi, carry):
        tile(pl.multiple_of(4 * gi * tk, 4 * tk), None, width=4 * tk)
        return carry

    lax.fori_loop(0, n_full // 4, quad, 0)

    @pl.when(n_full % 4 >= 2)
    def _():
        tile(pl.multiple_of((n_full // 4) * 4 * tk, 2 * tk), None, width=2 * tk)

    @pl.when(n_full % 2 == 1)
    def _():
        tile(pl.multiple_of((n_full - 1) * tk, tk), None)

    for hh in range(heads):
        acc = acc_ref[hh]
        cols = slice(hh * head_dim, (hh + 1) * head_dim)
        o_ref[:, cols] = (acc[:, :head_dim] / acc[:, head_dim:]
                          * _sigmoid(g_ref[:, cols].astype(F32))).astype(o_ref.dtype)


def fox_attention(q_aug, k_aug, proj, batch, seq, tq=ATT_TQ, tk=ATT_TK, group=ATT_HEADS):
    m = q_aug.shape[0]
    h, hd = B_HEADS, B_HEAD_DIM
    tq = min(tq, seq)
    tk = min(tk, tq)
    nq = seq // tq
    v_block = (4 * A_HEADS * A_DK + 2 * h * hd) // (group * hd)
    gate_block = v_block + h // group
    return pl.pallas_call(
        functools.partial(_fox_attn_kernel, tq=tq, tk=tk, heads=group, head_dim=hd),
        grid=(batch, h // group, nq),
        in_specs=[
            pl.BlockSpec((tq, 2 * group * hd), lambda b, hg, i: (b * nq + i, hg)),
            pl.BlockSpec((seq, 2 * group * hd), lambda b, hg, i: (b, hg),
                         pipeline_mode=pl.Buffered(1)),
            pl.BlockSpec((seq, group * hd), lambda b, hg, i: (b, v_block + hg),
                         pipeline_mode=pl.Buffered(1)),
            pl.BlockSpec((tq, group * hd), lambda b, hg, i: (b * nq + i, gate_block + hg)),
        ],
        out_specs=pl.BlockSpec((tq, group * hd), lambda b, hg, i: (b * nq + i, hg)),
        out_shape=jax.ShapeDtypeStruct((m, h * hd), BF16),
        scratch_shapes=[pltpu.VMEM((group, tq, LANES), F32), pltpu.VMEM((group, tq, 2 * hd), F32)],
        compiler_params=_params("parallel", "parallel", "arbitrary"),
        name="fox_attention",
    )(q_aug, k_aug, proj, proj)


def _pad_cols(w, n):
    return jnp.zeros((w.shape[0], n), w.dtype).at[:, :w.shape[1]].set(w)


def kernel(x, c, mod_w, mod_b, norm_mix_gain, norm_ffn_gain, ab_w_in, ab_w_out, hgrn_lb_logits,
           hgrn_out_gain, fox_q_gain, fox_k_gain, fox_f_bias, gla_w_in, gla_w_gate_up, gla_b_gate,
           gla_out_gain, gla_w_out, ffn_w_in, ffn_w_out):
    batch, seq, d = x.shape
    depth = mod_w.shape[0]
    ab_main = 4 * A_HEADS * A_DK + 4 * B_HEADS * B_HEAD_DIM
    c_main = 2 * C_HEADS * C_DK + 2 * C_HEADS * C_DV

    lbs = lower_bounds(hgrn_lb_logits)
    c_pad = jnp.zeros((8, d), F32).at[:batch].set(c)
    mod_b3 = mod_b.reshape(depth, 1, mod_b.shape[1])
    mod_even = modulation(c_pad, mod_w, mod_b, stride=2)
    mod_odd = None
    xs = x.reshape(batch * seq, d)
    ab_w_in_b, ab_w_out_b = ab_w_in.astype(BF16), ab_w_out.astype(BF16)
    gla_w_in_b, gla_w_out_b = gla_w_in.astype(BF16), gla_w_out.astype(BF16)
    ffn_w_in_b = ffn_w_out_b = None

    for layer in range(depth):
        mod = mod_even[layer // 2] if layer % 2 == 0 else mod_odd
        sh1, sc1, g1, sh2, sc2, g2 = [
            mod[:batch, i * d:(i + 1) * d].reshape(batch, 1, d) for i in range(6)]
        if layer % 2 == 0:
            i = layer // 2
            w_aux = _pad_cols(ab_w_in[i, :, ab_main:], LANES).astype(BF16)
            if layer == 0:
                proj, logits, ffn_w_in_b, ffn_w_out_b = norm_proj(
                    xs, norm_mix_gain[layer], sh1, sc1, ab_w_in_b, i, ab_main, w_aux, seq,
                    side_cast=((ffn_w_in, ffn_w_out), 0))
            else:
                proj, logits = norm_proj(xs, norm_mix_gain[layer], sh1, sc1, ab_w_in_b, i, ab_main,
                                         w_aux, seq)
            fox = (logits, fox_f_bias[i], fox_q_gain[i], fox_k_gain[i])
            if layer + 1 < depth:
                o_a, mod_odd, q_aug, k_aug = hgrn2_heads(
                    proj, lbs[i], hgrn_out_gain[i], batch, seq,
                    mod_next=(c_pad, mod_w, mod_b3, layer + 1), fox=fox)
            else:
                o_a, q_aug, k_aug = hgrn2_heads(proj, lbs[i], hgrn_out_gain[i], batch, seq, fox=fox)
            o_b = fox_attention(q_aug, k_aug, proj, batch, seq)
            xs = out_proj_residual([o_a, o_b], ab_w_out_b, i, xs, g1, seq)
        else:
            j = layer // 2
            proj, low = norm_proj(xs, norm_mix_gain[layer], sh1, sc1, gla_w_in_b, j, c_main,
                                  _pad_cols(gla_w_in[j, :, c_main:], LANES).astype(BF16), seq)
            w_up = jnp.zeros((LANES, gla_w_gate_up.shape[2]), BF16).at[:C_GATE_RANK].set(
                gla_w_gate_up[j].astype(BF16))
            o_c = gla_heads(proj, low, w_up, gla_b_gate[j], gla_out_gain[j], batch, seq)
            xs = out_proj_residual([o_c], gla_w_out_b, j, xs, g1, seq)
        if layer + 1 < depth:
            xs, ffn_w_in_b, ffn_w_out_b = ffn_residual(
                xs, norm_ffn_gain[layer], sh2, sc2, g2, ffn_w_in_b, ffn_w_out_b, seq,
                next_weights=(ffn_w_in, ffn_w_out, layer + 1))
        else:
            xs = ffn_residual(xs, norm_ffn_gain[layer], sh2, sc2, g2, ffn_w_in_b, ffn_w_out_b, seq)
    return xs.reshape(batch, seq, d)
```
